```python
import math
import jax
import jax.numpy as jnp
from jax import lax
import numpy as np

D_MODEL = 1024
BATCH = 16
SEQ = 256
DEPTH = 4
DEC_BATCH = 8
DEC_SEQ = 1024
PAST_LEN = 256

GRID_W = 64
N_MOD = 6
EPS = 1e-6
CONV_W = 4
CONV_LEFT = 2
ROPE_BASE = 10000.0
H_RET = 8
DK_RET = D_MODEL // H_RET
DV_RET = D_MODEL // H_RET
RET_CHUNK = 128
D_LRU = D_MODEL
LRU_BLOCKS = 8
LRU_BS = D_LRU // LRU_BLOCKS
LRU_C = 8.0
H_DN = 8
DK_DN = D_MODEL // H_DN
DV_DN = D_MODEL // H_DN
DN_CHUNK = 64
PEER_HEADS = 8
PEER_DQ = D_MODEL // 4
N_KEYS = 128
N_EXPERTS = N_KEYS * N_KEYS
PEER_TOPK = 16
PEER_BLOCK = 128
IN_SPLITS = (H_RET * DK_RET, H_RET * DK_RET, H_RET * DV_RET, H_RET * DV_RET,
             D_LRU, D_LRU,
             H_DN * DK_DN, H_DN * DK_DN, H_DN * DV_DN, H_DN * DV_DN, H_DN, H_DN, H_DN, H_DN,
             3 * D_MODEL)
D_IN = sum(IN_SPLITS)

kernel_name = 'hybrid_diffusion_ret_rglru_gdn_peer_step'


def rms_norm(x, g):
    xf = x.astype(jnp.float32)
    y = xf * lax.rsqrt(jnp.mean(jnp.square(xf), axis=-1, keepdims=True) + EPS)
    return (y * g.astype(jnp.float32)).astype(x.dtype)


def l2norm(x):
    return x * lax.rsqrt(jnp.sum(jnp.square(x), axis=-1, keepdims=True) + EPS)


def dw_conv(x, w, b=None):
    L = x.shape[1]
    xp = jnp.pad(x, ((0, 0), (CONV_LEFT, CONV_W - 1 - CONV_LEFT), (0, 0)))
    y = xp[:, 0:L] * w[0]
    for j in range(1, CONV_W):
        y = y + xp[:, j:j + L] * w[j]
    return y if b is None else y + b


def grid_rope(L, dim):
    rows = L // GRID_W
    r = jnp.repeat(jnp.arange(rows, dtype=jnp.float32), GRID_W)
    col = jnp.tile(jnp.arange(GRID_W, dtype=jnp.float32), rows)
    nf = dim // 4
    inv = ROPE_BASE ** (-jnp.arange(nf, dtype=jnp.float32) / nf)
    ang = jnp.concatenate([r[:, None] * inv, col[:, None] * inv], axis=-1)
    return jnp.cos(ang), jnp.sin(ang)


def apply_rope(x, cos, sin):
    d2 = x.shape[-1] // 2
    x1, x2 = x[..., :d2], x[..., d2:]
    cos = cos[None, :, None, :]
    sin = sin[None, :, None, :]
    return jnp.concatenate([x1 * cos - x2 * sin, x1 * sin + x2 * cos], axis=-1)


def retention_chunked(q, k, v, log_gamma, s0):
    B, H, L, dk = q.shape
    dv = v.shape[-1]
    C = RET_CHUNK
    N = L // C
    q = q.reshape(B, H, N, C, dk)
    k = k.reshape(B, H, N, C, dk)
    v = v.reshape(B, H, N, C, dv)
    pos = jnp.arange(C, dtype=jnp.float32)
    lg = log_gamma[:, None]
    diff = pos[:, None] - pos[None, :]
    decay = jnp.where(diff >= 0, jnp.exp(lg[:, :, None] * jnp.maximum(diff, 0.0)), 0.0)
    scores = jnp.einsum('bhncd,bhnsd->bhncs', q, k) * decay[None, :, None]
    o_intra = jnp.einsum('bhncs,bhnsv->bhncv', scores, v)
    q_dec = q * jnp.exp(lg * (pos + 1.0))[None, :, None, :, None]
    k_dec = k * jnp.exp(lg * (C - 1.0 - pos))[None, :, None, :, None]
    chunk_kv = jnp.einsum('bhncd,bhncv->bhndv', k_dec, v)
    g_chunk = jnp.exp(log_gamma * C)[None, :, None, None]

    def step(s, inp):
        kv_n, q_n = inp
        o_n = jnp.einsum('bhcd,bhdv->bhcv', q_n, s)
        return s * g_chunk + kv_n, o_n

    s_final, o_inter = lax.scan(step, s0, (jnp.moveaxis(chunk_kv, 2, 0), jnp.moveaxis(q_dec, 2, 0)))
    o = o_intra + jnp.moveaxis(o_inter, 0, 2)
    return o.reshape(B, H, L, dv), s_final


def retention_bidir(q, k, v, log_gamma, s0):
    to_bh = lambda t: jnp.swapaxes(t.astype(jnp.float32), 1, 2)
    q, k, v = to_bh(q), to_bh(k), to_bh(v)
    s0 = s0.astype(jnp.float32)
    fl = lambda t: jnp.flip(t, axis=2)
    o_f, s_f = retention_chunked(q, k, v, log_gamma[0], s0[:, 0])
    o_b, s_b = retention_chunked(fl(q), fl(k), fl(v), log_gamma[1], s0[:, 1])
    return jnp.swapaxes(o_f + fl(o_b), 1, 2), jnp.stack([s_f, s_b], axis=1)


def linear_scan(a, b, h0):
    def comb(e1, e2):
        a1, b1 = e1
        a2, b2 = e2
        return a1 * a2, a2 * b1 + b2
    a_cum, h_zero = lax.associative_scan(comb, (a, b), axis=1)
    h = h_zero + a_cum * h0[:, None]
    return h, h[:, -1]


def gated_delta_chunked(q, k, v, g, beta, s0):
    B, H, L, dk = q.shape
    dv = v.shape[-1]
    C = DN_CHUNK
    N = L // C
    q = q.reshape(B, H, N, C, dk)
    k = k.reshape(B, H, N, C, dk)
    v = v.reshape(B, H, N, C, dv)
    gc = jnp.cumsum(g.reshape(B, H, N, C), axis=-1)
    beta = beta.reshape(B, H, N, C)
    incl = jnp.tril(jnp.ones((C, C), dtype=bool))
    strict = jnp.tril(jnp.ones((C, C), dtype=bool), -1)
    diff = gc[..., :, None] - gc[..., None, :]
    decay = jnp.where(incl, jnp.exp(jnp.where(incl, diff, 0.0)), 0.0)
    k_beta = k * beta[..., None]
    lmat = jnp.where(strict, jnp.einsum('bhnid,bhnjd->bhnij', k_beta, k) * decay, 0.0)
    rhs = jnp.concatenate([v * beta[..., None], k_beta * jnp.exp(gc)[..., None]], axis=-1)
    sol = lax.linalg.triangular_solve(lmat, rhs, left_side=True, lower=True, unit_diagonal=True)
    u, w = sol[..., :dv], sol[..., dv:]
    attn = jnp.where(incl, jnp.einsum('bhnid,bhnjd->bhnij', q, k) * decay, 0.0)
    q_dec = q * jnp.exp(gc)[..., None]
    k_tail = k * jnp.exp(gc[..., -1:] - gc)[..., None]
    g_last = jnp.exp(gc[..., -1])
    xs = tuple(jnp.moveaxis(t, 2, 0) for t in (q_dec, attn, u, w, k_tail, g_last))

    def step(S, inp):
        q_n, a_n, u_n, w_n, kt_n, gl_n = inp
        v_new = u_n - jnp.einsum('bhcd,bhde->bhce', w_n, S)
        o = jnp.einsum('bhcd,bhde->bhce', q_n, S) + jnp.einsum('bhcs,bhse->bhce', a_n, v_new)
        S = S * gl_n[..., None, None] + jnp.einsum('bhcd,bhce->bhde', kt_n, v_new)
        return S, o

    S_final, o = lax.scan(step, s0, xs)
    return jnp.moveaxis(o, 0, 2).reshape(B, H, L, dv), S_final


def delta_bidir(q, k, v, g, beta, s0):
    to_bh = lambda t: jnp.swapaxes(t, 1, 2)
    q, k, v = to_bh(q), to_bh(k), to_bh(v)
    gt = jnp.moveaxis(g, 1, -1)
    bt = jnp.moveaxis(beta, 1, -1)
    s0 = s0.astype(jnp.float32)
    fl = lambda t: jnp.flip(t, axis=2)
    o_f, s_f = gated_delta_chunked(q, k, v, gt[:, 0], bt[:, 0], s0[:, 0])
    o_b, s_b = gated_delta_chunked(fl(q), fl(k), fl(v), fl(gt[:, 1]), fl(bt[:, 1]), s0[:, 1])
    return jnp.swapaxes(o_f + fl(o_b), 1, 2), jnp.stack([s_f, s_b], axis=1)


def mixer(h, p, states, rope):
    f32 = jnp.float32
    B, L, _ = h.shape
    s_ret, s_lru, s_dn = states
    offs = np.cumsum(IN_SPLITS)[:-1].tolist()
    (rq, rk, rv, rg, lx, lgate, dq, dk, dv, dz, daf, dbf, dab, dbb, bgate) = jnp.split(h @ p['w_in'], offs, axis=-1)

    rq = rq.reshape(B, L, H_RET, DK_RET)
    rk = rk.reshape(B, L, H_RET, DK_RET) * (DK_RET ** -0.5)
    if rope is not None:
        rq = apply_rope(rq, rope[0], rope[1])
        rk = apply_rope(rk, rope[0], rope[1])
    rv = rv.reshape(B, L, H_RET, DV_RET)
    log_gamma = jax.nn.log_sigmoid(p['ret_gamma_logit'].astype(f32))
    o_ret, s_ret_new = retention_bidir(rq, rk, rv, log_gamma, s_ret)
    o_ret = rms_norm(o_ret, p['ret_norm_g']) * jax.nn.silu(rg.astype(f32)).reshape(B, L, H_RET, DV_RET)

    xc = dw_conv(lx, p['lru_conv_w'], p['lru_conv_b']).astype(f32)
    gp = jnp.einsum('blnc,dgncm->bldgnm', xc.reshape(B, L, LRU_BLOCKS, LRU_BS), p['lru_gate_w'].astype(f32))
    gt = jax.nn.sigmoid(gp.reshape(B, L, 2, 2, D_LRU) + p['lru_gate_b'].astype(f32))
    r_gate, i_gate = gt[:, :, :, 0], gt[:, :, :, 1]
    log_a = -LRU_C * r_gate * jax.nn.softplus(-p['lru_lambda'].astype(f32))
    a = jnp.exp(log_a)
    u = jnp.sqrt(-jnp.expm1(2.0 * log_a)) * i_gate * xc[:, :, None]
    s_lru = s_lru.astype(f32)
    h_f, hT_f = linear_scan(a[:, :, 0], u[:, :, 0], s_lru[:, 0])
    h_b, hT_b = linear_scan(jnp.flip(a[:, :, 1], 1), jnp.flip(u[:, :, 1], 1), s_lru[:, 1])
    o_lru = (h_f + jnp.flip(h_b, 1)) * jax.nn.gelu(lgate.astype(f32), approximate=False)
    s_lru_new = jnp.stack([hT_f, hT_b], axis=1)

    qkv = jax.nn.silu(dw_conv(jnp.concatenate([dq, dk, dv], axis=-1), p['dn_conv_w']).astype(f32))
    cq, ck, cv = jnp.split(qkv, [H_DN * DK_DN, 2 * H_DN * DK_DN], axis=-1)
    q = l2norm(cq.reshape(B, L, H_DN, DK_DN)) * (DK_DN ** -0.5)
    k = l2norm(ck.reshape(B, L, H_DN, DK_DN))
    v = cv.reshape(B, L, H_DN, DV_DN)
    a_in = jnp.stack([daf, dab], axis=2).astype(f32)
    b_in = jnp.stack([dbf, dbb], axis=2).astype(f32)
    g = -jnp.exp(p['dn_a_log'].astype(f32)) * jax.nn.softplus(a_in + p['dn_dt_bias'].astype(f32))
    beta = jax.nn.sigmoid(b_in)
    o_dn, s_dn_new = delta_bidir(q, k, v, g, beta, s_dn)
    o_dn = rms_norm(o_dn, p['dn_norm_g']) * jax.nn.silu(dz.astype(f32)).reshape(B, L, H_DN, DV_DN)

    gates = jax.nn.sigmoid(bgate.astype(f32)).reshape(B, L, 3, D_MODEL)
    br = jnp.stack([o_ret.reshape(B, L, -1), o_lru, o_dn.reshape(B, L, -1)], axis=2).astype(h.dtype)
    proj = jnp.einsum('blne,ned->blnd', br, p['w_br'])
    merged = jnp.sum(gates * proj, axis=2).astype(h.dtype)
    return merged @ p['w_out'], (s_ret_new, s_lru_new, s_dn_new)


def peer(h, w_q, sub_keys, u_tab, v_tab):
    B, L, D = h.shape
    T = B * L
    t = h.reshape(T, D)
    q = (t @ w_q).reshape(T, PEER_HEADS, 2, PEER_DQ // 2)
    s = jnp.einsum('thpd,pkd->thpk', q, sub_keys).astype(jnp.float32)
    s_top, i_top = lax.top_k(s, PEER_TOPK)
    cand_s = (s_top[:, :, 0, :, None] + s_top[:, :, 1, None, :]).reshape(T, PEER_HEADS, PEER_TOPK * PEER_TOPK)
    cand_i = (i_top[:, :, 0, :, None] * N_KEYS + i_top[:, :, 1, None, :]).reshape(T, PEER_HEADS, PEER_TOPK * PEER_TOPK)
    best_s, best_pos = lax.top_k(cand_s, PEER_TOPK)
    idx = jnp.take_along_axis(cand_i, best_pos, axis=-1)
    wts = jax.nn.softmax(best_s, axis=-1)
    nb = T // PEER_BLOCK

    def expert_block(args):
        tb, ib, wb = args
        act = jnp.einsum('td,thkd->thk', tb, u_tab[ib]).astype(jnp.float32)
        coef = (wb * jax.nn.gelu(act, approximate=False)).astype(tb.dtype)
        return jnp.einsum('thk,thkd->td', coef, v_tab[ib])

    out = lax.map(expert_block, (t.reshape(nb, PEER_BLOCK, D),
                                 idx.reshape(nb, PEER_BLOCK, PEER_HEADS, PEER_TOPK),
                                 wts.reshape(nb, PEER_BLOCK, PEER_HEADS, PEER_TOPK)))
    return out.reshape(B, L, D).astype(h.dtype)


def ada_mod(cond, w, b):
    m = jax.nn.silu(cond) @ w + b
    return m.reshape(cond.shape[0], 1, N_MOD, D_MODEL)


def layer(x, m, p, states, rope):
    h = (rms_norm(x, p['norm1_g']) * (1.0 + m[:, :, 1]) + m[:, :, 0]).astype(x.dtype)
    mix, new_states = mixer(h, p, states, rope)
    x = (x + m[:, :, 2] * mix).astype(x.dtype)
    h = (rms_norm(x, p['norm2_g']) * (1.0 + m[:, :, 4]) + m[:, :, 3]).astype(x.dtype)
    x = (x + m[:, :, 5] * peer(h, p['peer_w_q'], p['peer_sub_keys'], p['peer_u'], p['peer_v'])).astype(x.dtype)
    return x, new_states


def setup_inputs(seed: int = 0) -> dict:
    key = jax.random.key(seed)
    ks = jax.random.split(key, 40)
    f32 = jnp.float32
    D = D_MODEL
    nrm = lambda k, shape, scale: scale * jax.random.normal(k, shape, f32)
    uni = lambda k, shape, lo, hi: jax.random.uniform(k, shape, f32, lo, hi)

    x_prompt = nrm(ks[0], (BATCH, SEQ, D), 1.0)
    x_sample = nrm(ks[1], (DEC_BATCH, DEC_SEQ, D), 1.0)
    c = nrm(ks[2], (DEC_BATCH, D), 1.0)
    state_ret = nrm(ks[3], (DEC_BATCH, DEPTH, 2, H_RET, DK_RET, DV_RET), 0.1)
    state_lru = nrm(ks[4], (DEC_BATCH, DEPTH, 2, D_LRU), 0.5)
    state_dn = nrm(ks[5], (DEC_BATCH, DEPTH, 2, H_DN, DK_DN, DV_DN), 0.1)
    c_ctx = nrm(ks[6], (D,), 1.0)

    w_mod = nrm(ks[7], (DEPTH, D, N_MOD * D), 0.5 * D ** -0.5)
    b_mod = nrm(ks[8], (DEPTH, N_MOD * D), 0.02)
    norm1_g = 1.0 + nrm(ks[9], (DEPTH, D), 0.02)
    norm2_g = 1.0 + nrm(ks[10], (DEPTH, D), 0.02)
    w_in = nrm(ks[11], (DEPTH, D, D_IN), D ** -0.5)
    hidx = jnp.arange(H_RET, dtype=f32)
    gam = 1.0 - 2.0 ** (-5.0 - hidx)
    ret_logit0 = jnp.log(gam) + (5.0 + hidx) * math.log(2.0)
    ret_gamma_logit = ret_logit0[None, None, :] + nrm(ks[12], (DEPTH, 2, H_RET), 0.1)
    ret_norm_g = 1.0 + nrm(ks[13], (DEPTH, H_RET, DV_RET), 0.02)
    lru_conv_w = nrm(ks[14], (DEPTH, CONV_W, D_LRU), CONV_W ** -0.5)
    lru_conv_b = nrm(ks[15], (DEPTH, D_LRU), 0.02)
    lru_gate_w = nrm(ks[16], (DEPTH, 2, 2, LRU_BLOCKS, LRU_BS, LRU_BS), LRU_BS ** -0.5)
    lru_gate_b = nrm(ks[17], (DEPTH, 2, 2, D_LRU), 0.02)
    a_pow = uni(ks[18], (DEPTH, 2, D_LRU), 0.9, 0.999)
    sig = a_pow ** (1.0 / LRU_C)
    lru_lambda = jnp.log(sig) - jnp.log1p(-sig)
    dn_conv_w = nrm(ks[19], (DEPTH, CONV_W, 2 * H_DN * DK_DN + H_DN * DV_DN), CONV_W ** -0.5)
    dn_a_log = jnp.log(uni(ks[20], (DEPTH, 2, H_DN), 1.0, 16.0))
    dt = jnp.exp(uni(ks[21], (DEPTH, 2, H_DN), math.log(1e-3), math.log(1e-1)))
    dn_dt_bias = dt + jnp.log(-jnp.expm1(-dt))
    dn_norm_g = 1.0 + nrm(ks[22], (DEPTH, H_DN, DV_DN), 0.02)
    w_br = nrm(ks[23], (DEPTH, 3, D_MODEL, D), D_MODEL ** -0.5)
    w_out = nrm(ks[24], (DEPTH, D, D), D ** -0.5)
    peer_w_q = nrm(ks[25], (DEPTH, D, PEER_HEADS * PEER_DQ), D ** -0.5)
    peer_sub_keys = nrm(ks[26], (DEPTH, 2, N_KEYS, PEER_DQ // 2), (PEER_DQ // 2) ** -0.5)
    peer_u = nrm(ks[27], (DEPTH, N_EXPERTS, D), D ** -0.5)
    peer_v = nrm(ks[28], (DEPTH, N_EXPERTS, D), PEER_HEADS ** -0.5)
    final_norm_g = 1.0 + nrm(ks[29], (D,), 0.02)
    return {'x_prompt': x_prompt, 'x_sample': x_sample, 'c': c,
            'state_ret': state_ret, 'state_lru': state_lru, 'state_dn': state_dn,
            'c_ctx': c_ctx, 'w_mod': w_mod, 'b_mod': b_mod, 'norm1_g': norm1_g, 'norm2_g': norm2_g,
            'w_in': w_in, 'ret_gamma_logit': ret_gamma_logit, 'ret_norm_g': ret_norm_g,
            'lru_conv_w': lru_conv_w, 'lru_conv_b': lru_conv_b, 'lru_gate_w': lru_gate_w,
            'lru_gate_b': lru_gate_b, 'lru_lambda': lru_lambda, 'dn_conv_w': dn_conv_w,
            'dn_a_log': dn_a_log, 'dn_dt_bias': dn_dt_bias, 'dn_norm_g': dn_norm_g,
            'w_br': w_br, 'w_out': w_out, 'peer_w_q': peer_w_q, 'peer_sub_keys': peer_sub_keys,
            'peer_u': peer_u, 'peer_v': peer_v, 'final_norm_g': final_norm_g}


def reference(x_prompt, x_sample, c, state_ret, state_lru, state_dn, c_ctx, w_mod, b_mod, norm1_g, norm2_g,
              w_in, ret_gamma_logit, ret_norm_g, lru_conv_w, lru_conv_b, lru_gate_w, lru_gate_b, lru_lambda,
              dn_conv_w, dn_a_log, dn_dt_bias, dn_norm_g, w_br, w_out, peer_w_q, peer_sub_keys, peer_u, peer_v,
              final_norm_g):
    f32 = jnp.float32
    n_ctx = x_prompt.shape[0]
    rope = grid_rope(x_sample.shape[1], DK_RET)
    zero_states = (jnp.zeros((n_ctx, 2, H_RET, DK_RET, DV_RET), f32),
                   jnp.zeros((n_ctx, 2, D_LRU), f32),
                   jnp.zeros((n_ctx, 2, H_DN, DK_DN, DV_DN), f32))
    xp, xs = x_prompt, x_sample
    ret_states, lru_states, dn_states = [], [], []
    for l in range(DEPTH):
        p = dict(norm1_g=norm1_g[l], norm2_g=norm2_g[l], w_in=w_in[l], ret_gamma_logit=ret_gamma_logit[l],
                 ret_norm_g=ret_norm_g[l], lru_conv_w=lru_conv_w[l], lru_conv_b=lru_conv_b[l],
                 lru_gate_w=lru_gate_w[l], lru_gate_b=lru_gate_b[l], lru_lambda=lru_lambda[l],
                 dn_conv_w=dn_conv_w[l], dn_a_log=dn_a_log[l], dn_dt_bias=dn_dt_bias[l], dn_norm_g=dn_norm_g[l],
                 w_br=w_br[l], w_out=w_out[l], peer_w_q=peer_w_q[l], peer_sub_keys=peer_sub_keys[l],
                 peer_u=peer_u[l], peer_v=peer_v[l])
        xp, (s_r, s_l, s_d) = layer(xp, ada_mod(c_ctx[None], w_mod[l], b_mod[l]), p, zero_states, None)
        ret_states.append(s_r)
        lru_states.append(s_l)
        dn_states.append(s_d)
        xs, _ = layer(xs, ada_mod(c, w_mod[l], b_mod[l]), p,
                      (state_ret[:, l], state_lru[:, l], state_dn[:, l]), rope)
    y_prompt = rms_norm(xp, final_norm_g)
    y_sample = rms_norm(xs, final_norm_g)
    return (y_prompt, y_sample, jnp.stack(ret_states, axis=1), jnp.stack(lru_states, axis=1), jnp.stack(dn_states, axis=1))
```

```python
import functools
import math

import jax
import jax.numpy as jnp
from jax import lax
from jax.experimental import pallas as pl
from jax.experimental.pallas import tpu as pltpu

F32 = jnp.float32
BF16 = jnp.bfloat16

D_MODEL = 1024
DEPTH = 4
N_MOD = 6
EPS = 1e-6
GRID_W = 64
ROPE_BASE = 10000.0
N_HEADS = 8
HEAD_DIM = 128
RET_CHUNK = 128
DN_CHUNK = 64
DN_SUPER = 256
LRU_C = 8.0
N_KEYS = 128
PEER_TOPK = 16
N_EXPERTS = N_KEYS * N_KEYS
SUBLANES = 8
TOP_N = PEER_TOPK + 1
TOP_PAD = -(-TOP_N // SUBLANES) * SUBLANES
NEG_BIG = -3.0e38

COL_RQ, COL_RK, COL_RV, COL_RG = 0, 8, 16, 24
COL_LX, COL_LG = 32, 40
COL_DQ, COL_DK, COL_DV, COL_DZ = 48, 56, 64, 72
N_MAIN = 80 * 128
N_SMALL = 4 * N_HEADS
COL_BGATE = 80
COL_SMALL = 104
N_PROJ = 105 * 128

VMEM_LIMIT = 48 * 1024 * 1024


def _cparams(sem):
    return pltpu.CompilerParams(dimension_semantics=sem, vmem_limit_bytes=VMEM_LIMIT)


def _mm(a, b):
    return jnp.dot(a.astype(BF16), b.astype(BF16), preferred_element_type=F32)


def _mm_nt(a, b):
    return lax.dot_general(a.astype(BF16), b.astype(BF16), (((1,), (1,)), ((), ())),
                           preferred_element_type=F32)


def _mm_tn(a, b):
    return lax.dot_general(a.astype(BF16), b.astype(BF16), (((0,), (0,)), ((), ())),
                           preferred_element_type=F32)


def _softplus(x):
    return jnp.maximum(x, 0.0) + jnp.log1p(jnp.exp(-jnp.abs(x)))


def _silu(x):
    return x * jax.nn.sigmoid(x)


def _gelu(x):
    return 0.5 * x * (1.0 + lax.erf(x * math.sqrt(0.5)))


def _rms(x):
    return x * lax.rsqrt(jnp.mean(x * x, axis=-1, keepdims=True) + EPS)


def _shift_rows(x, s, row):
    n = x.shape[0]
    if s == 0:
        return x
    y = pltpu.roll(x, (-s) % n, 0)
    ok = (row + s >= 0) & (row + s < n)
    return jnp.where(ok, y, 0.0)


def _dw_conv(x, w, row):
    y = _shift_rows(x, -2, row) * w[0:1, :]
    y = y + _shift_rows(x, -1, row) * w[1:2, :]
    y = y + x * w[2:3, :]
    y = y + _shift_rows(x, 1, row) * w[3:4, :]
    return y


def _mod_kernel(c_ref, w_ref, b_ref, o_ref):
    c = c_ref[...]
    o_ref[...] = jnp.dot(_silu(c), w_ref[...], precision=lax.Precision.HIGHEST,
                         preferred_element_type=F32) + b_ref[...]


def _modulation(cond, w_mod, b_mod):
    n_rows = cond.shape[0]
    tn = 1536
    n_out = N_MOD * D_MODEL
    return pl.pallas_call(
        _mod_kernel,
        grid=(DEPTH, n_out // tn),
        in_specs=[pl.BlockSpec((n_rows, D_MODEL), lambda l, j: (0, 0)),
                  pl.BlockSpec((None, D_MODEL, tn), lambda l, j: (l, 0, j)),
                  pl.BlockSpec((None, 1, tn), lambda l, j: (l, 0, j))],
        out_specs=pl.BlockSpec((None, n_rows, tn), lambda l, j: (l, 0, j)),
        out_shape=jax.ShapeDtypeStruct((DEPTH, n_rows, n_out), F32),
        compiler_params=_cparams(("parallel", "parallel")),
        name="modulation",
    )(cond, w_mod, b_mod.reshape(DEPTH, 1, n_out))


def _in_proj_kernel(x_ref, mod_ref, g_ref, w_ref, o_ref, h_scr):
    @pl.when(pl.program_id(1) == 0)
    def _():
        y = _rms(x_ref[...]) * g_ref[...]
        h_scr[...] = (y * (1.0 + mod_ref[1:2, :]) + mod_ref[0:1, :]).astype(BF16)

    o_ref[...] = jnp.dot(h_scr[...], w_ref[...], preferred_element_type=F32)


def _in_proj(x, mod, norm_g, w, mod_row, tm):
    t = x.shape[0]
    tn = 1920
    return pl.pallas_call(
        _in_proj_kernel,
        grid=(t // tm, N_PROJ // tn),
        in_specs=[pl.BlockSpec((tm, D_MODEL), lambda i, j: (i, 0)),
                  pl.BlockSpec((None, N_MOD, D_MODEL), lambda i, j: (mod_row(i), 0, 0)),
                  pl.BlockSpec((1, D_MODEL), lambda i, j: (0, 0)),
                  pl.BlockSpec((D_MODEL, tn), lambda i, j: (0, j))],
        out_specs=pl.BlockSpec((tm, tn), lambda i, j: (i, j)),
        out_shape=jax.ShapeDtypeStruct((t, N_PROJ), F32),
        scratch_shapes=[pltpu.VMEM((tm, D_MODEL), BF16)],
        compiler_params=_cparams(("parallel", "arbitrary")),
        name="in_proj",
    )(x, mod, norm_g, w)


def _ret_kernel(*refs, seq_len, rope):
    if rope:
        (q_ref, k_ref, v_ref, g_ref, gam_ref, ng_ref, s0_ref, cs_ref, sn_ref,
         o_ref, so_ref, q_scr, k_scr) = refs
    else:
        (q_ref, k_ref, v_ref, g_ref, gam_ref, ng_ref, s0_ref,
         o_ref, so_ref, q_scr, k_scr) = refs
    c = RET_CHUNK
    n_chunks = seq_len // c
    gam = gam_ref[...]
    lg = -_softplus(-gam)
    lgf, lgb = lg[0:1, :], lg[1:2, :]
    r = lax.broadcasted_iota(jnp.int32, (c, HEAD_DIM), 0).astype(F32)
    ci = lax.broadcasted_iota(jnp.int32, (c, c), 0)
    si = lax.broadcasted_iota(jnp.int32, (c, c), 1)
    dmat = (ci - si).astype(F32)
    dec_f = jnp.where(dmat >= 0, jnp.exp(lgf * jnp.maximum(dmat, 0.0)), 0.0)
    dec_b = jnp.where(dmat <= 0, jnp.exp(lgb * jnp.maximum(-dmat, 0.0)), 0.0)
    qsc_f, ksc_f, gch_f = jnp.exp(lgf * (r + 1.0)), jnp.exp(lgf * (c - 1.0 - r)), jnp.exp(lgf * c)
    qsc_b, ksc_b, gch_b = jnp.exp(lgb * (c - r)), jnp.exp(lgb * r), jnp.exp(lgb * c)
    scale = HEAD_DIM ** -0.5

    s = s0_ref[0]
    for n in range(n_chunks):
        sl = pl.ds(n * c, c)
        q = q_ref[sl, :]
        k = k_ref[sl, :] * scale
        v = v_ref[sl, :]
        if rope:
            cs, sn = cs_ref[sl, :], sn_ref[sl, :]
            q = q * cs + pltpu.roll(q, HEAD_DIM // 2, 1) * sn
            k = k * cs + pltpu.roll(k, HEAD_DIM // 2, 1) * sn
        q_scr[sl, :] = q
        k_scr[sl, :] = k
        o = _mm(_mm_nt(q, k) * dec_f, v) + _mm(q * qsc_f, s)
        s = s * gch_f + _mm_tn(k * ksc_f, v)
        o_ref[sl, :] = o
    so_ref[0] = s

    s = s0_ref[1]
    ng = ng_ref[...]
    for n in reversed(range(n_chunks)):
        sl = pl.ds(n * c, c)
        q, k, v = q_scr[sl, :], k_scr[sl, :], v_ref[sl, :]
        o = _mm(_mm_nt(q, k) * dec_b, v) + _mm(q * qsc_b, s)
        s = s * gch_b + _mm_tn(k * ksc_b, v)
        tot = o_ref[sl, :] + o
        o_ref[sl, :] = _rms(tot) * ng * _silu(g_ref[sl, :])
    so_ref[1] = s


def _retention(proj, gam, norm_g, s0, s0_map, rope_tabs, n_seq, seq_len, row_off):
    rope = rope_tabs is not None

    def col(c0):
        return pl.BlockSpec((seq_len, HEAD_DIM), lambda b, h: (row_off + b, c0 + h))

    in_specs = [col(COL_RQ), col(COL_RK), col(COL_RV), col(COL_RG),
                pl.BlockSpec((None, 2, HEAD_DIM), lambda b, h: (h, 0, 0)),
                pl.BlockSpec((None, 1, HEAD_DIM), lambda b, h: (h, 0, 0)),
                s0_map]
    args = [proj, proj, proj, proj, gam, norm_g, s0]
    if rope:
        tab = pl.BlockSpec((seq_len, HEAD_DIM), lambda b, h: (0, 0))
        in_specs += [tab, tab]
        args += list(rope_tabs)
    return pl.pallas_call(
        functools.partial(_ret_kernel, seq_len=seq_len, rope=rope),
        grid=(n_seq, N_HEADS),
        in_specs=in_specs,
        out_specs=[pl.BlockSpec((seq_len, HEAD_DIM), lambda b, h: (b, h)),
                   pl.BlockSpec((None, 2, None, HEAD_DIM, HEAD_DIM), lambda b, h: (b, 0, h, 0, 0))],
        out_shape=[jax.ShapeDtypeStruct((n_seq * seq_len, D_MODEL), F32),
                   jax.ShapeDtypeStruct((n_seq, 2, N_HEADS, HEAD_DIM, HEAD_DIM), F32)],
        scratch_shapes=[pltpu.VMEM((seq_len, HEAD_DIM), F32), pltpu.VMEM((seq_len, HEAD_DIM), F32)],
        compiler_params=_cparams(("parallel", "parallel")),
        name="retention_rope" if rope else "retention",
    )(*args)


def _lru_kernel(x_ref, gate_ref, cw_ref, cb_ref, gw_ref, gb_ref, lam_ref, s0_ref, o_ref, so_ref, *, seq_len):
    n = seq_len
    row = lax.broadcasted_iota(jnp.int32, (n, HEAD_DIM), 0)
    xc = _dw_conv(x_ref[...], cw_ref[...], row) + cb_ref[...]
    lam = lam_ref[...]
    hs = []
    for d in range(2):
        r_gate = jax.nn.sigmoid(_mm(xc, gw_ref[d, 0]) + gb_ref[d, 0:1, :])
        i_gate = jax.nn.sigmoid(_mm(xc, gw_ref[d, 1]) + gb_ref[d, 1:2, :])
        log_a = -LRU_C * r_gate * _softplus(-lam[d:d + 1, :])
        a = jnp.exp(log_a)
        u = jnp.sqrt(-jnp.tanh(log_a) * (1.0 + a * a)) * i_gate * xc
        step = 1
        while step < n:
            if d == 0:
                ok = row >= step
                sh = step
            else:
                ok = row < n - step
                sh = n - step
            a_sh = jnp.where(ok, pltpu.roll(a, sh, 0), 1.0)
            u_sh = jnp.where(ok, pltpu.roll(u, sh, 0), 0.0)
            u = a * u_sh + u
            a = a * a_sh
            step *= 2
        h = u + a * s0_ref[d:d + 1, :]
        hs.append(h)
        so_ref[d:d + 1, :] = h[n - 1:n, :] if d == 0 else h[0:1, :]
    o_ref[...] = (hs[0] + hs[1]) * _gelu(gate_ref[...])


def _rglru(proj, conv_w, conv_b, gate_w, gate_b, lam, s0, s0_map, n_seq, seq_len, row_off):
    return pl.pallas_call(
        functools.partial(_lru_kernel, seq_len=seq_len),
        grid=(n_seq, N_HEADS),
        in_specs=[pl.BlockSpec((seq_len, HEAD_DIM), lambda b, n: (row_off + b, COL_LX + n)),
                  pl.BlockSpec((seq_len, HEAD_DIM), lambda b, n: (row_off + b, COL_LG + n)),
                  pl.BlockSpec((4, HEAD_DIM), lambda b, n: (0, n)),
                  pl.BlockSpec((1, HEAD_DIM), lambda b, n: (0, n)),
                  pl.BlockSpec((2, 2, None, HEAD_DIM, HEAD_DIM), lambda b, n: (0, 0, n, 0, 0)),
                  pl.BlockSpec((2, 2, HEAD_DIM), lambda b, n: (0, 0, n)),
                  pl.BlockSpec((2, HEAD_DIM), lambda b, n: (0, n)),
                  s0_map],
        out_specs=[pl.BlockSpec((seq_len, HEAD_DIM), lambda b, n: (b, n)),
                   pl.BlockSpec((None, 2, HEAD_DIM), lambda b, n: (b, 0, n))],
        out_shape=[jax.ShapeDtypeStruct((n_seq * seq_len, D_MODEL), F32),
                   jax.ShapeDtypeStruct((n_seq, 2, D_MODEL), F32)],
        compiler_params=_cparams(("parallel", "parallel")),
        name="rglru",
    )(proj, proj, conv_w, conv_b, gate_w, gate_b, lam, s0)


def _dn_kernel(q_ref, k_ref, v_ref, z_ref, sm_ref, cwq_ref, cwk_ref, cwv_ref, par_ref, ng_ref, s0_ref,
               o_ref, so_ref, q_scr, k_scr, v_scr, c_scr, b_scr, of_scr, ob_scr, *, seq_len):
    n = seq_len
    cc = DN_CHUNK
    sc = DN_SUPER
    n_super = n // sc
    head = pl.program_id(1)
    row = lax.broadcasted_iota(jnp.int32, (n, HEAD_DIM), 0)
    lane = lax.broadcasted_iota(jnp.int32, (n, HEAD_DIM), 1)
    pos = row & (cc - 1)

    xq = _silu(_dw_conv(q_ref[...], cwq_ref[...], row))
    xk = _silu(_dw_conv(k_ref[...], cwk_ref[...], row))
    xv = _silu(_dw_conv(v_ref[...], cwv_ref[...], row))
    q_scr[...] = xq * lax.rsqrt(jnp.sum(xq * xq, axis=-1, keepdims=True) + EPS) * (HEAD_DIM ** -0.5)
    k_scr[...] = xk * lax.rsqrt(jnp.sum(xk * xk, axis=-1, keepdims=True) + EPS)
    v_scr[...] = xv

    small = sm_ref[...]
    par = par_ref[...]
    for d in range(2):
        a_in = jnp.sum(jnp.where(lane == 2 * N_HEADS * d + head, small, 0.0), axis=-1, keepdims=True)
        b_in = jnp.sum(jnp.where(lane == 2 * N_HEADS * d + N_HEADS + head, small, 0.0), axis=-1, keepdims=True)
        g = -jnp.exp(par[d:d + 1, :]) * _softplus(a_in + par[2 + d:3 + d, :])
        b_scr[d] = jnp.broadcast_to(jax.nn.sigmoid(b_in), (n, HEAD_DIM))
        step = 1
        while step < cc:
            if d == 0:
                g = g + jnp.where(pos >= step, pltpu.roll(g, step, 0), 0.0)
            else:
                g = g + jnp.where(pos < cc - step, pltpu.roll(g, n - step, 0), 0.0)
            step *= 2
        c_scr[d] = g

    ri = lax.broadcasted_iota(jnp.int32, (sc, sc), 0)
    cj = lax.broadcasted_iota(jnp.int32, (sc, sc), 1)
    sh = cc.bit_length() - 1
    same = (ri >> sh) == (cj >> sh)
    incl = (same & (ri >= cj), same & (ri <= cj))
    strict = (same & (ri > cj), same & (ri < cj))
    eye = jnp.where(ri == cj, 1.0, 0.0)
    level = [(ri >> 3) == (cj >> 3)]
    for b in range(4, sh + 1):
        level.append(((ri >> b) == (cj >> b)) & ((ri >> (b - 1)) != (cj >> (b - 1))))

    def super_chunk(d, base, s):
        sl = pl.ds(base, sc)
        q, k, v = q_scr[sl, :], k_scr[sl, :], v_scr[sl, :]
        cum = c_scr[d, sl, :]
        beta = b_scr[d, sl, :]
        cb = jnp.concatenate([cum, cum], axis=1)
        diff = cb - cb.T
        decay = jnp.where(incl[d], jnp.exp(jnp.where(incl[d], diff, 0.0)), 0.0)
        bb = jnp.concatenate([beta, beta], axis=1)
        x = -jnp.where(strict[d], _mm_nt(k, k) * bb * decay, 0.0)
        attn = _mm_nt(q, k) * decay
        xd = jnp.where(level[0], x, 0.0)
        p = eye + xd
        xp = xd
        for _ in range(2):
            xp = _mm(xp, xp)
            p = p + _mm(p, xp)
        for lv in range(1, len(level)):
            xo = jnp.where(level[lv], x, 0.0)
            p = p + _mm(p, _mm(xo, p))
        kb = k * beta
        sol = _mm(p, jnp.concatenate([v * beta, kb * jnp.exp(cum)], axis=1))
        u, w = sol[:, :HEAD_DIM], sol[:, HEAD_DIM:]
        qd = q * jnp.exp(cum)
        order = range(sc // cc) if d == 0 else reversed(range(sc // cc))
        v_new = [None] * (sc // cc)
        o_inter = [None] * (sc // cc)
        for ch in order:
            rs = slice(ch * cc, (ch + 1) * cc)
            cum_c = cum[rs, :]
            tot = cum_c[cc - 1:cc, :] if d == 0 else cum_c[0:1, :]
            vn = u[rs, :] - _mm(w[rs, :], s)
            o_inter[ch] = _mm(qd[rs, :], s)
            s = s * jnp.exp(tot) + _mm_tn(k[rs, :] * jnp.exp(tot - cum_c), vn)
            v_new[ch] = vn
        o = jnp.concatenate(o_inter, axis=0) + _mm(attn, jnp.concatenate(v_new, axis=0))
        if d == 0:
            of_scr[sl, :] = o
        else:
            ob_scr[sl, :] = o
        return s

    def body(i, carry):
        sf, sb = carry
        sf = super_chunk(0, pl.multiple_of(i * sc, sc), sf)
        sb = super_chunk(1, pl.multiple_of((n_super - 1 - i) * sc, sc), sb)
        return sf, sb

    sf, sb = lax.fori_loop(0, n_super, body, (s0_ref[0], s0_ref[1]))
    so_ref[0] = sf
    so_ref[1] = sb
    o_ref[...] = _rms(of_scr[...] + ob_scr[...]) * ng_ref[...] * _silu(z_ref[...])


def _deltanet(proj, conv_w, par, norm_g, s0, s0_map, n_seq, seq_len, row_off):

    def col(c0):
        return pl.BlockSpec((seq_len, HEAD_DIM), lambda b, h: (row_off + b, c0 + h))

    def cw(c0):
        return pl.BlockSpec((4, HEAD_DIM), lambda b, h: (0, c0 + h))

    vm = functools.partial(pltpu.VMEM, dtype=F32)
    return pl.pallas_call(
        functools.partial(_dn_kernel, seq_len=seq_len),
        grid=(n_seq, N_HEADS),
        in_specs=[col(COL_DQ), col(COL_DK), col(COL_DV), col(COL_DZ),
                  pl.BlockSpec((seq_len, HEAD_DIM), lambda b, h: (row_off + b, COL_SMALL)),
                  cw(0), cw(N_HEADS), cw(2 * N_HEADS),
                  pl.BlockSpec((None, 4, HEAD_DIM), lambda b, h: (h, 0, 0)),
                  pl.BlockSpec((None, 1, HEAD_DIM), lambda b, h: (h, 0, 0)),
                  s0_map],
        out_specs=[pl.BlockSpec((seq_len, HEAD_DIM), lambda b, h: (b, h)),
                   pl.BlockSpec((None, 2, None, HEAD_DIM, HEAD_DIM), lambda b, h: (b, 0, h, 0, 0))],
        out_shape=[jax.ShapeDtypeStruct((n_seq * seq_len, D_MODEL), F32),
                   jax.ShapeDtypeStruct((n_seq, 2, N_HEADS, HEAD_DIM, HEAD_DIM), F32)],
        scratch_shapes=[vm((seq_len, HEAD_DIM)), vm((seq_len, HEAD_DIM)), vm((seq_len, HEAD_DIM)),
                        vm((2, seq_len, HEAD_DIM)), vm((2, seq_len, HEAD_DIM)),
                        vm((seq_len, HEAD_DIM)), vm((seq_len, HEAD_DIM))],
        compiler_params=_cparams(("parallel", "parallel")),
        name="deltanet",
    )(proj, proj, proj, proj, proj, conv_w, conv_w, conv_w, par, norm_g, s0)


def _merge_kernel(ret_ref, lru_ref, dn_ref, g0_ref, g1_ref, g2_ref, x_ref, mod_ref, ng_ref, wbr_ref, wout_ref,
                  xo_ref, h_ref):
    merged = jax.nn.sigmoid(g0_ref[...]) * _mm(ret_ref[...], wbr_ref[0])
    merged = merged + jax.nn.sigmoid(g1_ref[...]) * _mm(lru_ref[...], wbr_ref[1])
    merged = merged + jax.nn.sigmoid(g2_ref[...]) * _mm(dn_ref[...], wbr_ref[2])
    x = x_ref[...] + mod_ref[2:3, :] * _mm(merged, wout_ref[...])
    xo_ref[...] = x
    h_ref[...] = (_rms(x) * ng_ref[...] * (1.0 + mod_ref[4:5, :]) + mod_ref[3:4, :]).astype(BF16)


def _merge(o_ret, o_lru, o_dn, proj, x, mod, norm_g, w_br, w_out, mod_row, tm):
    t = x.shape[0]
    row = pl.BlockSpec((tm, D_MODEL), lambda i: (i, 0))
    col0 = COL_BGATE * HEAD_DIM // D_MODEL

    def gate(k):
        return pl.BlockSpec((tm, D_MODEL), lambda i: (i, col0 + k))

    return pl.pallas_call(
        _merge_kernel,
        grid=(t // tm,),
        in_specs=[row, row, row, gate(0), gate(1), gate(2), row,
                  pl.BlockSpec((None, N_MOD, D_MODEL), lambda i: (mod_row(i), 0, 0)),
                  pl.BlockSpec((1, D_MODEL), lambda i: (0, 0)),
                  pl.BlockSpec((3, D_MODEL, D_MODEL), lambda i: (0, 0, 0)),
                  pl.BlockSpec((D_MODEL, D_MODEL), lambda i: (0, 0))],
        out_specs=[row, row],
        out_shape=[jax.ShapeDtypeStruct((t, D_MODEL), F32), jax.ShapeDtypeStruct((t, D_MODEL), BF16)],
        compiler_params=_cparams(("parallel",)),
        name="merge",
    )(o_ret, o_lru, o_dn, proj, proj, proj, x, mod, norm_g, w_br, w_out)


def _router_kernel(h_ref, wq_ref, keys_ref, s1_ref, thr_ref, e0_ref, e1_ref, q_scr, top_scr, cand_scr):
    tb = h_ref.shape[0]
    q_scr[...] = _mm_nt(wq_ref[...], h_ref[...])

    top_scr[...] = jnp.full(top_scr.shape, NEG_BIG, F32)

    def top_values(x, dst):
        for kk in range(TOP_N):
            m = jnp.max(x, axis=0, keepdims=True)
            top_scr[dst, kk:kk + 1, :] = m
            x = jnp.where(x >= m, NEG_BIG, x)

    def body(hd, carry):
        base = pl.multiple_of(hd * 2 * N_KEYS, 2 * N_KEYS)
        s0 = _mm(keys_ref[0], q_scr[pl.ds(base, N_KEYS), :])
        s1 = _mm(keys_ref[1], q_scr[pl.ds(base + N_KEYS, N_KEYS), :])
        top_values(s0, 0)
        top_values(s1, 1)
        a1 = top_scr[1]
        for p in range(TOP_N):
            cand_scr[p * TOP_PAD:(p + 1) * TOP_PAD, :] = top_scr[0, p:p + 1, :] + a1
        x = cand_scr[...]
        m0 = jnp.max(x, axis=0, keepdims=True)
        z = jnp.zeros_like(m0)
        m = m0
        for kk in range(PEER_TOPK):
            if kk > 0:
                x = jnp.where(x >= m, NEG_BIG, x)
                m = jnp.max(x, axis=0, keepdims=True)
            z = z + jnp.exp(m - m0)
        m_next = jnp.max(jnp.where(x >= m, NEG_BIG, x), axis=0, keepdims=True)
        s1_ref[hd] = s1
        thr_ref[hd] = 0.5 * (m + m_next) - s0
        e0_ref[hd] = jnp.exp(s0 - top_scr[0, 0:1, :])
        e1_ref[hd] = jnp.exp(s1 - top_scr[1, 0:1, :]) / z
        return carry

    lax.fori_loop(0, N_HEADS, body, 0)


def _router(h2, wq_t, keys, tb):
    t = h2.shape[0]
    out = pl.BlockSpec((N_HEADS, N_KEYS, tb), lambda i: (0, 0, i))
    shp = jax.ShapeDtypeStruct((N_HEADS, N_KEYS, t), F32)
    return pl.pallas_call(
        _router_kernel,
        grid=(t // tb,),
        in_specs=[pl.BlockSpec((tb, D_MODEL), lambda i: (i, 0)),
                  pl.BlockSpec((2 * N_KEYS * N_HEADS, D_MODEL), lambda i: (0, 0)),
                  pl.BlockSpec((2, N_KEYS, N_KEYS), lambda i: (0, 0, 0))],
        out_specs=[out, out, out, out],
        out_shape=[shp, shp, shp, shp],
        scratch_shapes=[pltpu.VMEM((2 * N_KEYS * N_HEADS, tb), F32),
                        pltpu.VMEM((2, TOP_PAD, tb), F32),
                        pltpu.VMEM((TOP_N * TOP_PAD, tb), F32)],
        compiler_params=_cparams(("parallel",)),
        name="peer_router",
    )(h2, wq_t, keys)


def _expert_kernel(h_ref, u_ref, vt_ref, s1_ref, thr_ref, e0_ref, e1_ref, x_ref, mod_ref, o_ref,
                   acc_scr, act_scr, g_scr, *, tile_e):
    j = pl.program_id(1)
    tb = h_ref.shape[0]
    n_sub = tile_e // N_KEYS

    @pl.when(j == 0)
    def _():
        acc_scr[...] = jnp.zeros_like(acc_scr)

    act_scr[...] = _mm_nt(u_ref[...], h_ref[...])
    i0 = pl.multiple_of(j * n_sub, n_sub)
    for ii in range(n_sub):
        for c in range(tb // HEAD_DIM):
            ls = slice(c * HEAD_DIM, (c + 1) * HEAD_DIM)
            wd = jnp.zeros((N_KEYS, HEAD_DIM), F32)
            for hd in range(N_HEADS):
                thr = thr_ref[hd, pl.ds(i0, n_sub), ls][ii:ii + 1, :]
                e0 = e0_ref[hd, pl.ds(i0, n_sub), ls][ii:ii + 1, :]
                wd = wd + jnp.where(s1_ref[hd, :, ls] >= thr, e1_ref[hd, :, ls] * e0, 0.0)
            rs = slice(ii * N_KEYS, (ii + 1) * N_KEYS)
            g_scr[rs, ls] = (_gelu(act_scr[rs, ls]) * wd).astype(BF16)
    acc_scr[...] += jnp.dot(vt_ref[...], g_scr[...], preferred_element_type=F32)

    @pl.when(j == pl.num_programs(1) - 1)
    def _():
        o_ref[...] = x_ref[...] + mod_ref[5:6, :] * acc_scr[...].T


def _experts(h2, u_tab, vt_tab, routing, x, mod, mod_row, tb, tile_e):
    t = h2.shape[0]
    n_exp = u_tab.shape[0]
    rt = pl.BlockSpec((N_HEADS, N_KEYS, tb), lambda i, j: (0, 0, i))
    return pl.pallas_call(
        functools.partial(_expert_kernel, tile_e=tile_e),
        grid=(t // tb, n_exp // tile_e),
        in_specs=[pl.BlockSpec((tb, D_MODEL), lambda i, j: (i, 0)),
                  pl.BlockSpec((tile_e, D_MODEL), lambda i, j: (j, 0)),
                  pl.BlockSpec((D_MODEL, tile_e), lambda i, j: (0, j)),
                  rt, rt, rt, rt,
                  pl.BlockSpec((tb, D_MODEL), lambda i, j: (i, 0)),
                  pl.BlockSpec((None, N_MOD, D_MODEL), lambda i, j: (mod_row(i), 0, 0))],
        out_specs=pl.BlockSpec((tb, D_MODEL), lambda i, j: (i, 0)),
        out_shape=jax.ShapeDtypeStruct((t, D_MODEL), F32),
        scratch_shapes=[pltpu.VMEM((D_MODEL, tb), F32), pltpu.VMEM((tile_e, tb), F32),
                        pltpu.VMEM((tile_e, tb), BF16)],
        compiler_params=_cparams(("parallel", "arbitrary")),
        name="peer_experts",
    )(h2, u_tab, vt_tab, *routing, x, mod)


def _final_norm_kernel(x_ref, g_ref, o_ref):
    o_ref[...] = _rms(x_ref[...]) * g_ref[...]


def _final_norm(x, g, tm):
    t = x.shape[0]
    row = pl.BlockSpec((tm, D_MODEL), lambda i: (i, 0))
    return pl.pallas_call(
        _final_norm_kernel, grid=(t // tm,),
        in_specs=[row, pl.BlockSpec((1, D_MODEL), lambda i: (0, 0))],
        out_specs=row, out_shape=jax.ShapeDtypeStruct((t, D_MODEL), F32),
        compiler_params=_cparams(("parallel",)), name="final_norm",
    )(x, g)


def _rope_tables(seq_len):
    rows = seq_len // GRID_W
    r = jnp.repeat(jnp.arange(rows, dtype=F32), GRID_W)
    col = jnp.tile(jnp.arange(GRID_W, dtype=F32), rows)
    nf = HEAD_DIM // 4
    inv = ROPE_BASE ** (-jnp.arange(nf, dtype=F32) / nf)
    ang = jnp.concatenate([r[:, None] * inv, col[:, None] * inv], axis=-1)
    cos, sin = jnp.cos(ang), jnp.sin(ang)
    return jnp.concatenate([cos, cos], axis=-1), jnp.concatenate([-sin, sin], axis=-1)


def _lanes(a):
    return jnp.broadcast_to(jnp.moveaxis(a, -1, 0)[..., None], (a.shape[-1],) + a.shape[:-1] + (HEAD_DIM,))


def kernel(x_prompt, x_sample, c, state_ret, state_lru, state_dn, c_ctx, w_mod, b_mod, norm1_g, norm2_g, w_in, ret_gamma_logit, ret_norm_g, lru_conv_w, lru_conv_b, lru_gate_w, lru_gate_b, lru_lambda, dn_conv_w, dn_a_log, dn_dt_bias, dn_norm_g, w_br, w_out, peer_w_q, peer_sub_keys, peer_u, peer_v, final_norm_g):
    n_ctx, l_ctx, _ = x_prompt.shape
    n_lat, l_lat, _ = x_sample.shape
    t_ctx, t_lat = n_ctx * l_ctx, n_lat * l_lat
    assert t_ctx % l_lat == 0
    tb = 512
    assert l_lat % tb == 0 and t_ctx % tb == 0
    ctx_blocks, per_seq = t_ctx // tb, l_lat // tb

    def mod_row(i):
        return jnp.where(i < ctx_blocks, 0, 1 + (i - ctx_blocks) // per_seq)

    tm = 256
    ctx_blocks_m, per_seq_m = t_ctx // tm, l_lat // tm

    def mod_row_m(i):
        return jnp.where(i < ctx_blocks_m, 0, 1 + (i - ctx_blocks_m) // per_seq_m)

    x = jnp.concatenate([x_prompt.reshape(t_ctx, D_MODEL), x_sample.reshape(t_lat, D_MODEL)], axis=0)
    n_cond = 16
    cond = jnp.zeros((n_cond, D_MODEL), F32).at[0].set(c_ctx).at[1:1 + n_lat].set(c)
    mods = _modulation(cond, w_mod, b_mod).reshape(DEPTH, n_cond, N_MOD, D_MODEL)

    w_in_r = jnp.concatenate(
        [w_in[:, :, :N_MAIN], w_in[:, :, N_MAIN + N_SMALL:], w_in[:, :, N_MAIN:N_MAIN + N_SMALL],
         jnp.zeros((DEPTH, D_MODEL, HEAD_DIM - N_SMALL), F32)], axis=-1).astype(BF16)
    w_br_b, w_out_b = w_br.astype(BF16), w_out.astype(BF16)
    wq_t = jnp.swapaxes(peer_w_q, 1, 2).astype(BF16)
    keys_b = peer_sub_keys.astype(BF16)
    u_b = peer_u.astype(BF16)
    vt_b = jnp.swapaxes(peer_v, 1, 2).astype(BF16)
    gam = _lanes(ret_gamma_logit)
    dn_par = _lanes(jnp.concatenate([dn_a_log, dn_dt_bias], axis=1))
    rope_tabs = _rope_tables(l_lat)
    zero_ret = jnp.zeros((n_ctx, 2, N_HEADS, HEAD_DIM, HEAD_DIM), F32)
    zero_lru = jnp.zeros((n_ctx, 2, D_MODEL), F32)

    mat_zero = pl.BlockSpec((None, 2, None, HEAD_DIM, HEAD_DIM), lambda b, h: (b, 0, h, 0, 0))
    vec_zero = pl.BlockSpec((None, 2, HEAD_DIM), lambda b, n: (b, 0, n))
    row_off_lat = t_ctx // l_lat

    ret_states, lru_states, dn_states = [], [], []
    for l in range(DEPTH):
        mat_lat = pl.BlockSpec((None, None, 2, None, HEAD_DIM, HEAD_DIM), lambda b, h, l=l: (b, l, 0, h, 0, 0))
        vec_lat = pl.BlockSpec((None, None, 2, HEAD_DIM), lambda b, n, l=l: (b, l, 0, n))
        proj = _in_proj(x, mods[l], norm1_g[l][None], w_in_r[l], mod_row, tb)

        ng_ret = ret_norm_g[l][:, None, :]
        o_ret_c, s_ret = _retention(proj, gam[:, l], ng_ret, zero_ret, mat_zero, None, n_ctx, l_ctx, 0)
        o_ret_l, _ = _retention(proj, gam[:, l], ng_ret, state_ret, mat_lat, rope_tabs, n_lat, l_lat, row_off_lat)

        lru_args = (lru_conv_w[l], lru_conv_b[l][None], lru_gate_w[l], lru_gate_b[l], lru_lambda[l])
        o_lru_c, s_lru = _rglru(proj, *lru_args, zero_lru, vec_zero, n_ctx, l_ctx, 0)
        o_lru_l, _ = _rglru(proj, *lru_args, state_lru, vec_lat, n_lat, l_lat, row_off_lat)

        ng_dn = dn_norm_g[l][:, None, :]
        o_dn_c, s_dn = _deltanet(proj, dn_conv_w[l], dn_par[:, l], ng_dn, zero_ret, mat_zero, n_ctx, l_ctx, 0)
        o_dn_l, _ = _deltanet(proj, dn_conv_w[l], dn_par[:, l], ng_dn, state_dn, mat_lat, n_lat, l_lat, row_off_lat)

        o_ret = jnp.concatenate([o_ret_c, o_ret_l], axis=0)
        o_lru = jnp.concatenate([o_lru_c, o_lru_l], axis=0)
        o_dn = jnp.concatenate([o_dn_c, o_dn_l], axis=0)

        x, h2 = _merge(o_ret, o_lru, o_dn, proj, x, mods[l], norm2_g[l][None], w_br_b[l], w_out_b[l],
                       mod_row_m, tm)
        routing = _router(h2, wq_t[l], keys_b[l], tb)
        x = _experts(h2, u_b[l], vt_b[l], routing, x, mods[l], mod_row, tb, 1024)

        ret_states.append(s_ret)
        lru_states.append(s_lru)
        dn_states.append(s_dn)

    y = _final_norm(x, final_norm_g[None], tb)
    y_prompt = y[:t_ctx].reshape(n_ctx, l_ctx, D_MODEL)
    y_sample = y[t_ctx:].reshape(n_lat, l_lat, D_MODEL)
    return (y_prompt, y_sample, jnp.stack(ret_states, axis=1), jnp.stack(lru_states, axis=1),
            jnp.stack(dn_states, axis=1))
```

```python
import functools
import math

import jax
import jax.numpy as jnp
from jax import lax
from jax.experimental import pallas as pl
from jax.experimental.pallas import tpu as pltpu

F32 = jnp.float32
BF16 = jnp.bfloat16

D_MODEL = 1024
DEPTH = 4
N_MOD = 6
EPS = 1e-6
GRID_W = 64
ROPE_BASE = 10000.0
N_HEADS = 8
HEAD_DIM = 128
RET_CHUNK = 128
DN_CHUNK = 64
DN_SUPER = 256
LRU_C = 8.0
N_KEYS = 128
PEER_TOPK = 16
N_EXPERTS = N_KEYS * N_KEYS
SUBLANES = 8
TOP_N = PEER_TOPK + 1
TOP_PAD = -(-TOP_N // SUBLANES) * SUBLANES
EXPERT_PART = 256
NEG_BIG = -3.0e38

COL_RQ, COL_RK, COL_RV, COL_RG = 0, 8, 16, 24
COL_LX, COL_LG = 32, 40
COL_DQ, COL_DK, COL_DV, COL_DZ = 48, 56, 64, 72
N_MAIN = 80 * 128
N_SMALL = 4 * N_HEADS
COL_BGATE = 80
COL_SMALL = 104
N_PROJ = 105 * 128

VMEM_LIMIT = 48 * 1024 * 1024


def _cparams(sem):
    return pltpu.CompilerParams(dimension_semantics=sem, vmem_limit_bytes=VMEM_LIMIT)


def _mm(a, b):
    return jnp.dot(a.astype(BF16), b.astype(BF16), preferred_element_type=F32)


def _mm_nt(a, b):
    return lax.dot_general(a.astype(BF16), b.astype(BF16), (((1,), (1,)), ((), ())),
                           preferred_element_type=F32)


def _mm_tn(a, b):
    return lax.dot_general(a.astype(BF16), b.astype(BF16), (((0,), (0,)), ((), ())),
                           preferred_element_type=F32)


def _softplus(x):
    return jnp.maximum(x, 0.0) + jnp.log1p(jnp.exp(-jnp.abs(x)))


def _silu(x):
    return x * jax.nn.sigmoid(x)


def _gelu(x):
    return 0.5 * x * (1.0 + lax.erf(x * math.sqrt(0.5)))


def _rms(x):
    return x * lax.rsqrt(jnp.mean(x * x, axis=-1, keepdims=True) + EPS)


def _shift_rows(x, s, row):
    n = x.shape[0]
    if s == 0:
        return x
    y = pltpu.roll(x, (-s) % n, 0)
    ok = (row + s >= 0) & (row + s < n)
    return jnp.where(ok, y, 0.0)


def _dw_conv(x, w, row):
    y = _shift_rows(x, -2, row) * w[0:1, :]
    y = y + _shift_rows(x, -1, row) * w[1:2, :]
    y = y + x * w[2:3, :]
    y = y + _shift_rows(x, 1, row) * w[3:4, :]
    return y


def _mod_kernel(c_ref, w_ref, b_ref, o_ref):
    c = c_ref[...]
    o_ref[...] = jnp.dot(_silu(c), w_ref[...], precision=lax.Precision.HIGHEST,
                         preferred_element_type=F32) + b_ref[...]


def _modulation(cond, w_mod, b_mod):
    n_rows = cond.shape[0]
    tn = 1536
    n_out = N_MOD * D_MODEL
    return pl.pallas_call(
        _mod_kernel,
        grid=(DEPTH, n_out // tn),
        in_specs=[pl.BlockSpec((n_rows, D_MODEL), lambda l, j: (0, 0)),
                  pl.BlockSpec((None, D_MODEL, tn), lambda l, j: (l, 0, j)),
                  pl.BlockSpec((None, 1, tn), lambda l, j: (l, 0, j))],
        out_specs=pl.BlockSpec((None, n_rows, tn), lambda l, j: (l, 0, j)),
        out_shape=jax.ShapeDtypeStruct((DEPTH, n_rows, n_out), F32),
        compiler_params=_cparams(("parallel", "parallel")),
        name="modulation",
    )(cond, w_mod, b_mod.reshape(DEPTH, 1, n_out))


def _in_proj_kernel(x_ref, mod_ref, g_ref, w_ref, o_ref, h_scr):
    @pl.when(pl.program_id(1) == 0)
    def _():
        y = _rms(x_ref[...]) * g_ref[...]
        h_scr[...] = (y * (1.0 + mod_ref[1:2, :]) + mod_ref[0:1, :]).astype(BF16)

    o_ref[...] = jnp.dot(h_scr[...], w_ref[...], preferred_element_type=F32)


def _in_proj(x, mod, norm_g, w, mod_row, tm):
    t = x.shape[0]
    tn = 1920
    return pl.pallas_call(
        _in_proj_kernel,
        grid=(t // tm, N_PROJ // tn),
        in_specs=[pl.BlockSpec((tm, D_MODEL), lambda i, j: (i, 0)),
                  pl.BlockSpec((None, N_MOD, D_MODEL), lambda i, j: (mod_row(i), 0, 0)),
                  pl.BlockSpec((1, D_MODEL), lambda i, j: (0, 0)),
                  pl.BlockSpec((D_MODEL, tn), lambda i, j: (0, j))],
        out_specs=pl.BlockSpec((tm, tn), lambda i, j: (i, j)),
        out_shape=jax.ShapeDtypeStruct((t, N_PROJ), F32),
        scratch_shapes=[pltpu.VMEM((tm, D_MODEL), BF16)],
        compiler_params=_cparams(("parallel", "arbitrary")),
        name="in_proj",
    )(x, mod, norm_g, w)


def _ret_kernel(*refs, seq_len, rope):
    if rope:
        (q_ref, k_ref, v_ref, g_ref, gam_ref, ng_ref, s0_ref, cs_ref, sn_ref,
         o_ref, so_ref, q_scr, k_scr) = refs
    else:
        (q_ref, k_ref, v_ref, g_ref, gam_ref, ng_ref, s0_ref,
         o_ref, so_ref, q_scr, k_scr) = refs
    c = RET_CHUNK
    n_chunks = seq_len // c
    gam = gam_ref[...]
    lg = -_softplus(-gam)
    lgf, lgb = lg[0:1, :], lg[1:2, :]
    r = lax.broadcasted_iota(jnp.int32, (c, HEAD_DIM), 0).astype(F32)
    ci = lax.broadcasted_iota(jnp.int32, (c, c), 0)
    si = lax.broadcasted_iota(jnp.int32, (c, c), 1)
    dmat = (ci - si).astype(F32)
    dec_f = jnp.where(dmat >= 0, jnp.exp(lgf * jnp.maximum(dmat, 0.0)), 0.0)
    dec_b = jnp.where(dmat <= 0, jnp.exp(lgb * jnp.maximum(-dmat, 0.0)), 0.0)
    qsc_f, ksc_f, gch_f = jnp.exp(lgf * (r + 1.0)), jnp.exp(lgf * (c - 1.0 - r)), jnp.exp(lgf * c)
    qsc_b, ksc_b, gch_b = jnp.exp(lgb * (c - r)), jnp.exp(lgb * r), jnp.exp(lgb * c)
    scale = HEAD_DIM ** -0.5

    s = s0_ref[0]
    for n in range(n_chunks):
        sl = pl.ds(n * c, c)
        q = q_ref[sl, :]
        k = k_ref[sl, :] * scale
        v = v_ref[sl, :]
        if rope:
            cs, sn = cs_ref[sl, :], sn_ref[sl, :]
            q = q * cs + pltpu.roll(q, HEAD_DIM // 2, 1) * sn
            k = k * cs + pltpu.roll(k, HEAD_DIM // 2, 1) * sn
        q_scr[sl, :] = q
        k_scr[sl, :] = k
        o = _mm(_mm_nt(q, k) * dec_f, v) + _mm(q * qsc_f, s)
        s = s * gch_f + _mm_tn(k * ksc_f, v)
        o_ref[sl, :] = o
    so_ref[0] = s

    s = s0_ref[1]
    ng = ng_ref[...]
    for n in reversed(range(n_chunks)):
        sl = pl.ds(n * c, c)
        q, k, v = q_scr[sl, :], k_scr[sl, :], v_ref[sl, :]
        o = _mm(_mm_nt(q, k) * dec_b, v) + _mm(q * qsc_b, s)
        s = s * gch_b + _mm_tn(k * ksc_b, v)
        tot = o_ref[sl, :] + o
        o_ref[sl, :] = _rms(tot) * ng * _silu(g_ref[sl, :])
    so_ref[1] = s


def _retention(proj, gam, norm_g, s0, s0_map, rope_tabs, n_seq, seq_len, row_off):
    rope = rope_tabs is not None

    def col(c0):
        return pl.BlockSpec((seq_len, HEAD_DIM), lambda b, h: (row_off + b, c0 + h))

    in_specs = [col(COL_RQ), col(COL_RK), col(COL_RV), col(COL_RG),
                pl.BlockSpec((None, 2, HEAD_DIM), lambda b, h: (h, 0, 0)),
                pl.BlockSpec((None, 1, HEAD_DIM), lambda b, h: (h, 0, 0)),
                s0_map]
    args = [proj, proj, proj, proj, gam, norm_g, s0]
    if rope:
        tab = pl.BlockSpec((seq_len, HEAD_DIM), lambda b, h: (0, 0))
        in_specs += [tab, tab]
        args += list(rope_tabs)
    return pl.pallas_call(
        functools.partial(_ret_kernel, seq_len=seq_len, rope=rope),
        grid=(n_seq, N_HEADS),
        in_specs=in_specs,
        out_specs=[pl.BlockSpec((seq_len, HEAD_DIM), lambda b, h: (b, h)),
                   pl.BlockSpec((None, 2, None, HEAD_DIM, HEAD_DIM), lambda b, h: (b, 0, h, 0, 0))],
        out_shape=[jax.ShapeDtypeStruct((n_seq * seq_len, D_MODEL), F32),
                   jax.ShapeDtypeStruct((n_seq, 2, N_HEADS, HEAD_DIM, HEAD_DIM), F32)],
        scratch_shapes=[pltpu.VMEM((seq_len, HEAD_DIM), F32), pltpu.VMEM((seq_len, HEAD_DIM), F32)],
        compiler_params=_cparams(("parallel", "parallel")),
        name="retention_rope" if rope else "retention",
    )(*args)


def _lru_kernel(x_ref, gate_ref, cw_ref, cb_ref, gw_ref, gb_ref, lam_ref, s0_ref, o_ref, so_ref, *, seq_len):
    n = seq_len
    row = lax.broadcasted_iota(jnp.int32, (n, HEAD_DIM), 0)
    xc = _dw_conv(x_ref[...], cw_ref[...], row) + cb_ref[...]
    lam = lam_ref[...]
    hs = []
    for d in range(2):
        r_gate = jax.nn.sigmoid(_mm(xc, gw_ref[d, 0]) + gb_ref[d, 0:1, :])
        i_gate = jax.nn.sigmoid(_mm(xc, gw_ref[d, 1]) + gb_ref[d, 1:2, :])
        log_a = -LRU_C * r_gate * _softplus(-lam[d:d + 1, :])
        a = jnp.exp(log_a)
        u = jnp.sqrt(-jnp.tanh(log_a) * (1.0 + a * a)) * i_gate * xc
        step = 1
        while step < n:
            if d == 0:
                ok = row >= step
                sh = step
            else:
                ok = row < n - step
                sh = n - step
            a_sh = jnp.where(ok, pltpu.roll(a, sh, 0), 1.0)
            u_sh = jnp.where(ok, pltpu.roll(u, sh, 0), 0.0)
            u = a * u_sh + u
            a = a * a_sh
            step *= 2
        h = u + a * s0_ref[d:d + 1, :]
        hs.append(h)
        so_ref[d:d + 1, :] = h[n - 1:n, :] if d == 0 else h[0:1, :]
    o_ref[...] = (hs[0] + hs[1]) * _gelu(gate_ref[...])


def _rglru(proj, conv_w, conv_b, gate_w, gate_b, lam, s0, s0_map, n_seq, seq_len, row_off):
    return pl.pallas_call(
        functools.partial(_lru_kernel, seq_len=seq_len),
        grid=(n_seq, N_HEADS),
        in_specs=[pl.BlockSpec((seq_len, HEAD_DIM), lambda b, n: (row_off + b, COL_LX + n)),
                  pl.BlockSpec((seq_len, HEAD_DIM), lambda b, n: (row_off + b, COL_LG + n)),
                  pl.BlockSpec((4, HEAD_DIM), lambda b, n: (0, n)),
                  pl.BlockSpec((1, HEAD_DIM), lambda b, n: (0, n)),
                  pl.BlockSpec((2, 2, None, HEAD_DIM, HEAD_DIM), lambda b, n: (0, 0, n, 0, 0)),
                  pl.BlockSpec((2, 2, HEAD_DIM), lambda b, n: (0, 0, n)),
                  pl.BlockSpec((2, HEAD_DIM), lambda b, n: (0, n)),
                  s0_map],
        out_specs=[pl.BlockSpec((seq_len, HEAD_DIM), lambda b, n: (b, n)),
                   pl.BlockSpec((None, 2, HEAD_DIM), lambda b, n: (b, 0, n))],
        out_shape=[jax.ShapeDtypeStruct((n_seq * seq_len, D_MODEL), F32),
                   jax.ShapeDtypeStruct((n_seq, 2, D_MODEL), F32)],
        compiler_params=_cparams(("parallel", "parallel")),
        name="rglru",
    )(proj, proj, conv_w, conv_b, gate_w, gate_b, lam, s0)


def _dn_kernel(q_ref, k_ref, v_ref, z_ref, sm_ref, cwq_ref, cwk_ref, cwv_ref, par_ref, ng_ref, s0_ref,
               o_ref, so_ref, q_scr, k_scr, v_scr, c_scr, b_scr, of_scr, ob_scr, *, seq_len, hg):
    n = seq_len
    cc = DN_CHUNK
    sc = DN_SUPER
    n_super = n // sc
    head0 = pl.program_id(1) * hg
    roww = lax.broadcasted_iota(jnp.int32, (n, hg * HEAD_DIM), 0)
    row = lax.broadcasted_iota(jnp.int32, (n, HEAD_DIM), 0)
    lane = lax.broadcasted_iota(jnp.int32, (n, HEAD_DIM), 1)
    pos = row & (cc - 1)

    xq = _silu(_dw_conv(q_ref[...], cwq_ref[...], roww))
    xk = _silu(_dw_conv(k_ref[...], cwk_ref[...], roww))
    v_scr[...] = _silu(_dw_conv(v_ref[...], cwv_ref[...], roww))
    small = sm_ref[...]
    par = par_ref[...]
    g_all = -jnp.exp(par[0:1, :]) * _softplus(small + par[1:2, :])
    beta_all = jax.nn.sigmoid(small)
    cum_all = [g_all, g_all]
    step = 1
    while step < cc:
        cum_all[0] = cum_all[0] + jnp.where(pos >= step, pltpu.roll(cum_all[0], step, 0), 0.0)
        cum_all[1] = cum_all[1] + jnp.where(pos < cc - step, pltpu.roll(cum_all[1], n - step, 0), 0.0)
        step *= 2
    for hh in range(hg):
        cs = slice(hh * HEAD_DIM, (hh + 1) * HEAD_DIM)
        xqh, xkh = xq[:, cs], xk[:, cs]
        q_scr[:, cs] = xqh * lax.rsqrt(jnp.sum(xqh * xqh, axis=-1, keepdims=True) + EPS) * (HEAD_DIM ** -0.5)
        k_scr[:, cs] = xkh * lax.rsqrt(jnp.sum(xkh * xkh, axis=-1, keepdims=True) + EPS)
        head = head0 + hh
        for d in range(2):
            a_lane = lane == 2 * N_HEADS * d + head
            b_lane = lane == 2 * N_HEADS * d + N_HEADS + head
            cum = jnp.sum(jnp.where(a_lane, cum_all[d], 0.0), axis=-1, keepdims=True)
            beta = jnp.sum(jnp.where(b_lane, beta_all, 0.0), axis=-1, keepdims=True)
            c_scr[d, hh] = jnp.broadcast_to(cum, (n, HEAD_DIM))
            b_scr[d, hh] = jnp.broadcast_to(beta, (n, HEAD_DIM))

    ri = lax.broadcasted_iota(jnp.int32, (sc, sc), 0)
    cj = lax.broadcasted_iota(jnp.int32, (sc, sc), 1)
    sh = cc.bit_length() - 1
    same = (ri >> sh) == (cj >> sh)
    incl = (same & (ri >= cj), same & (ri <= cj))
    strict = (same & (ri > cj), same & (ri < cj))
    eye = jnp.where(ri == cj, 1.0, 0.0)
    level = [(ri >> 3) == (cj >> 3)]
    for b in range(4, sh + 1):
        level.append(((ri >> b) == (cj >> b)) & ((ri >> (b - 1)) != (cj >> (b - 1))))

    n_ch = sc // cc

    def super_chunks(chains):
        idx = range(len(chains))
        dd = [c[0] for c in chains]
        sl = [pl.ds(c[2], sc) for c in chains]
        cs = [slice(c[1] * HEAD_DIM, (c[1] + 1) * HEAD_DIM) for c in chains]
        q = [q_scr[sl[c], cs[c]] for c in idx]
        k = [k_scr[sl[c], cs[c]] for c in idx]
        v = [v_scr[sl[c], cs[c]] for c in idx]
        cum = [c_scr[dd[c], chains[c][1], sl[c], :] for c in idx]
        beta = [b_scr[dd[c], chains[c][1], sl[c], :] for c in idx]
        kk = [_mm_nt(k[c], k[c]) for c in idx]
        qk = [_mm_nt(q[c], k[c]) for c in idx]
        decay, x, attn = [], [], []
        for c in idx:
            cb = jnp.concatenate([cum[c], cum[c]], axis=1)
            diff = cb - cb.T
            dec = jnp.where(incl[dd[c]], jnp.exp(jnp.where(incl[dd[c]], diff, 0.0)), 0.0)
            bb = jnp.concatenate([beta[c], beta[c]], axis=1)
            x.append(-jnp.where(strict[dd[c]], kk[c] * bb * dec, 0.0))
            attn.append(qk[c] * dec)
        xp = [jnp.where(level[0], x[c], 0.0) for c in idx]
        p = [eye + xp[c] for c in idx]
        for _ in range(2):
            xp = [_mm(xp[c], xp[c]) for c in idx]
            pm = [_mm(p[c], xp[c]) for c in idx]
            p = [p[c] + pm[c] for c in idx]
        for lv in range(1, len(level)):
            t1 = [_mm(jnp.where(level[lv], x[c], 0.0), p[c]) for c in idx]
            t2 = [_mm(p[c], t1[c]) for c in idx]
            p = [p[c] + t2[c] for c in idx]
        rhs = [jnp.concatenate([v[c] * beta[c], k[c] * beta[c] * jnp.exp(cum[c])], axis=1) for c in idx]
        sol = [_mm(p[c], rhs[c]) for c in idx]
        qd = [q[c] * jnp.exp(cum[c]) for c in idx]
        s = [so_ref[dd[c], chains[c][1]] for c in idx]
        v_new = [[None] * n_ch for _ in idx]
        o_inter = [[None] * n_ch for _ in idx]
        for stp in range(n_ch):
            ch = [stp if dd[c] == 0 else n_ch - 1 - stp for c in idx]
            rs = [slice(ch[c] * cc, (ch[c] + 1) * cc) for c in idx]
            ws = [_mm(sol[c][rs[c], HEAD_DIM:], s[c]) for c in idx]
            for c in idx:
                o_inter[c][ch[c]] = _mm(qd[c][rs[c], :], s[c])
            kt = []
            for c in idx:
                cum_c = cum[c][rs[c], :]
                tot = cum_c[cc - 1:cc, :] if dd[c] == 0 else cum_c[0:1, :]
                v_new[c][ch[c]] = sol[c][rs[c], :HEAD_DIM] - ws[c]
                kt.append((k[c][rs[c], :] * jnp.exp(tot - cum_c), jnp.exp(tot)))
            upd = [_mm_tn(kt[c][0], v_new[c][ch[c]]) for c in idx]
            s = [s[c] * kt[c][1] + upd[c] for c in idx]
        oi = [_mm(attn[c], jnp.concatenate(v_new[c], axis=0)) for c in idx]
        for c in idx:
            o = jnp.concatenate(o_inter[c], axis=0) + oi[c]
            if dd[c] == 0:
                of_scr[sl[c], cs[c]] = o
            else:
                ob_scr[sl[c], cs[c]] = o
            so_ref[dd[c], chains[c][1]] = s[c]

    so_ref[...] = s0_ref[...]

    def body(i, carry):
        fwd = pl.multiple_of(i * sc, sc)
        bwd = pl.multiple_of((n_super - 1 - i) * sc, sc)
        super_chunks([(d, hh, fwd if d == 0 else bwd) for hh in range(hg) for d in range(2)])
        return carry

    lax.fori_loop(0, n_super, body, 0)
    tot = of_scr[...] + ob_scr[...]
    z = z_ref[...]
    for hh in range(hg):
        cs = slice(hh * HEAD_DIM, (hh + 1) * HEAD_DIM)
        o_ref[:, cs] = _rms(tot[:, cs]) * ng_ref[hh] * _silu(z[:, cs])


def _state_spec(s0, layer, hg):
    if layer is None:
        return pl.BlockSpec((None, 2, hg, HEAD_DIM, HEAD_DIM), lambda b, g: (b, 0, g, 0, 0))
    return pl.BlockSpec((None, None, 2, hg, HEAD_DIM, HEAD_DIM), lambda b, g: (b, layer, 0, g, 0, 0))


def _deltanet(proj, conv_w, par, norm_g, s0, layer, n_seq, seq_len, row_off, hg):
    wide = hg * HEAD_DIM

    def col(c0):
        return pl.BlockSpec((seq_len, wide), lambda b, g: (row_off + b, c0 // hg + g))

    def cw(c0):
        return pl.BlockSpec((4, wide), lambda b, g: (0, c0 // hg + g))

    vm = functools.partial(pltpu.VMEM, dtype=F32)
    return pl.pallas_call(
        functools.partial(_dn_kernel, seq_len=seq_len, hg=hg),
        grid=(n_seq, N_HEADS // hg),
        in_specs=[col(COL_DQ), col(COL_DK), col(COL_DV), col(COL_DZ),
                  pl.BlockSpec((seq_len, HEAD_DIM), lambda b, g: (row_off + b, COL_SMALL)),
                  cw(0), cw(N_HEADS), cw(2 * N_HEADS),
                  pl.BlockSpec((2, HEAD_DIM), lambda b, g: (0, 0)),
                  pl.BlockSpec((hg, 1, HEAD_DIM), lambda b, g: (g, 0, 0)),
                  _state_spec(s0, layer, hg)],
        out_specs=[pl.BlockSpec((seq_len, wide), lambda b, g: (b, g)),
                   pl.BlockSpec((None, 2, hg, HEAD_DIM, HEAD_DIM), lambda b, g: (b, 0, g, 0, 0))],
        out_shape=[jax.ShapeDtypeStruct((n_seq * seq_len, D_MODEL), F32),
                   jax.ShapeDtypeStruct((n_seq, 2, N_HEADS, HEAD_DIM, HEAD_DIM), F32)],
        scratch_shapes=[vm((seq_len, wide)), vm((seq_len, wide)), vm((seq_len, wide)),
                        vm((2, hg, seq_len, HEAD_DIM)), vm((2, hg, seq_len, HEAD_DIM)),
                        vm((seq_len, wide)), vm((seq_len, wide))],
        compiler_params=_cparams(("parallel", "parallel")),
        name="deltanet",
    )(proj, proj, proj, proj, proj, conv_w, conv_w, conv_w, par, norm_g, s0)


def _merge_kernel(ret_ref, lru_ref, dn_ref, g0_ref, g1_ref, g2_ref, x_ref, mod_ref, ng_ref, wbr_ref, wout_ref,
                  xo_ref, h_ref):
    merged = jax.nn.sigmoid(g0_ref[...]) * _mm(ret_ref[...], wbr_ref[0])
    merged = merged + jax.nn.sigmoid(g1_ref[...]) * _mm(lru_ref[...], wbr_ref[1])
    merged = merged + jax.nn.sigmoid(g2_ref[...]) * _mm(dn_ref[...], wbr_ref[2])
    x = x_ref[...] + mod_ref[2:3, :] * _mm(merged, wout_ref[...])
    xo_ref[...] = x
    h_ref[...] = (_rms(x) * ng_ref[...] * (1.0 + mod_ref[4:5, :]) + mod_ref[3:4, :]).astype(BF16)


def _merge(o_ret, o_lru, o_dn, proj, x, mod, norm_g, w_br, w_out, mod_row, tm):
    t = x.shape[0]
    row = pl.BlockSpec((tm, D_MODEL), lambda i: (i, 0))
    col0 = COL_BGATE * HEAD_DIM // D_MODEL

    def gate(k):
        return pl.BlockSpec((tm, D_MODEL), lambda i: (i, col0 + k))

    return pl.pallas_call(
        _merge_kernel,
        grid=(t // tm,),
        in_specs=[row, row, row, gate(0), gate(1), gate(2), row,
                  pl.BlockSpec((None, N_MOD, D_MODEL), lambda i: (mod_row(i), 0, 0)),
                  pl.BlockSpec((1, D_MODEL), lambda i: (0, 0)),
                  pl.BlockSpec((3, D_MODEL, D_MODEL), lambda i: (0, 0, 0)),
                  pl.BlockSpec((D_MODEL, D_MODEL), lambda i: (0, 0))],
        out_specs=[row, row],
        out_shape=[jax.ShapeDtypeStruct((t, D_MODEL), F32), jax.ShapeDtypeStruct((t, D_MODEL), BF16)],
        compiler_params=_cparams(("parallel",)),
        name="merge",
    )(o_ret, o_lru, o_dn, proj, proj, proj, x, mod, norm_g, w_br, w_out)


def _router_kernel(h_ref, wq_ref, keys_ref, thr_ref, e0_ref, e1_ref, q_scr, top_scr, cand_scr):
    tb = h_ref.shape[0]
    q_scr[...] = _mm_nt(wq_ref[...], h_ref[...])

    top_scr[...] = jnp.full(top_scr.shape, NEG_BIG, F32)

    def top_values(x, dst):
        for kk in range(TOP_N):
            m = jnp.max(x, axis=0, keepdims=True)
            top_scr[dst, kk:kk + 1, :] = m
            x = jnp.where(x >= m, NEG_BIG, x)

    def body(hd, carry):
        base = pl.multiple_of(hd * 2 * N_KEYS, 2 * N_KEYS)
        s0 = _mm(keys_ref[0], q_scr[pl.ds(base, N_KEYS), :])
        s1 = _mm(keys_ref[1], q_scr[pl.ds(base + N_KEYS, N_KEYS), :])
        top_values(s0, 0)
        top_values(s1, 1)
        a1 = top_scr[1]
        for p in range(TOP_N):
            cand_scr[p * TOP_PAD:(p + 1) * TOP_PAD, :] = top_scr[0, p:p + 1, :] + a1
        x = cand_scr[...]
        m0 = jnp.max(x, axis=0, keepdims=True)
        z = jnp.zeros_like(m0)
        m = m0
        for kk in range(PEER_TOPK):
            if kk > 0:
                x = jnp.where(x >= m, NEG_BIG, x)
                m = jnp.max(x, axis=0, keepdims=True)
            z = z + jnp.exp(m - m0)
        m_next = jnp.max(jnp.where(x >= m, NEG_BIG, x), axis=0, keepdims=True)
        tau = 0.5 * (m + m_next)
        max1 = top_scr[1, 0:1, :]
        thr_ref[hd] = jnp.exp(tau - s0 - max1) / z
        e0_ref[hd] = jnp.exp(s0 - top_scr[0, 0:1, :])
        e1_ref[hd] = jnp.exp(s1 - max1) / z
        return carry

    lax.fori_loop(0, N_HEADS, body, 0)


def _router(h2, wq_t, keys, tb):
    t = h2.shape[0]
    out = pl.BlockSpec((N_HEADS, N_KEYS, tb), lambda i: (0, 0, i))
    shp = jax.ShapeDtypeStruct((N_HEADS, N_KEYS, t), F32)
    return pl.pallas_call(
        _router_kernel,
        grid=(t // tb,),
        in_specs=[pl.BlockSpec((tb, D_MODEL), lambda i: (i, 0)),
                  pl.BlockSpec((2 * N_KEYS * N_HEADS, D_MODEL), lambda i: (0, 0)),
                  pl.BlockSpec((2, N_KEYS, N_KEYS), lambda i: (0, 0, 0))],
        out_specs=[out, out, out],
        out_shape=[shp, shp, shp],
        scratch_shapes=[pltpu.VMEM((2 * N_KEYS * N_HEADS, tb), F32),
                        pltpu.VMEM((2, TOP_PAD, tb), F32),
                        pltpu.VMEM((TOP_N * TOP_PAD, tb), F32)],
        compiler_params=_cparams(("parallel",)),
        name="peer_router",
    )(h2, wq_t, keys)


def _expert_kernel(h_ref, u_ref, vt_ref, thr_ref, e0_ref, e1_ref, x_ref, mod_ref, o_ref,
                   acc_scr, g_scr, *, tile_e):
    j = pl.program_id(1)
    tb = h_ref.shape[0]
    n_sub = tile_e // N_KEYS

    @pl.when(j == 0)
    def _():
        acc_scr[...] = jnp.zeros_like(acc_scr)

    i0 = pl.multiple_of(j * n_sub, n_sub)
    n_part = tile_e // EXPERT_PART
    keys_per_part = EXPERT_PART // N_KEYS

    def activations(part):
        return _mm_nt(u_ref[part * EXPERT_PART:(part + 1) * EXPERT_PART, :], h_ref[...])

    act = activations(0)
    acc = acc_scr[...]
    for part in range(n_part):
        act_next = activations(part + 1) if part + 1 < n_part else None
        for c in range(tb // HEAD_DIM):
            ls = slice(c * HEAD_DIM, (c + 1) * HEAD_DIM)
            thr = [thr_ref[hd, pl.ds(i0, n_sub), ls] for hd in range(N_HEADS)]
            e0 = [e0_ref[hd, pl.ds(i0, n_sub), ls] for hd in range(N_HEADS)]
            for r in range(keys_per_part):
                ii = part * keys_per_part + r
                wd = jnp.zeros((N_KEYS, HEAD_DIM), F32)
                for hd in range(N_HEADS):
                    e1 = e1_ref[hd, :, ls]
                    wd = wd + jnp.where(e1 >= thr[hd][ii:ii + 1, :], e1 * e0[hd][ii:ii + 1, :], 0.0)
                g_scr[ii * N_KEYS:(ii + 1) * N_KEYS, ls] = (
                    _gelu(act[r * N_KEYS:(r + 1) * N_KEYS, ls]) * wd).astype(BF16)
        rows = slice(part * EXPERT_PART, (part + 1) * EXPERT_PART)
        acc = acc + jnp.dot(vt_ref[:, rows], g_scr[rows, :], preferred_element_type=F32)
        act = act_next
    acc_scr[...] = acc

    @pl.when(j == pl.num_programs(1) - 1)
    def _():
        o_ref[...] = x_ref[...] + mod_ref[5:6, :] * acc_scr[...].T


def _experts(h2, u_tab, vt_tab, routing, x, mod, mod_row, tb, tile_e):
    t = h2.shape[0]
    n_exp = u_tab.shape[0]
    rt = pl.BlockSpec((N_HEADS, N_KEYS, tb), lambda i, j: (0, 0, i))
    return pl.pallas_call(
        functools.partial(_expert_kernel, tile_e=tile_e),
        grid=(t // tb, n_exp // tile_e),
        in_specs=[pl.BlockSpec((tb, D_MODEL), lambda i, j: (i, 0)),
                  pl.BlockSpec((tile_e, D_MODEL), lambda i, j: (j, 0)),
                  pl.BlockSpec((D_MODEL, tile_e), lambda i, j: (0, j)),
                  rt, rt, rt,
                  pl.BlockSpec((tb, D_MODEL), lambda i, j: (i, 0)),
                  pl.BlockSpec((None, N_MOD, D_MODEL), lambda i, j: (mod_row(i), 0, 0))],
        out_specs=pl.BlockSpec((tb, D_MODEL), lambda i, j: (i, 0)),
        out_shape=jax.ShapeDtypeStruct((t, D_MODEL), F32),
        scratch_shapes=[pltpu.VMEM((D_MODEL, tb), F32), pltpu.VMEM((tile_e, tb), BF16)],
        compiler_params=_cparams(("parallel", "arbitrary")),
        name="peer_experts",
    )(h2, u_tab, vt_tab, *routing, x, mod)


def _final_norm_kernel(x_ref, g_ref, o_ref):
    o_ref[...] = _rms(x_ref[...]) * g_ref[...]


def _final_norm(x, g, tm):
    t = x.shape[0]
    row = pl.BlockSpec((tm, D_MODEL), lambda i: (i, 0))
    return pl.pallas_call(
        _final_norm_kernel, grid=(t // tm,),
        in_specs=[row, pl.BlockSpec((1, D_MODEL), lambda i: (0, 0))],
        out_specs=row, out_shape=jax.ShapeDtypeStruct((t, D_MODEL), F32),
        compiler_params=_cparams(("parallel",)), name="final_norm",
    )(x, g)


def _rope_tables(seq_len):
    rows = seq_len // GRID_W
    r = jnp.repeat(jnp.arange(rows, dtype=F32), GRID_W)
    col = jnp.tile(jnp.arange(GRID_W, dtype=F32), rows)
    nf = HEAD_DIM // 4
    inv = ROPE_BASE ** (-jnp.arange(nf, dtype=F32) / nf)
    ang = jnp.concatenate([r[:, None] * inv, col[:, None] * inv], axis=-1)
    cos, sin = jnp.cos(ang), jnp.sin(ang)
    return jnp.concatenate([cos, cos], axis=-1), jnp.concatenate([-sin, sin], axis=-1)


def _lanes(a):
    return jnp.broadcast_to(jnp.moveaxis(a, -1, 0)[..., None], (a.shape[-1],) + a.shape[:-1] + (HEAD_DIM,))


def kernel(x_prompt, x_sample, c, state_ret, state_lru, state_dn, c_ctx, w_mod, b_mod, norm1_g, norm2_g, w_in, ret_gamma_logit, ret_norm_g, lru_conv_w, lru_conv_b, lru_gate_w, lru_gate_b, lru_lambda, dn_conv_w, dn_a_log, dn_dt_bias, dn_norm_g, w_br, w_out, peer_w_q, peer_sub_keys, peer_u, peer_v, final_norm_g):
    n_ctx, l_ctx, _ = x_prompt.shape
    n_lat, l_lat, _ = x_sample.shape
    t_ctx, t_lat = n_ctx * l_ctx, n_lat * l_lat
    assert t_ctx % l_lat == 0
    tb = 512
    assert l_lat % tb == 0 and t_ctx % tb == 0
    ctx_blocks, per_seq = t_ctx // tb, l_lat // tb

    def mod_row(i):
        return jnp.where(i < ctx_blocks, 0, 1 + (i - ctx_blocks) // per_seq)

    tm = 256
    ctx_blocks_m, per_seq_m = t_ctx // tm, l_lat // tm

    def mod_row_m(i):
        return jnp.where(i < ctx_blocks_m, 0, 1 + (i - ctx_blocks_m) // per_seq_m)

    x = jnp.concatenate([x_prompt.reshape(t_ctx, D_MODEL), x_sample.reshape(t_lat, D_MODEL)], axis=0)
    n_cond = 16
    cond = jnp.zeros((n_cond, D_MODEL), F32).at[0].set(c_ctx).at[1:1 + n_lat].set(c)
    mods = _modulation(cond, w_mod, b_mod).reshape(DEPTH, n_cond, N_MOD, D_MODEL)

    w_in_r = jnp.concatenate(
        [w_in[:, :, :N_MAIN], w_in[:, :, N_MAIN + N_SMALL:], w_in[:, :, N_MAIN:N_MAIN + N_SMALL],
         jnp.zeros((DEPTH, D_MODEL, HEAD_DIM - N_SMALL), F32)], axis=-1).astype(BF16)
    w_br_b, w_out_b = w_br.astype(BF16), w_out.astype(BF16)
    wq_t = jnp.swapaxes(peer_w_q, 1, 2).astype(BF16)
    keys_b = peer_sub_keys.astype(BF16)
    u_b = peer_u.astype(BF16)
    vt_b = jnp.swapaxes(peer_v, 1, 2).astype(BF16)
    gam = _lanes(ret_gamma_logit)
    lane_pad = jnp.zeros((DEPTH, 2, N_HEADS), F32)
    dn_ab = jnp.stack([dn_a_log, dn_dt_bias], axis=1)
    dn_par = jnp.concatenate([dn_ab[:, :, 0], lane_pad, dn_ab[:, :, 1], lane_pad,
                              jnp.zeros((DEPTH, 2, HEAD_DIM - N_SMALL), F32)], axis=-1)
    rope_tabs = _rope_tables(l_lat)
    zero_ret = jnp.zeros((n_ctx, 2, N_HEADS, HEAD_DIM, HEAD_DIM), F32)
    zero_lru = jnp.zeros((n_ctx, 2, D_MODEL), F32)

    mat_zero = pl.BlockSpec((None, 2, None, HEAD_DIM, HEAD_DIM), lambda b, h: (b, 0, h, 0, 0))
    vec_zero = pl.BlockSpec((None, 2, HEAD_DIM), lambda b, n: (b, 0, n))
    row_off_lat = t_ctx // l_lat

    ret_states, lru_states, dn_states = [], [], []
    for l in range(DEPTH):
        mat_lat = pl.BlockSpec((None, None, 2, None, HEAD_DIM, HEAD_DIM), lambda b, h, l=l: (b, l, 0, h, 0, 0))
        vec_lat = pl.BlockSpec((None, None, 2, HEAD_DIM), lambda b, n, l=l: (b, l, 0, n))
        proj = _in_proj(x, mods[l], norm1_g[l][None], w_in_r[l], mod_row, tb)

        ng_ret = ret_norm_g[l][:, None, :]
        o_ret_c, s_ret = _retention(proj, gam[:, l], ng_ret, zero_ret, mat_zero, None, n_ctx, l_ctx, 0)
        o_ret_l, _ = _retention(proj, gam[:, l], ng_ret, state_ret, mat_lat, rope_tabs, n_lat, l_lat, row_off_lat)

        lru_args = (lru_conv_w[l], lru_conv_b[l][None], lru_gate_w[l], lru_gate_b[l], lru_lambda[l])
        o_lru_c, s_lru = _rglru(proj, *lru_args, zero_lru, vec_zero, n_ctx, l_ctx, 0)
        o_lru_l, _ = _rglru(proj, *lru_args, state_lru, vec_lat, n_lat, l_lat, row_off_lat)

        ng_dn = dn_norm_g[l][:, None, :]
        o_dn_c, s_dn = _deltanet(proj, dn_conv_w[l], dn_par[l], ng_dn, zero_ret, None, n_ctx, l_ctx, 0, 4)
        o_dn_l, _ = _deltanet(proj, dn_conv_w[l], dn_par[l], ng_dn, state_dn, l, n_lat, l_lat, row_off_lat, 2)

        o_ret = jnp.concatenate([o_ret_c, o_ret_l], axis=0)
        o_lru = jnp.concatenate([o_lru_c, o_lru_l], axis=0)
        o_dn = jnp.concatenate([o_dn_c, o_dn_l], axis=0)

        x, h2 = _merge(o_ret, o_lru, o_dn, proj, x, mods[l], norm2_g[l][None], w_br_b[l], w_out_b[l],
                       mod_row_m, tm)
        routing = _router(h2, wq_t[l], keys_b[l], tb)
        x = _experts(h2, u_b[l], vt_b[l], routing, x, mods[l], mod_row, tb, 1024)

        ret_states.append(s_ret)
        lru_states.append(s_lru)
        dn_states.append(s_dn)

    y = _final_norm(x, final_norm_g[None], tb)
    y_prompt = y[:t_ctx].reshape(n_ctx, l_ctx, D_MODEL)
    y_sample = y[t_ctx:].reshape(n_lat, l_lat, D_MODEL)
    return (y_prompt, y_sample, jnp.stack(ret_states, axis=1), jnp.stack(lru_states, axis=1),
            jnp.stack(dn_states, axis=1))
```

```python
import functools
import math

import jax
import jax.numpy as jnp
from jax import lax
from jax.experimental import pallas as pl
from jax.experimental.pallas import tpu as pltpu

F32 = jnp.float32
BF16 = jnp.bfloat16

D_MODEL = 1024
DEPTH = 4
N_MOD = 6
EPS = 1e-6
GRID_W = 64
ROPE_BASE = 10000.0
N_HEADS = 8
HEAD_DIM = 128
RET_CHUNK = 128
DN_CHUNK = 64
DN_SUPER = 256
LRU_C = 8.0
N_KEYS = 128
PEER_TOPK = 16
N_EXPERTS = N_KEYS * N_KEYS
SUBLANES = 8
TOP_N = PEER_TOPK + 1
TOP_PAD = -(-TOP_N // SUBLANES) * SUBLANES
N_MULTI = TOP_N - SUBLANES
CAND_ROWS = TOP_PAD + N_MULTI * SUBLANES
assert TOP_N // 2 <= SUBLANES and TOP_N // (N_MULTI + 1) == 1
EXPERT_PART = 256
NEG_BIG = -3.0e38

COL_RQ, COL_RK, COL_RV, COL_RG = 0, 8, 16, 24
COL_LX, COL_LG = 32, 40
COL_DQ, COL_DK, COL_DV, COL_DZ = 48, 56, 64, 72
N_MAIN = 80 * 128
N_SMALL = 4 * N_HEADS
COL_BGATE = 80
COL_SMALL = 104
N_PROJ = 105 * 128

VMEM_LIMIT = 48 * 1024 * 1024


def _cparams(sem):
    return pltpu.CompilerParams(dimension_semantics=sem, vmem_limit_bytes=VMEM_LIMIT)


def _mm(a, b):
    return jnp.dot(a.astype(BF16), b.astype(BF16), preferred_element_type=F32)


def _mm_nt(a, b):
    return lax.dot_general(a.astype(BF16), b.astype(BF16), (((1,), (1,)), ((), ())),
                           preferred_element_type=F32)


def _mm_tn(a, b):
    return lax.dot_general(a.astype(BF16), b.astype(BF16), (((0,), (0,)), ((), ())),
                           preferred_element_type=F32)


def _softplus(x):
    return jnp.maximum(x, 0.0) + jnp.log1p(jnp.exp(-jnp.abs(x)))


def _silu(x):
    return x * jax.nn.sigmoid(x)


def _gelu(x):
    return 0.5 * x * (1.0 + lax.erf(x * math.sqrt(0.5)))


def _rms(x):
    return x * lax.rsqrt(jnp.mean(x * x, axis=-1, keepdims=True) + EPS)


def _shift_rows(x, s, row):
    n = x.shape[0]
    if s == 0:
        return x
    y = pltpu.roll(x, (-s) % n, 0)
    ok = (row + s >= 0) & (row + s < n)
    return jnp.where(ok, y, 0.0)


def _dw_conv(x, w, row):
    y = _shift_rows(x, -2, row) * w[0:1, :]
    y = y + _shift_rows(x, -1, row) * w[1:2, :]
    y = y + x * w[2:3, :]
    y = y + _shift_rows(x, 1, row) * w[3:4, :]
    return y


def _mod_kernel(c_ref, w_ref, b_ref, o_ref):
    c = c_ref[...]
    o_ref[...] = jnp.dot(_silu(c), w_ref[...], precision=lax.Precision.HIGHEST,
                         preferred_element_type=F32) + b_ref[...]


def _modulation(cond, w_mod, b_mod):
    n_rows = cond.shape[0]
    tn = 1536
    n_out = N_MOD * D_MODEL
    return pl.pallas_call(
        _mod_kernel,
        grid=(DEPTH, n_out // tn),
        in_specs=[pl.BlockSpec((n_rows, D_MODEL), lambda l, j: (0, 0)),
                  pl.BlockSpec((None, D_MODEL, tn), lambda l, j: (l, 0, j)),
                  pl.BlockSpec((None, 1, tn), lambda l, j: (l, 0, j))],
        out_specs=pl.BlockSpec((None, n_rows, tn), lambda l, j: (l, 0, j)),
        out_shape=jax.ShapeDtypeStruct((DEPTH, n_rows, n_out), F32),
        compiler_params=_cparams(("parallel", "parallel")),
        name="modulation",
    )(cond, w_mod, b_mod.reshape(DEPTH, 1, n_out))


def _in_proj_kernel(x_ref, mod_ref, g_ref, w_ref, o_ref, h_scr):
    @pl.when(pl.program_id(1) == 0)
    def _():
        y = _rms(x_ref[...]) * g_ref[...]
        h_scr[...] = (y * (1.0 + mod_ref[1:2, :]) + mod_ref[0:1, :]).astype(BF16)

    o_ref[...] = jnp.dot(h_scr[...], w_ref[...], preferred_element_type=F32)


def _in_proj(x, mod, norm_g, w, mod_row, tm):
    t = x.shape[0]
    tn = 1920
    return pl.pallas_call(
        _in_proj_kernel,
        grid=(t // tm, N_PROJ // tn),
        in_specs=[pl.BlockSpec((tm, D_MODEL), lambda i, j: (i, 0)),
                  pl.BlockSpec((None, N_MOD, D_MODEL), lambda i, j: (mod_row(i), 0, 0)),
                  pl.BlockSpec((1, D_MODEL), lambda i, j: (0, 0)),
                  pl.BlockSpec((D_MODEL, tn), lambda i, j: (0, j))],
        out_specs=pl.BlockSpec((tm, tn), lambda i, j: (i, j)),
        out_shape=jax.ShapeDtypeStruct((t, N_PROJ), F32),
        scratch_shapes=[pltpu.VMEM((tm, D_MODEL), BF16)],
        compiler_params=_cparams(("parallel", "arbitrary")),
        name="in_proj",
    )(x, mod, norm_g, w)


def _ret_kernel(*refs, seq_len, rope, hg):
    if rope:
        (q_ref, k_ref, v_ref, g_ref, gam_ref, ng_ref, s0_ref, cs_ref, sn_ref,
         o_ref, so_ref, of_scr, ob_scr) = refs
    else:
        (q_ref, k_ref, v_ref, g_ref, gam_ref, ng_ref, s0_ref,
         o_ref, so_ref, of_scr, ob_scr) = refs
    c = RET_CHUNK
    n_chunks = seq_len // c
    r = lax.broadcasted_iota(jnp.int32, (c, HEAD_DIM), 0).astype(F32)
    ci = lax.broadcasted_iota(jnp.int32, (c, c), 0)
    si = lax.broadcasted_iota(jnp.int32, (c, c), 1)
    dmat = (ci - si).astype(F32)
    scale = HEAD_DIM ** -0.5

    chains = [(hh, d) for hh in range(hg) for d in range(2)]
    idx = range(len(chains))
    dec, qsc, ksc, gch = [], [], [], []
    for hh, d in chains:
        lg = -_softplus(-gam_ref[hh, d:d + 1, :])
        if d == 0:
            dec.append(jnp.where(dmat >= 0, jnp.exp(lg * jnp.maximum(dmat, 0.0)), 0.0))
            qsc.append(jnp.exp(lg * (r + 1.0)))
            ksc.append(jnp.exp(lg * (c - 1.0 - r)))
        else:
            dec.append(jnp.where(dmat <= 0, jnp.exp(lg * jnp.maximum(-dmat, 0.0)), 0.0))
            qsc.append(jnp.exp(lg * (c - r)))
            ksc.append(jnp.exp(lg * r))
        gch.append(jnp.exp(lg * c))

    s = [s0_ref[d, hh] for hh, d in chains]
    for stp in range(n_chunks):
        sl = [pl.ds((stp if d == 0 else n_chunks - 1 - stp) * c, c) for _, d in chains]
        cs = [slice(hh * HEAD_DIM, (hh + 1) * HEAD_DIM) for hh, _ in chains]
        q = [q_ref[sl[i], cs[i]] for i in idx]
        k = [k_ref[sl[i], cs[i]] * scale for i in idx]
        v = [v_ref[sl[i], cs[i]] for i in idx]
        if rope:
            cos = [cs_ref[sl[i], :] for i in idx]
            sin = [sn_ref[sl[i], :] for i in idx]
            q = [q[i] * cos[i] + pltpu.roll(q[i], HEAD_DIM // 2, 1) * sin[i] for i in idx]
            k = [k[i] * cos[i] + pltpu.roll(k[i], HEAD_DIM // 2, 1) * sin[i] for i in idx]
        sc = [_mm_nt(q[i], k[i]) for i in idx]
        qs = [_mm(q[i] * qsc[i], s[i]) for i in idx]
        kv = [_mm_tn(k[i] * ksc[i], v[i]) for i in idx]
        oi = [_mm(sc[i] * dec[i], v[i]) for i in idx]
        s = [s[i] * gch[i] + kv[i] for i in idx]
        for i in idx:
            if chains[i][1] == 0:
                of_scr[sl[i], cs[i]] = oi[i] + qs[i]
            else:
                ob_scr[sl[i], cs[i]] = oi[i] + qs[i]
    for i in idx:
        so_ref[chains[i][1], chains[i][0]] = s[i]
    tot = of_scr[...] + ob_scr[...]
    g = g_ref[...]
    for hh in range(hg):
        cs1 = slice(hh * HEAD_DIM, (hh + 1) * HEAD_DIM)
        o_ref[:, cs1] = _rms(tot[:, cs1]) * ng_ref[hh] * _silu(g[:, cs1])


def _retention(proj, gam, norm_g, s0, layer, rope_tabs, n_seq, seq_len, row_off, hg):
    rope = rope_tabs is not None
    wide = hg * HEAD_DIM

    def col(c0):
        return pl.BlockSpec((seq_len, wide), lambda b, g: (row_off + b, c0 // hg + g))

    in_specs = [col(COL_RQ), col(COL_RK), col(COL_RV), col(COL_RG),
                pl.BlockSpec((hg, 2, HEAD_DIM), lambda b, g: (g, 0, 0)),
                pl.BlockSpec((hg, 1, HEAD_DIM), lambda b, g: (g, 0, 0)),
                _state_spec(s0, layer, hg)]
    args = [proj, proj, proj, proj, gam, norm_g, s0]
    if rope:
        tab = pl.BlockSpec((seq_len, HEAD_DIM), lambda b, g: (0, 0))
        in_specs += [tab, tab]
        args += list(rope_tabs)
    return pl.pallas_call(
        functools.partial(_ret_kernel, seq_len=seq_len, rope=rope, hg=hg),
        grid=(n_seq, N_HEADS // hg),
        in_specs=in_specs,
        out_specs=[pl.BlockSpec((seq_len, wide), lambda b, g: (b, g)),
                   pl.BlockSpec((None, 2, hg, HEAD_DIM, HEAD_DIM), lambda b, g: (b, 0, g, 0, 0))],
        out_shape=[jax.ShapeDtypeStruct((n_seq * seq_len, D_MODEL), F32),
                   jax.ShapeDtypeStruct((n_seq, 2, N_HEADS, HEAD_DIM, HEAD_DIM), F32)],
        scratch_shapes=[pltpu.VMEM((seq_len, wide), F32), pltpu.VMEM((seq_len, wide), F32)],
        compiler_params=_cparams(("parallel", "parallel")),
        name="retention_rope" if rope else "retention",
    )(*args)


def _lru_kernel(x_ref, gate_ref, cw_ref, cb_ref, gw_ref, gb_ref, lam_ref, s0_ref, o_ref, so_ref, *, seq_len):
    n = seq_len
    row = lax.broadcasted_iota(jnp.int32, (n, HEAD_DIM), 0)
    xc = _dw_conv(x_ref[...], cw_ref[...], row) + cb_ref[...]
    lam = lam_ref[...]
    hs = []
    for d in range(2):
        r_gate = jax.nn.sigmoid(_mm(xc, gw_ref[d, 0]) + gb_ref[d, 0:1, :])
        i_gate = jax.nn.sigmoid(_mm(xc, gw_ref[d, 1]) + gb_ref[d, 1:2, :])
        log_a = -LRU_C * r_gate * _softplus(-lam[d:d + 1, :])
        a = jnp.exp(log_a)
        u = jnp.sqrt(-jnp.tanh(log_a) * (1.0 + a * a)) * i_gate * xc
        step = 1
        while step < n:
            if d == 0:
                ok = row >= step
                sh = step
            else:
                ok = row < n - step
                sh = n - step
            a_sh = jnp.where(ok, pltpu.roll(a, sh, 0), 1.0)
            u_sh = jnp.where(ok, pltpu.roll(u, sh, 0), 0.0)
            u = a * u_sh + u
            a = a * a_sh
            step *= 2
        h = u + a * s0_ref[d:d + 1, :]
        hs.append(h)
        so_ref[d:d + 1, :] = h[n - 1:n, :] if d == 0 else h[0:1, :]
    o_ref[...] = (hs[0] + hs[1]) * _gelu(gate_ref[...])


def _rglru(proj, conv_w, conv_b, gate_w, gate_b, lam, s0, s0_map, n_seq, seq_len, row_off):
    return pl.pallas_call(
        functools.partial(_lru_kernel, seq_len=seq_len),
        grid=(n_seq, N_HEADS),
        in_specs=[pl.BlockSpec((seq_len, HEAD_DIM), lambda b, n: (row_off + b, COL_LX + n)),
                  pl.BlockSpec((seq_len, HEAD_DIM), lambda b, n: (row_off + b, COL_LG + n)),
                  pl.BlockSpec((4, HEAD_DIM), lambda b, n: (0, n)),
                  pl.BlockSpec((1, HEAD_DIM), lambda b, n: (0, n)),
                  pl.BlockSpec((2, 2, None, HEAD_DIM, HEAD_DIM), lambda b, n: (0, 0, n, 0, 0)),
                  pl.BlockSpec((2, 2, HEAD_DIM), lambda b, n: (0, 0, n)),
                  pl.BlockSpec((2, HEAD_DIM), lambda b, n: (0, n)),
                  s0_map],
        out_specs=[pl.BlockSpec((seq_len, HEAD_DIM), lambda b, n: (b, n)),
                   pl.BlockSpec((None, 2, HEAD_DIM), lambda b, n: (b, 0, n))],
        out_shape=[jax.ShapeDtypeStruct((n_seq * seq_len, D_MODEL), F32),
                   jax.ShapeDtypeStruct((n_seq, 2, D_MODEL), F32)],
        compiler_params=_cparams(("parallel", "parallel")),
        name="rglru",
    )(proj, proj, conv_w, conv_b, gate_w, gate_b, lam, s0)


def _dn_kernel(q_ref, k_ref, v_ref, z_ref, sm_ref, cwq_ref, cwk_ref, cwv_ref, par_ref, ng_ref, s0_ref,
               o_ref, so_ref, q_scr, k_scr, v_scr, c_scr, b_scr, of_scr, ob_scr, *, seq_len, hg):
    n = seq_len
    cc = DN_CHUNK
    sc = DN_SUPER
    n_super = n // sc
    head0 = pl.program_id(1) * hg
    roww = lax.broadcasted_iota(jnp.int32, (n, hg * HEAD_DIM), 0)
    row = lax.broadcasted_iota(jnp.int32, (n, HEAD_DIM), 0)
    lane = lax.broadcasted_iota(jnp.int32, (n, HEAD_DIM), 1)
    pos = row & (cc - 1)

    xq = _silu(_dw_conv(q_ref[...], cwq_ref[...], roww))
    xk = _silu(_dw_conv(k_ref[...], cwk_ref[...], roww))
    v_scr[...] = _silu(_dw_conv(v_ref[...], cwv_ref[...], roww))
    small = sm_ref[...]
    par = par_ref[...]
    g_all = -jnp.exp(par[0:1, :]) * _softplus(small + par[1:2, :])
    beta_all = jax.nn.sigmoid(small)
    cum_all = [g_all, g_all]
    step = 1
    while step < cc:
        cum_all[0] = cum_all[0] + jnp.where(pos >= step, pltpu.roll(cum_all[0], step, 0), 0.0)
        cum_all[1] = cum_all[1] + jnp.where(pos < cc - step, pltpu.roll(cum_all[1], n - step, 0), 0.0)
        step *= 2
    for hh in range(hg):
        cs = slice(hh * HEAD_DIM, (hh + 1) * HEAD_DIM)
        xqh, xkh = xq[:, cs], xk[:, cs]
        q_scr[:, cs] = xqh * lax.rsqrt(jnp.sum(xqh * xqh, axis=-1, keepdims=True) + EPS) * (HEAD_DIM ** -0.5)
        k_scr[:, cs] = xkh * lax.rsqrt(jnp.sum(xkh * xkh, axis=-1, keepdims=True) + EPS)
        head = head0 + hh
        for d in range(2):
            a_lane = lane == 2 * N_HEADS * d + head
            b_lane = lane == 2 * N_HEADS * d + N_HEADS + head
            cum = jnp.sum(jnp.where(a_lane, cum_all[d], 0.0), axis=-1, keepdims=True)
            beta = jnp.sum(jnp.where(b_lane, beta_all, 0.0), axis=-1, keepdims=True)
            c_scr[d, hh] = jnp.broadcast_to(cum, (n, HEAD_DIM))
            b_scr[d, hh] = jnp.broadcast_to(beta, (n, HEAD_DIM))

    ri = lax.broadcasted_iota(jnp.int32, (sc, sc), 0)
    cj = lax.broadcasted_iota(jnp.int32, (sc, sc), 1)
    sh = cc.bit_length() - 1
    same = (ri >> sh) == (cj >> sh)
    incl = (same & (ri >= cj), same & (ri <= cj))
    strict = (same & (ri > cj), same & (ri < cj))
    eye = jnp.where(ri == cj, 1.0, 0.0)
    level = [(ri >> 3) == (cj >> 3)]
    for b in range(4, sh + 1):
        level.append(((ri >> b) == (cj >> b)) & ((ri >> (b - 1)) != (cj >> (b - 1))))

    n_ch = sc // cc

    def super_chunks(chains):
        idx = range(len(chains))
        dd = [c[0] for c in chains]
        sl = [pl.ds(c[2], sc) for c in chains]
        cs = [slice(c[1] * HEAD_DIM, (c[1] + 1) * HEAD_DIM) for c in chains]
        q = [q_scr[sl[c], cs[c]] for c in idx]
        k = [k_scr[sl[c], cs[c]] for c in idx]
        v = [v_scr[sl[c], cs[c]] for c in idx]
        cum = [c_scr[dd[c], chains[c][1], sl[c], :] for c in idx]
        beta = [b_scr[dd[c], chains[c][1], sl[c], :] for c in idx]
        kk = [_mm_nt(k[c], k[c]) for c in idx]
        qk = [_mm_nt(q[c], k[c]) for c in idx]
        decay, x, attn = [], [], []
        for c in idx:
            cb = jnp.concatenate([cum[c], cum[c]], axis=1)
            diff = cb - cb.T
            dec = jnp.where(incl[dd[c]], jnp.exp(jnp.where(incl[dd[c]], diff, 0.0)), 0.0)
            bb = jnp.concatenate([beta[c], beta[c]], axis=1)
            x.append(-jnp.where(strict[dd[c]], kk[c] * bb * dec, 0.0))
            attn.append(qk[c] * dec)
        xp = [jnp.where(level[0], x[c], 0.0) for c in idx]
        p = [eye + xp[c] for c in idx]
        for _ in range(2):
            xp = [_mm(xp[c], xp[c]) for c in idx]
            pm = [_mm(p[c], xp[c]) for c in idx]
            p = [p[c] + pm[c] for c in idx]
        for lv in range(1, len(level)):
            t1 = [_mm(jnp.where(level[lv], x[c], 0.0), p[c]) for c in idx]
            t2 = [_mm(p[c], t1[c]) for c in idx]
            p = [p[c] + t2[c] for c in idx]
        rhs = [jnp.concatenate([v[c] * beta[c], k[c] * beta[c] * jnp.exp(cum[c])], axis=1) for c in idx]
        sol = [_mm(p[c], rhs[c]) for c in idx]
        qd = [q[c] * jnp.exp(cum[c]) for c in idx]
        s = [so_ref[dd[c], chains[c][1]] for c in idx]
        v_new = [[None] * n_ch for _ in idx]
        o_inter = [[None] * n_ch for _ in idx]
        for stp in range(n_ch):
            ch = [stp if dd[c] == 0 else n_ch - 1 - stp for c in idx]
            rs = [slice(ch[c] * cc, (ch[c] + 1) * cc) for c in idx]
            ws = [_mm(sol[c][rs[c], HEAD_DIM:], s[c]) for c in idx]
            for c in idx:
                o_inter[c][ch[c]] = _mm(qd[c][rs[c], :], s[c])
            kt = []
            for c in idx:
                cum_c = cum[c][rs[c], :]
                tot = cum_c[cc - 1:cc, :] if dd[c] == 0 else cum_c[0:1, :]
                v_new[c][ch[c]] = sol[c][rs[c], :HEAD_DIM] - ws[c]
                kt.append((k[c][rs[c], :] * jnp.exp(tot - cum_c), jnp.exp(tot)))
            upd = [_mm_tn(kt[c][0], v_new[c][ch[c]]) for c in idx]
            s = [s[c] * kt[c][1] + upd[c] for c in idx]
        oi = [_mm(attn[c], jnp.concatenate(v_new[c], axis=0)) for c in idx]
        for c in idx:
            o = jnp.concatenate(o_inter[c], axis=0) + oi[c]
            if dd[c] == 0:
                of_scr[sl[c], cs[c]] = o
            else:
                ob_scr[sl[c], cs[c]] = o
            so_ref[dd[c], chains[c][1]] = s[c]

    so_ref[...] = s0_ref[...]

    def body(i, carry):
        fwd = pl.multiple_of(i * sc, sc)
        bwd = pl.multiple_of((n_super - 1 - i) * sc, sc)
        super_chunks([(d, hh, fwd if d == 0 else bwd) for hh in range(hg) for d in range(2)])
        return carry

    lax.fori_loop(0, n_super, body, 0)
    tot = of_scr[...] + ob_scr[...]
    z = z_ref[...]
    for hh in range(hg):
        cs = slice(hh * HEAD_DIM, (hh + 1) * HEAD_DIM)
        o_ref[:, cs] = _rms(tot[:, cs]) * ng_ref[hh] * _silu(z[:, cs])


def _state_spec(s0, layer, hg):
    if layer is None:
        return pl.BlockSpec((None, 2, hg, HEAD_DIM, HEAD_DIM), lambda b, g: (b, 0, g, 0, 0))
    return pl.BlockSpec((None, None, 2, hg, HEAD_DIM, HEAD_DIM), lambda b, g: (b, layer, 0, g, 0, 0))


def _deltanet(proj, conv_w, par, norm_g, s0, layer, n_seq, seq_len, row_off, hg):
    wide = hg * HEAD_DIM

    def col(c0):
        return pl.BlockSpec((seq_len, wide), lambda b, g: (row_off + b, c0 // hg + g))

    def cw(c0):
        return pl.BlockSpec((4, wide), lambda b, g: (0, c0 // hg + g))

    vm = functools.partial(pltpu.VMEM, dtype=F32)
    return pl.pallas_call(
        functools.partial(_dn_kernel, seq_len=seq_len, hg=hg),
        grid=(n_seq, N_HEADS // hg),
        in_specs=[col(COL_DQ), col(COL_DK), col(COL_DV), col(COL_DZ),
                  pl.BlockSpec((seq_len, HEAD_DIM), lambda b, g: (row_off + b, COL_SMALL)),
                  cw(0), cw(N_HEADS), cw(2 * N_HEADS),
                  pl.BlockSpec((2, HEAD_DIM), lambda b, g: (0, 0)),
                  pl.BlockSpec((hg, 1, HEAD_DIM), lambda b, g: (g, 0, 0)),
                  _state_spec(s0, layer, hg)],
        out_specs=[pl.BlockSpec((seq_len, wide), lambda b, g: (b, g)),
                   pl.BlockSpec((None, 2, hg, HEAD_DIM, HEAD_DIM), lambda b, g: (b, 0, g, 0, 0))],
        out_shape=[jax.ShapeDtypeStruct((n_seq * seq_len, D_MODEL), F32),
                   jax.ShapeDtypeStruct((n_seq, 2, N_HEADS, HEAD_DIM, HEAD_DIM), F32)],
        scratch_shapes=[vm((seq_len, wide)), vm((seq_len, wide)), vm((seq_len, wide)),
                        vm((2, hg, seq_len, HEAD_DIM)), vm((2, hg, seq_len, HEAD_DIM)),
                        vm((seq_len, wide)), vm((seq_len, wide))],
        compiler_params=_cparams(("parallel", "parallel")),
        name="deltanet",
    )(proj, proj, proj, proj, proj, conv_w, conv_w, conv_w, par, norm_g, s0)


def _merge_kernel(ret_ref, lru_ref, dn_ref, g0_ref, g1_ref, g2_ref, x_ref, mod_ref, ng_ref, wbr_ref, wout_ref,
                  xo_ref, h_ref):
    merged = jax.nn.sigmoid(g0_ref[...]) * _mm(ret_ref[...], wbr_ref[0])
    merged = merged + jax.nn.sigmoid(g1_ref[...]) * _mm(lru_ref[...], wbr_ref[1])
    merged = merged + jax.nn.sigmoid(g2_ref[...]) * _mm(dn_ref[...], wbr_ref[2])
    x = x_ref[...] + mod_ref[2:3, :] * _mm(merged, wout_ref[...])
    xo_ref[...] = x
    h_ref[...] = (_rms(x) * ng_ref[...] * (1.0 + mod_ref[4:5, :]) + mod_ref[3:4, :]).T.astype(BF16)


def _merge(o_ret, o_lru, o_dn, proj, x, mod, norm_g, w_br, w_out, mod_row, tm):
    t = x.shape[0]
    row = pl.BlockSpec((tm, D_MODEL), lambda i: (i, 0))
    col0 = COL_BGATE * HEAD_DIM // D_MODEL

    def gate(k):
        return pl.BlockSpec((tm, D_MODEL), lambda i: (i, col0 + k))

    return pl.pallas_call(
        _merge_kernel,
        grid=(t // tm,),
        in_specs=[row, row, row, gate(0), gate(1), gate(2), row,
                  pl.BlockSpec((None, N_MOD, D_MODEL), lambda i: (mod_row(i), 0, 0)),
                  pl.BlockSpec((1, D_MODEL), lambda i: (0, 0)),
                  pl.BlockSpec((3, D_MODEL, D_MODEL), lambda i: (0, 0, 0)),
                  pl.BlockSpec((D_MODEL, D_MODEL), lambda i: (0, 0))],
        out_specs=[row, pl.BlockSpec((D_MODEL, tm), lambda i: (0, i))],
        out_shape=[jax.ShapeDtypeStruct((t, D_MODEL), F32), jax.ShapeDtypeStruct((D_MODEL, t), BF16)],
        compiler_params=_cparams(("parallel",)),
        name="merge",
    )(o_ret, o_lru, o_dn, proj, proj, proj, x, mod, norm_g, w_br, w_out)


def _router_kernel(h_ref, wq_ref, keys_ref, thr_ref, e0_ref, e1_ref, q_scr, top_scr, cand_scr):
    tb = h_ref.shape[1]
    q_scr[...] = jnp.dot(wq_ref[...], h_ref[...], preferred_element_type=F32)

    top_scr[...] = jnp.full(top_scr.shape, NEG_BIG, F32)
    sub_row = lax.broadcasted_iota(jnp.int32, (SUBLANES, tb), 0)

    def top_values(x, dst):
        for kk in range(TOP_N):
            m = jnp.max(x, axis=0, keepdims=True)
            top_scr[dst, kk:kk + 1, :] = m
            x = jnp.where(x >= m, NEG_BIG, x)

    def body(hd, carry):
        base = pl.multiple_of(hd * 2 * N_KEYS, 2 * N_KEYS)
        s0 = _mm(keys_ref[0], q_scr[pl.ds(base, N_KEYS), :])
        s1 = _mm(keys_ref[1], q_scr[pl.ds(base + N_KEYS, N_KEYS), :])
        top_values(s0, 0)
        top_values(s1, 1)
        cand_scr[0:TOP_PAD, :] = top_scr[0, 0:1, :] + top_scr[1]
        a1 = top_scr[1, 0:SUBLANES, :]
        for p in range(1, N_MULTI):
            keep = sub_row < TOP_N // (p + 1)
            cand_scr[TOP_PAD + (p - 1) * SUBLANES:TOP_PAD + p * SUBLANES, :] = jnp.where(
                keep, top_scr[0, p:p + 1, :] + a1, NEG_BIG)
        cand_scr[CAND_ROWS - SUBLANES:CAND_ROWS, :] = top_scr[0, N_MULTI:TOP_N, :] + top_scr[1, 0:1, :]
        x = cand_scr[...]
        m0 = jnp.max(x, axis=0, keepdims=True)
        z = jnp.zeros_like(m0)
        m = m0
        for kk in range(PEER_TOPK):
            if kk > 0:
                x = jnp.where(x >= m, NEG_BIG, x)
                m = jnp.max(x, axis=0, keepdims=True)
            z = z + jnp.exp(m - m0)
        m_next = jnp.max(jnp.where(x >= m, NEG_BIG, x), axis=0, keepdims=True)
        tau = 0.5 * (m + m_next)
        max1 = top_scr[1, 0:1, :]
        thr_ref[hd] = jnp.exp(tau - s0 - max1) / z
        e0_ref[hd] = jnp.exp(s0 - top_scr[0, 0:1, :])
        e1_ref[hd] = jnp.exp(s1 - max1) / z
        return carry

    lax.fori_loop(0, N_HEADS, body, 0)


def _router(h2, wq_t, keys, tb):
    t = h2.shape[1]
    out = pl.BlockSpec((N_HEADS, N_KEYS, tb), lambda i: (0, 0, i))
    shp = jax.ShapeDtypeStruct((N_HEADS, N_KEYS, t), F32)
    return pl.pallas_call(
        _router_kernel,
        grid=(t // tb,),
        in_specs=[pl.BlockSpec((D_MODEL, tb), lambda i: (0, i)),
                  pl.BlockSpec((2 * N_KEYS * N_HEADS, D_MODEL), lambda i: (0, 0)),
                  pl.BlockSpec((2, N_KEYS, N_KEYS), lambda i: (0, 0, 0))],
        out_specs=[out, out, out],
        out_shape=[shp, shp, shp],
        scratch_shapes=[pltpu.VMEM((2 * N_KEYS * N_HEADS, tb), F32),
                        pltpu.VMEM((2, TOP_PAD, tb), F32),
                        pltpu.VMEM((CAND_ROWS, tb), F32)],
        compiler_params=_cparams(("parallel",)),
        name="peer_router",
    )(h2, wq_t, keys)


def _expert_kernel(h_ref, u0_ref, un_ref, vt_ref, thr_ref, e0_ref, e1_ref, x_ref, mod_ref, o_ref,
                   acc_scr, g_scr, act_a, act_b, *, tile_e):
    act_scr = (act_a, act_b)
    j = pl.program_id(1)
    tb = h_ref.shape[1]
    n_sub = tile_e // N_KEYS

    @pl.when(j == 0)
    def _():
        acc_scr[...] = jnp.zeros_like(acc_scr)
        act_scr[0][...] = jnp.dot(u0_ref[...], h_ref[...], preferred_element_type=F32)

    i0 = pl.multiple_of(j * n_sub, n_sub)
    n_part = tile_e // EXPERT_PART
    keys_per_part = EXPERT_PART // N_KEYS

    def step(cur, nxt):
        act_scr[nxt][...] = jnp.dot(un_ref[...], h_ref[...], preferred_element_type=F32)
        acc = acc_scr[...]
        for part in range(n_part):
            for c in range(tb // HEAD_DIM):
                ls = slice(c * HEAD_DIM, (c + 1) * HEAD_DIM)
                thr = [thr_ref[hd, pl.ds(i0, n_sub), ls] for hd in range(N_HEADS)]
                e0 = [e0_ref[hd, pl.ds(i0, n_sub), ls] for hd in range(N_HEADS)]
                for r in range(keys_per_part):
                    ii = part * keys_per_part + r
                    rs = slice(ii * N_KEYS, (ii + 1) * N_KEYS)
                    wd = jnp.zeros((N_KEYS, HEAD_DIM), F32)
                    for hd in range(N_HEADS):
                        e1 = e1_ref[hd, :, ls]
                        wd = wd + jnp.where(e1 >= thr[hd][ii:ii + 1, :], e1 * e0[hd][ii:ii + 1, :], 0.0)
                    g_scr[rs, ls] = (_gelu(act_scr[cur][rs, ls]) * wd).astype(BF16)
            rows = slice(part * EXPERT_PART, (part + 1) * EXPERT_PART)
            acc = acc + jnp.dot(vt_ref[:, rows], g_scr[rows, :], preferred_element_type=F32)
        acc_scr[...] = acc

    slot = lax.rem(j, 2)

    @pl.when(slot == 0)
    def _():
        step(0, 1)

    @pl.when(slot == 1)
    def _():
        step(1, 0)

    @pl.when(j == pl.num_programs(1) - 1)
    def _():
        o_ref[...] = x_ref[...] + mod_ref[5:6, :] * acc_scr[...].T


def _experts(h2t, u_tab, vt_tab, routing, x, mod, mod_row, tb, tile_e):
    t = h2t.shape[1]
    n_exp = u_tab.shape[0]
    rt = pl.BlockSpec((N_HEADS, N_KEYS, tb), lambda i, j: (0, 0, i))
    n_tiles = n_exp // tile_e
    return pl.pallas_call(
        functools.partial(_expert_kernel, tile_e=tile_e),
        grid=(t // tb, n_tiles),
        in_specs=[pl.BlockSpec((D_MODEL, tb), lambda i, j: (0, i)),
                  pl.BlockSpec((tile_e, D_MODEL), lambda i, j: (0, 0)),
                  pl.BlockSpec((tile_e, D_MODEL), lambda i, j: (jnp.minimum(j + 1, n_tiles - 1), 0)),
                  pl.BlockSpec((D_MODEL, tile_e), lambda i, j: (0, j)),
                  rt, rt, rt,
                  pl.BlockSpec((tb, D_MODEL), lambda i, j: (i, 0)),
                  pl.BlockSpec((None, N_MOD, D_MODEL), lambda i, j: (mod_row(i), 0, 0))],
        out_specs=pl.BlockSpec((tb, D_MODEL), lambda i, j: (i, 0)),
        out_shape=jax.ShapeDtypeStruct((t, D_MODEL), F32),
        scratch_shapes=[pltpu.VMEM((D_MODEL, tb), F32), pltpu.VMEM((tile_e, tb), BF16),
                        pltpu.VMEM((tile_e, tb), F32), pltpu.VMEM((tile_e, tb), F32)],
        compiler_params=_cparams(("parallel", "arbitrary")),
        name="peer_experts",
    )(h2t, u_tab, u_tab, vt_tab, *routing, x, mod)


def _final_norm_kernel(x_ref, g_ref, o_ref):
    o_ref[...] = _rms(x_ref[...]) * g_ref[...]


def _final_norm(x, g, tm):
    t = x.shape[0]
    row = pl.BlockSpec((tm, D_MODEL), lambda i: (i, 0))
    return pl.pallas_call(
        _final_norm_kernel, grid=(t // tm,),
        in_specs=[row, pl.BlockSpec((1, D_MODEL), lambda i: (0, 0))],
        out_specs=row, out_shape=jax.ShapeDtypeStruct((t, D_MODEL), F32),
        compiler_params=_cparams(("parallel",)), name="final_norm",
    )(x, g)


def _rope_tables(seq_len):
    rows = seq_len // GRID_W
    r = jnp.repeat(jnp.arange(rows, dtype=F32), GRID_W)
    col = jnp.tile(jnp.arange(GRID_W, dtype=F32), rows)
    nf = HEAD_DIM // 4
    inv = ROPE_BASE ** (-jnp.arange(nf, dtype=F32) / nf)
    ang = jnp.concatenate([r[:, None] * inv, col[:, None] * inv], axis=-1)
    cos, sin = jnp.cos(ang), jnp.sin(ang)
    return jnp.concatenate([cos, cos], axis=-1), jnp.concatenate([-sin, sin], axis=-1)


def _lanes(a):
    return jnp.broadcast_to(jnp.moveaxis(a, -1, 0)[..., None], (a.shape[-1],) + a.shape[:-1] + (HEAD_DIM,))


def kernel(x_prompt, x_sample, c, state_ret, state_lru, state_dn, c_ctx, w_mod, b_mod, norm1_g, norm2_g, w_in, ret_gamma_logit, ret_norm_g, lru_conv_w, lru_conv_b, lru_gate_w, lru_gate_b, lru_lambda, dn_conv_w, dn_a_log, dn_dt_bias, dn_norm_g, w_br, w_out, peer_w_q, peer_sub_keys, peer_u, peer_v, final_norm_g):
    n_ctx, l_ctx, _ = x_prompt.shape
    n_lat, l_lat, _ = x_sample.shape
    t_ctx, t_lat = n_ctx * l_ctx, n_lat * l_lat
    assert t_ctx % l_lat == 0
    tb = 512
    assert l_lat % tb == 0 and t_ctx % tb == 0
    ctx_blocks, per_seq = t_ctx // tb, l_lat // tb

    def mod_row(i):
        return jnp.where(i < ctx_blocks, 0, 1 + (i - ctx_blocks) // per_seq)

    tm = 256
    ctx_blocks_m, per_seq_m = t_ctx // tm, l_lat // tm

    def mod_row_m(i):
        return jnp.where(i < ctx_blocks_m, 0, 1 + (i - ctx_blocks_m) // per_seq_m)

    x = jnp.concatenate([x_prompt.reshape(t_ctx, D_MODEL), x_sample.reshape(t_lat, D_MODEL)], axis=0)
    n_cond = 16
    cond = jnp.zeros((n_cond, D_MODEL), F32).at[0].set(c_ctx).at[1:1 + n_lat].set(c)
    mods = _modulation(cond, w_mod, b_mod).reshape(DEPTH, n_cond, N_MOD, D_MODEL)

    w_in_r = jnp.concatenate(
        [w_in[:, :, :N_MAIN], w_in[:, :, N_MAIN + N_SMALL:], w_in[:, :, N_MAIN:N_MAIN + N_SMALL],
         jnp.zeros((DEPTH, D_MODEL, HEAD_DIM - N_SMALL), F32)], axis=-1).astype(BF16)
    w_br_b, w_out_b = w_br.astype(BF16), w_out.astype(BF16)
    wq_t = jnp.swapaxes(peer_w_q, 1, 2).astype(BF16)
    keys_b = peer_sub_keys.astype(BF16)
    u_b = peer_u.astype(BF16)
    vt_b = jnp.swapaxes(peer_v, 1, 2).astype(BF16)
    gam = _lanes(ret_gamma_logit)
    lane_pad = jnp.zeros((DEPTH, 2, N_HEADS), F32)
    dn_ab = jnp.stack([dn_a_log, dn_dt_bias], axis=1)
    dn_par = jnp.concatenate([dn_ab[:, :, 0], lane_pad, dn_ab[:, :, 1], lane_pad,
                              jnp.zeros((DEPTH, 2, HEAD_DIM - N_SMALL), F32)], axis=-1)
    rope_tabs = _rope_tables(l_lat)
    zero_ret = jnp.zeros((n_ctx, 2, N_HEADS, HEAD_DIM, HEAD_DIM), F32)
    zero_lru = jnp.zeros((n_ctx, 2, D_MODEL), F32)

    vec_zero = pl.BlockSpec((None, 2, HEAD_DIM), lambda b, n: (b, 0, n))
    row_off_lat = t_ctx // l_lat

    ret_states, lru_states, dn_states = [], [], []
    for l in range(DEPTH):
        vec_lat = pl.BlockSpec((None, None, 2, HEAD_DIM), lambda b, n, l=l: (b, l, 0, n))
        proj = _in_proj(x, mods[l], norm1_g[l][None], w_in_r[l],
                        lambda i: jnp.where(i < row_off_lat, 0, 1 + i - row_off_lat), l_lat)

        ng_ret = ret_norm_g[l][:, None, :]
        o_ret_c, s_ret = _retention(proj, gam[:, l], ng_ret, zero_ret, None, None, n_ctx, l_ctx, 0, 4)
        o_ret_l, _ = _retention(proj, gam[:, l], ng_ret, state_ret, l, rope_tabs, n_lat, l_lat, row_off_lat, 4)

        lru_args = (lru_conv_w[l], lru_conv_b[l][None], lru_gate_w[l], lru_gate_b[l], lru_lambda[l])
        o_lru_c, s_lru = _rglru(proj, *lru_args, zero_lru, vec_zero, n_ctx, l_ctx, 0)
        o_lru_l, _ = _rglru(proj, *lru_args, state_lru, vec_lat, n_lat, l_lat, row_off_lat)

        ng_dn = dn_norm_g[l][:, None, :]
        o_dn_c, s_dn = _deltanet(proj, dn_conv_w[l], dn_par[l], ng_dn, zero_ret, None, n_ctx, l_ctx, 0, 4)
        o_dn_l, _ = _deltanet(proj, dn_conv_w[l], dn_par[l], ng_dn, state_dn, l, n_lat, l_lat, row_off_lat, 2)

        o_ret = jnp.concatenate([o_ret_c, o_ret_l], axis=0)
        o_lru = jnp.concatenate([o_lru_c, o_lru_l], axis=0)
        o_dn = jnp.concatenate([o_dn_c, o_dn_l], axis=0)

        x, h2 = _merge(o_ret, o_lru, o_dn, proj, x, mods[l], norm2_g[l][None], w_br_b[l], w_out_b[l],
                       mod_row_m, tm)
        routing = _router(h2, wq_t[l], keys_b[l], tb)
        x = _experts(h2, u_b[l], vt_b[l], routing, x, mods[l], mod_row, tb, 1024)

        ret_states.append(s_ret)
        lru_states.append(s_lru)
        dn_states.append(s_dn)

    y = _final_norm(x, final_norm_g[None], tb)
    y_prompt = y[:t_ctx].reshape(n_ctx, l_ctx, D_MODEL)
    y_sample = y[t_ctx:].reshape(n_lat, l_lat, D_MODEL)
    return (y_prompt, y_sample, jnp.stack(ret_states, axis=1), jnp.stack(lru_states, axis=1),
            jnp.stack(dn_states, axis=1))
```

```python
import functools
import math

import jax
import jax.numpy as jnp
from jax import lax
from jax.experimental import pallas as pl
from jax.experimental.pallas import tpu as pltpu

F32 = jnp.float32
BF16 = jnp.bfloat16

D_MODEL = 1024
DEPTH = 4
N_MOD = 6
EPS = 1e-6
GRID_W = 64
ROPE_BASE = 10000.0
N_HEADS = 8
HEAD_DIM = 128
RET_CHUNK = 128
DN_CHUNK = 64
DN_SUPER = 256
LRU_C = 8.0
N_KEYS = 128
PEER_TOPK = 16
N_EXPERTS = N_KEYS * N_KEYS
SUBLANES = 8
TOP_N = PEER_TOPK + 1
TOP_PAD = -(-TOP_N // SUBLANES) * SUBLANES
N_MULTI = TOP_N - SUBLANES
CAND_ROWS = TOP_PAD + N_MULTI * SUBLANES
assert TOP_N // 2 <= SUBLANES and TOP_N // (N_MULTI + 1) == 1
EXPERT_PART = 256
NEG_BIG = -3.0e38

COL_RQ, COL_RK, COL_RV, COL_RG = 0, 8, 16, 24
COL_LX, COL_LG = 32, 40
COL_DQ, COL_DK, COL_DV, COL_DZ = 48, 56, 64, 72
N_MAIN = 80 * 128
N_SMALL = 4 * N_HEADS
COL_BGATE = 80
COL_SMALL = 104
N_PROJ = 105 * 128

VMEM_LIMIT = 48 * 1024 * 1024
VMEM_LIMIT_WIDE = 58 * 1024 * 1024


def _cparams(sem, vmem_limit=VMEM_LIMIT):
    return pltpu.CompilerParams(dimension_semantics=sem, vmem_limit_bytes=vmem_limit)


def _mm(a, b):
    return jnp.dot(a.astype(BF16), b.astype(BF16), preferred_element_type=F32)


def _mm_nt(a, b):
    return lax.dot_general(a.astype(BF16), b.astype(BF16), (((1,), (1,)), ((), ())),
                           preferred_element_type=F32)


def _mm_tn(a, b):
    return lax.dot_general(a.astype(BF16), b.astype(BF16), (((0,), (0,)), ((), ())),
                           preferred_element_type=F32)


def _softplus(x):
    return jnp.maximum(x, 0.0) + jnp.log1p(jnp.exp(-jnp.abs(x)))


def _silu(x):
    return x * jax.nn.sigmoid(x)


def _gelu(x):
    return 0.5 * x * (1.0 + lax.erf(x * math.sqrt(0.5)))


def _rms(x):
    return x * lax.rsqrt(jnp.mean(x * x, axis=-1, keepdims=True) + EPS)


def _shift_rows(x, s, row):
    n = x.shape[0]
    if s == 0:
        return x
    y = pltpu.roll(x, (-s) % n, 0)
    ok = (row + s >= 0) & (row + s < n)
    return jnp.where(ok, y, 0.0)


def _dw_conv(x, w, row):
    y = _shift_rows(x, -2, row) * w[0:1, :]
    y = y + _shift_rows(x, -1, row) * w[1:2, :]
    y = y + x * w[2:3, :]
    y = y + _shift_rows(x, 1, row) * w[3:4, :]
    return y


def _mod_kernel(c_ref, w_ref, b_ref, o_ref):
    c = c_ref[...]
    o_ref[...] = jnp.dot(_silu(c), w_ref[...], precision=lax.Precision.HIGHEST,
                         preferred_element_type=F32) + b_ref[...]


def _modulation(cond, w_mod, b_mod):
    n_rows = cond.shape[0]
    tn = 1536
    n_out = N_MOD * D_MODEL
    return pl.pallas_call(
        _mod_kernel,
        grid=(DEPTH, n_out // tn),
        in_specs=[pl.BlockSpec((n_rows, D_MODEL), lambda l, j: (0, 0)),
                  pl.BlockSpec((None, D_MODEL, tn), lambda l, j: (l, 0, j)),
                  pl.BlockSpec((None, 1, tn), lambda l, j: (l, 0, j))],
        out_specs=pl.BlockSpec((None, n_rows, tn), lambda l, j: (l, 0, j)),
        out_shape=jax.ShapeDtypeStruct((DEPTH, n_rows, n_out), F32),
        compiler_params=_cparams(("parallel", "parallel")),
        name="modulation",
    )(cond, w_mod, b_mod.reshape(DEPTH, 1, n_out))


def _in_proj_kernel(x_ref, mod_ref, g_ref, w_ref, o_ref, h_scr):
    @pl.when(pl.program_id(1) == 0)
    def _():
        y = _rms(x_ref[...]) * g_ref[...]
        h_scr[...] = (y * (1.0 + mod_ref[1:2, :]) + mod_ref[0:1, :]).astype(BF16)

    o_ref[...] = jnp.dot(h_scr[...], w_ref[...], preferred_element_type=F32)


def _in_proj(x, mod, norm_g, w, mod_row, tm):
    t = x.shape[0]
    tn = 1920
    return pl.pallas_call(
        _in_proj_kernel,
        grid=(t // tm, N_PROJ // tn),
        in_specs=[pl.BlockSpec((tm, D_MODEL), lambda i, j: (i, 0)),
                  pl.BlockSpec((None, N_MOD, D_MODEL), lambda i, j: (mod_row(i), 0, 0)),
                  pl.BlockSpec((1, D_MODEL), lambda i, j: (0, 0)),
                  pl.BlockSpec((D_MODEL, tn), lambda i, j: (0, j))],
        out_specs=pl.BlockSpec((tm, tn), lambda i, j: (i, j)),
        out_shape=jax.ShapeDtypeStruct((t, N_PROJ), F32),
        scratch_shapes=[pltpu.VMEM((tm, D_MODEL), BF16)],
        compiler_params=_cparams(("parallel", "arbitrary")),
        name="in_proj",
    )(x, mod, norm_g, w)


def _ret_kernel(*refs, seq_len, rope, hg):
    if rope:
        (q_ref, k_ref, v_ref, g_ref, gam_ref, ng_ref, s0_ref, cs_ref, sn_ref,
         o_ref, so_ref, of_scr, ob_scr) = refs
    else:
        (q_ref, k_ref, v_ref, g_ref, gam_ref, ng_ref, s0_ref,
         o_ref, so_ref, of_scr, ob_scr) = refs
    c = RET_CHUNK
    n_chunks = seq_len // c
    r = lax.broadcasted_iota(jnp.int32, (c, HEAD_DIM), 0).astype(F32)
    ci = lax.broadcasted_iota(jnp.int32, (c, c), 0)
    si = lax.broadcasted_iota(jnp.int32, (c, c), 1)
    dmat = (ci - si).astype(F32)
    scale = HEAD_DIM ** -0.5

    chains = [(hh, d) for hh in range(hg) for d in range(2)]
    idx = range(len(chains))
    dec, qsc, ksc, gch = [], [], [], []
    for hh, d in chains:
        lg = -_softplus(-gam_ref[hh, d:d + 1, :])
        if d == 0:
            dec.append(jnp.where(dmat >= 0, jnp.exp(lg * jnp.maximum(dmat, 0.0)), 0.0))
            qsc.append(jnp.exp(lg * (r + 1.0)))
            ksc.append(jnp.exp(lg * (c - 1.0 - r)))
        else:
            dec.append(jnp.where(dmat <= 0, jnp.exp(lg * jnp.maximum(-dmat, 0.0)), 0.0))
            qsc.append(jnp.exp(lg * (c - r)))
            ksc.append(jnp.exp(lg * r))
        gch.append(jnp.exp(lg * c))

    s = [s0_ref[d, hh] for hh, d in chains]
    for stp in range(n_chunks):
        sl = [pl.ds((stp if d == 0 else n_chunks - 1 - stp) * c, c) for _, d in chains]
        cs = [slice(hh * HEAD_DIM, (hh + 1) * HEAD_DIM) for hh, _ in chains]
        q = [q_ref[sl[i], cs[i]] for i in idx]
        k = [k_ref[sl[i], cs[i]] * scale for i in idx]
        v = [v_ref[sl[i], cs[i]] for i in idx]
        if rope:
            cos = [cs_ref[sl[i], :] for i in idx]
            sin = [sn_ref[sl[i], :] for i in idx]
            q = [q[i] * cos[i] + pltpu.roll(q[i], HEAD_DIM // 2, 1) * sin[i] for i in idx]
            k = [k[i] * cos[i] + pltpu.roll(k[i], HEAD_DIM // 2, 1) * sin[i] for i in idx]
        sc = [_mm_nt(q[i], k[i]) for i in idx]
        qs = [_mm(q[i] * qsc[i], s[i]) for i in idx]
        kv = [_mm_tn(k[i] * ksc[i], v[i]) for i in idx]
        oi = [_mm(sc[i] * dec[i], v[i]) for i in idx]
        s = [s[i] * gch[i] + kv[i] for i in idx]
        for i in idx:
            if chains[i][1] == 0:
                of_scr[sl[i], cs[i]] = oi[i] + qs[i]
            else:
                ob_scr[sl[i], cs[i]] = oi[i] + qs[i]
    for i in idx:
        so_ref[chains[i][1], chains[i][0]] = s[i]
    tot = of_scr[...] + ob_scr[...]
    g = g_ref[...]
    for hh in range(hg):
        cs1 = slice(hh * HEAD_DIM, (hh + 1) * HEAD_DIM)
        o_ref[:, cs1] = _rms(tot[:, cs1]) * ng_ref[hh] * _silu(g[:, cs1])


def _retention(proj, gam, norm_g, s0, layer, rope_tabs, n_seq, seq_len, row_off, hg):
    rope = rope_tabs is not None
    wide = hg * HEAD_DIM

    def col(c0):
        return pl.BlockSpec((seq_len, wide), lambda b, g: (row_off + b, c0 // hg + g))

    in_specs = [col(COL_RQ), col(COL_RK), col(COL_RV), col(COL_RG),
                pl.BlockSpec((hg, 2, HEAD_DIM), lambda b, g: (g, 0, 0)),
                pl.BlockSpec((hg, 1, HEAD_DIM), lambda b, g: (g, 0, 0)),
                _state_spec(s0, layer, hg)]
    args = [proj, proj, proj, proj, gam, norm_g, s0]
    if rope:
        tab = pl.BlockSpec((seq_len, HEAD_DIM), lambda b, g: (0, 0))
        in_specs += [tab, tab]
        args += list(rope_tabs)
    return pl.pallas_call(
        functools.partial(_ret_kernel, seq_len=seq_len, rope=rope, hg=hg),
        grid=(n_seq, N_HEADS // hg),
        in_specs=in_specs,
        out_specs=[pl.BlockSpec((seq_len, wide), lambda b, g: (b, g)),
                   pl.BlockSpec((None, 2, hg, HEAD_DIM, HEAD_DIM), lambda b, g: (b, 0, g, 0, 0))],
        out_shape=[jax.ShapeDtypeStruct((n_seq * seq_len, D_MODEL), F32),
                   jax.ShapeDtypeStruct((n_seq, 2, N_HEADS, HEAD_DIM, HEAD_DIM), F32)],
        scratch_shapes=[pltpu.VMEM((seq_len, wide), F32), pltpu.VMEM((seq_len, wide), F32)],
        compiler_params=_cparams(("parallel", "parallel")),
        name="retention_rope" if rope else "retention",
    )(*args)


def _lru_kernel(x_ref, gate_ref, cw_ref, cb_ref, gw_ref, gb_ref, lam_ref, s0_ref, o_ref, so_ref, *, seq_len):
    n = seq_len
    row = lax.broadcasted_iota(jnp.int32, (n, HEAD_DIM), 0)
    xc = _dw_conv(x_ref[...], cw_ref[...], row) + cb_ref[...]
    lam = lam_ref[...]
    hs = []
    for d in range(2):
        r_gate = jax.nn.sigmoid(_mm(xc, gw_ref[d, 0]) + gb_ref[d, 0:1, :])
        i_gate = jax.nn.sigmoid(_mm(xc, gw_ref[d, 1]) + gb_ref[d, 1:2, :])
        log_a = -LRU_C * r_gate * _softplus(-lam[d:d + 1, :])
        a = jnp.exp(log_a)
        u = jnp.sqrt(-jnp.tanh(log_a) * (1.0 + a * a)) * i_gate * xc
        step = 1
        while step < n:
            if d == 0:
                ok = row >= step
                sh = step
            else:
                ok = row < n - step
                sh = n - step
            a_sh = jnp.where(ok, pltpu.roll(a, sh, 0), 1.0)
            u_sh = jnp.where(ok, pltpu.roll(u, sh, 0), 0.0)
            u = a * u_sh + u
            a = a * a_sh
            step *= 2
        h = u + a * s0_ref[d:d + 1, :]
        hs.append(h)
        so_ref[d:d + 1, :] = h[n - 1:n, :] if d == 0 else h[0:1, :]
    o_ref[...] = (hs[0] + hs[1]) * _gelu(gate_ref[...])


def _rglru(proj, conv_w, conv_b, gate_w, gate_b, lam, s0, s0_map, n_seq, seq_len, row_off):
    return pl.pallas_call(
        functools.partial(_lru_kernel, seq_len=seq_len),
        grid=(n_seq, N_HEADS),
        in_specs=[pl.BlockSpec((seq_len, HEAD_DIM), lambda b, n: (row_off + b, COL_LX + n)),
                  pl.BlockSpec((seq_len, HEAD_DIM), lambda b, n: (row_off + b, COL_LG + n)),
                  pl.BlockSpec((4, HEAD_DIM), lambda b, n: (0, n)),
                  pl.BlockSpec((1, HEAD_DIM), lambda b, n: (0, n)),
                  pl.BlockSpec((2, 2, None, HEAD_DIM, HEAD_DIM), lambda b, n: (0, 0, n, 0, 0)),
                  pl.BlockSpec((2, 2, HEAD_DIM), lambda b, n: (0, 0, n)),
                  pl.BlockSpec((2, HEAD_DIM), lambda b, n: (0, n)),
                  s0_map],
        out_specs=[pl.BlockSpec((seq_len, HEAD_DIM), lambda b, n: (b, n)),
                   pl.BlockSpec((None, 2, HEAD_DIM), lambda b, n: (b, 0, n))],
        out_shape=[jax.ShapeDtypeStruct((n_seq * seq_len, D_MODEL), F32),
                   jax.ShapeDtypeStruct((n_seq, 2, D_MODEL), F32)],
        compiler_params=_cparams(("parallel", "parallel")),
        name="rglru",
    )(proj, proj, conv_w, conv_b, gate_w, gate_b, lam, s0)


def _dn_kernel(q_ref, k_ref, v_ref, z_ref, sm_ref, cwq_ref, cwk_ref, cwv_ref, par_ref, ng_ref, s0_ref,
               o_ref, so_ref, q_scr, k_scr, v_scr, c_scr, b_scr, of_scr, ob_scr, *, seq_len, hg):
    n = seq_len
    cc = DN_CHUNK
    sc = DN_SUPER
    n_super = n // sc
    head0 = pl.program_id(1) * hg
    roww = lax.broadcasted_iota(jnp.int32, (n, hg * HEAD_DIM), 0)
    row = lax.broadcasted_iota(jnp.int32, (n, HEAD_DIM), 0)
    lane = lax.broadcasted_iota(jnp.int32, (n, HEAD_DIM), 1)
    pos = row & (cc - 1)

    xq = _silu(_dw_conv(q_ref[...], cwq_ref[...], roww))
    xk = _silu(_dw_conv(k_ref[...], cwk_ref[...], roww))
    v_scr[...] = _silu(_dw_conv(v_ref[...], cwv_ref[...], roww))
    small = sm_ref[...]
    par = par_ref[...]
    g_all = -jnp.exp(par[0:1, :]) * _softplus(small + par[1:2, :])
    beta_all = jax.nn.sigmoid(small)
    cum_all = [g_all, g_all]
    step = 1
    while step < cc:
        cum_all[0] = cum_all[0] + jnp.where(pos >= step, pltpu.roll(cum_all[0], step, 0), 0.0)
        cum_all[1] = cum_all[1] + jnp.where(pos < cc - step, pltpu.roll(cum_all[1], n - step, 0), 0.0)
        step *= 2
    for hh in range(hg):
        cs = slice(hh * HEAD_DIM, (hh + 1) * HEAD_DIM)
        xqh, xkh = xq[:, cs], xk[:, cs]
        q_scr[:, cs] = xqh * lax.rsqrt(jnp.sum(xqh * xqh, axis=-1, keepdims=True) + EPS) * (HEAD_DIM ** -0.5)
        k_scr[:, cs] = xkh * lax.rsqrt(jnp.sum(xkh * xkh, axis=-1, keepdims=True) + EPS)
        head = head0 + hh
        for d in range(2):
            a_lane = lane == 2 * N_HEADS * d + head
            b_lane = lane == 2 * N_HEADS * d + N_HEADS + head
            cum = jnp.sum(jnp.where(a_lane, cum_all[d], 0.0), axis=-1, keepdims=True)
            beta = jnp.sum(jnp.where(b_lane, beta_all, 0.0), axis=-1, keepdims=True)
            c_scr[d, hh] = jnp.broadcast_to(cum, (n, HEAD_DIM))
            b_scr[d, hh] = jnp.broadcast_to(beta, (n, HEAD_DIM))

    ri = lax.broadcasted_iota(jnp.int32, (sc, sc), 0)
    cj = lax.broadcasted_iota(jnp.int32, (sc, sc), 1)
    sh = cc.bit_length() - 1
    same = (ri >> sh) == (cj >> sh)
    incl = (same & (ri >= cj), same & (ri <= cj))
    strict = (same & (ri > cj), same & (ri < cj))
    eye = jnp.where(ri == cj, 1.0, 0.0)
    level = [(ri >> 3) == (cj >> 3)]
    for b in range(4, sh + 1):
        level.append(((ri >> b) == (cj >> b)) & ((ri >> (b - 1)) != (cj >> (b - 1))))

    n_ch = sc // cc

    def super_chunks(chains):
        idx = range(len(chains))
        dd = [c[0] for c in chains]
        sl = [pl.ds(c[2], sc) for c in chains]
        cs = [slice(c[1] * HEAD_DIM, (c[1] + 1) * HEAD_DIM) for c in chains]
        q = [q_scr[sl[c], cs[c]] for c in idx]
        k = [k_scr[sl[c], cs[c]] for c in idx]
        v = [v_scr[sl[c], cs[c]] for c in idx]
        cum = [c_scr[dd[c], chains[c][1], sl[c], :] for c in idx]
        beta = [b_scr[dd[c], chains[c][1], sl[c], :] for c in idx]
        kk = [_mm_nt(k[c], k[c]) for c in idx]
        qk = [_mm_nt(q[c], k[c]) for c in idx]
        decay, x, attn = [], [], []
        for c in idx:
            cb = jnp.concatenate([cum[c], cum[c]], axis=1)
            diff = cb - cb.T
            dec = jnp.where(incl[dd[c]], jnp.exp(jnp.where(incl[dd[c]], diff, 0.0)), 0.0)
            bb = jnp.concatenate([beta[c], beta[c]], axis=1)
            x.append(-jnp.where(strict[dd[c]], kk[c] * bb * dec, 0.0))
            attn.append(qk[c] * dec)
        xp = [jnp.where(level[0], x[c], 0.0) for c in idx]
        p = [eye + xp[c] for c in idx]
        for _ in range(2):
            xp = [_mm(xp[c], xp[c]) for c in idx]
            pm = [_mm(p[c], xp[c]) for c in idx]
            p = [p[c] + pm[c] for c in idx]
        for lv in range(1, len(level)):
            t1 = [_mm(jnp.where(level[lv], x[c], 0.0), p[c]) for c in idx]
            t2 = [_mm(p[c], t1[c]) for c in idx]
            p = [p[c] + t2[c] for c in idx]
        rhs = [jnp.concatenate([v[c] * beta[c], k[c] * beta[c] * jnp.exp(cum[c])], axis=1) for c in idx]
        sol = [_mm(p[c], rhs[c]) for c in idx]
        qd = [q[c] * jnp.exp(cum[c]) for c in idx]
        s = [so_ref[dd[c], chains[c][1]] for c in idx]
        v_new = [[None] * n_ch for _ in idx]
        o_inter = [[None] * n_ch for _ in idx]
        for stp in range(n_ch):
            ch = [stp if dd[c] == 0 else n_ch - 1 - stp for c in idx]
            rs = [slice(ch[c] * cc, (ch[c] + 1) * cc) for c in idx]
            ws = [_mm(sol[c][rs[c], HEAD_DIM:], s[c]) for c in idx]
            for c in idx:
                o_inter[c][ch[c]] = _mm(qd[c][rs[c], :], s[c])
            kt = []
            for c in idx:
                cum_c = cum[c][rs[c], :]
                tot = cum_c[cc - 1:cc, :] if dd[c] == 0 else cum_c[0:1, :]
                v_new[c][ch[c]] = sol[c][rs[c], :HEAD_DIM] - ws[c]
                kt.append((k[c][rs[c], :] * jnp.exp(tot - cum_c), jnp.exp(tot)))
            upd = [_mm_tn(kt[c][0], v_new[c][ch[c]]) for c in idx]
            s = [s[c] * kt[c][1] + upd[c] for c in idx]
        oi = [_mm(attn[c], jnp.concatenate(v_new[c], axis=0)) for c in idx]
        for c in idx:
            o = jnp.concatenate(o_inter[c], axis=0) + oi[c]
            if dd[c] == 0:
                of_scr[sl[c], cs[c]] = o
            else:
                ob_scr[sl[c], cs[c]] = o
            so_ref[dd[c], chains[c][1]] = s[c]

    so_ref[...] = s0_ref[...]

    def body(i, carry):
        fwd = pl.multiple_of(i * sc, sc)
        bwd = pl.multiple_of((n_super - 1 - i) * sc, sc)
        super_chunks([(d, hh, fwd if d == 0 else bwd) for hh in range(hg) for d in range(2)])
        return carry

    lax.fori_loop(0, n_super, body, 0)
    tot = of_scr[...] + ob_scr[...]
    z = z_ref[...]
    for hh in range(hg):
        cs = slice(hh * HEAD_DIM, (hh + 1) * HEAD_DIM)
        o_ref[:, cs] = _rms(tot[:, cs]) * ng_ref[hh] * _silu(z[:, cs])


def _state_spec(s0, layer, hg):
    if layer is None:
        return pl.BlockSpec((None, 2, hg, HEAD_DIM, HEAD_DIM), lambda b, g: (b, 0, g, 0, 0))
    return pl.BlockSpec((None, None, 2, hg, HEAD_DIM, HEAD_DIM), lambda b, g: (b, layer, 0, g, 0, 0))


def _deltanet(proj, conv_w, par, norm_g, s0, layer, n_seq, seq_len, row_off, hg):
    wide = hg * HEAD_DIM

    def col(c0):
        return pl.BlockSpec((seq_len, wide), lambda b, g: (row_off + b, c0 // hg + g))

    def cw(c0):
        return pl.BlockSpec((4, wide), lambda b, g: (0, c0 // hg + g))

    vm = functools.partial(pltpu.VMEM, dtype=F32)
    return pl.pallas_call(
        functools.partial(_dn_kernel, seq_len=seq_len, hg=hg),
        grid=(n_seq, N_HEADS // hg),
        in_specs=[col(COL_DQ), col(COL_DK), col(COL_DV), col(COL_DZ),
                  pl.BlockSpec((seq_len, HEAD_DIM), lambda b, g: (row_off + b, COL_SMALL)),
                  cw(0), cw(N_HEADS), cw(2 * N_HEADS),
                  pl.BlockSpec((2, HEAD_DIM), lambda b, g: (0, 0)),
                  pl.BlockSpec((hg, 1, HEAD_DIM), lambda b, g: (g, 0, 0)),
                  _state_spec(s0, layer, hg)],
        out_specs=[pl.BlockSpec((seq_len, wide), lambda b, g: (b, g)),
                   pl.BlockSpec((None, 2, hg, HEAD_DIM, HEAD_DIM), lambda b, g: (b, 0, g, 0, 0))],
        out_shape=[jax.ShapeDtypeStruct((n_seq * seq_len, D_MODEL), F32),
                   jax.ShapeDtypeStruct((n_seq, 2, N_HEADS, HEAD_DIM, HEAD_DIM), F32)],
        scratch_shapes=[vm((seq_len, wide)), vm((seq_len, wide)), vm((seq_len, wide)),
                        vm((2, hg, seq_len, HEAD_DIM)), vm((2, hg, seq_len, HEAD_DIM)),
                        vm((seq_len, wide)), vm((seq_len, wide))],
        compiler_params=_cparams(("parallel", "parallel"),
                                 VMEM_LIMIT_WIDE if seq_len * hg > 2048 else VMEM_LIMIT),
        name="deltanet",
    )(proj, proj, proj, proj, proj, conv_w, conv_w, conv_w, par, norm_g, s0)


def _merge_kernel(retc_ref, retl_ref, lruc_ref, lrul_ref, dnc_ref, dnl_ref, g0_ref, g1_ref, g2_ref, x_ref, mod_ref,
                  ng_ref, wbr_ref, wout_ref, xo_ref, h_ref, *, ctx_blocks):
    is_ctx = pl.program_id(0) < ctx_blocks
    ret = jnp.where(is_ctx, retc_ref[...], retl_ref[...])
    lru = jnp.where(is_ctx, lruc_ref[...], lrul_ref[...])
    dn = jnp.where(is_ctx, dnc_ref[...], dnl_ref[...])
    merged = jax.nn.sigmoid(g0_ref[...]) * _mm(ret, wbr_ref[0])
    merged = merged + jax.nn.sigmoid(g1_ref[...]) * _mm(lru, wbr_ref[1])
    merged = merged + jax.nn.sigmoid(g2_ref[...]) * _mm(dn, wbr_ref[2])
    x = x_ref[...] + mod_ref[2:3, :] * _mm(merged, wout_ref[...])
    xo_ref[...] = x
    h_ref[...] = (_rms(x) * ng_ref[...] * (1.0 + mod_ref[4:5, :]) + mod_ref[3:4, :]).T.astype(BF16)


def _merge(branches, proj, x, mod, norm_g, w_br, w_out, mod_row, tm):
    t = x.shape[0]
    ctx_blocks = branches[0][0].shape[0] // tm
    row = pl.BlockSpec((tm, D_MODEL), lambda i: (i, 0))
    row_c = pl.BlockSpec((tm, D_MODEL), lambda i: (jnp.minimum(i, ctx_blocks - 1), 0))
    row_l = pl.BlockSpec((tm, D_MODEL), lambda i: (jnp.maximum(i - ctx_blocks, 0), 0))
    col0 = COL_BGATE * HEAD_DIM // D_MODEL

    def gate(k):
        return pl.BlockSpec((tm, D_MODEL), lambda i: (i, col0 + k))

    return pl.pallas_call(
        functools.partial(_merge_kernel, ctx_blocks=ctx_blocks),
        grid=(t // tm,),
        in_specs=[row_c, row_l, row_c, row_l, row_c, row_l, gate(0), gate(1), gate(2), row,
                  pl.BlockSpec((None, N_MOD, D_MODEL), lambda i: (mod_row(i), 0, 0)),
                  pl.BlockSpec((1, D_MODEL), lambda i: (0, 0)),
                  pl.BlockSpec((3, D_MODEL, D_MODEL), lambda i: (0, 0, 0)),
                  pl.BlockSpec((D_MODEL, D_MODEL), lambda i: (0, 0))],
        out_specs=[row, pl.BlockSpec((D_MODEL, tm), lambda i: (0, i))],
        out_shape=[jax.ShapeDtypeStruct((t, D_MODEL), F32), jax.ShapeDtypeStruct((D_MODEL, t), BF16)],
        compiler_params=_cparams(("parallel",)),
        name="merge",
    )(*branches[0], *branches[1], *branches[2], proj, proj, proj, x, mod, norm_g, w_br, w_out)


def _router_kernel(h_ref, wq_ref, keys_ref, thr_ref, e0_ref, e1_ref, q_scr, top_scr, cand_scr):
    tb = h_ref.shape[1]
    q_scr[...] = jnp.dot(wq_ref[...], h_ref[...], preferred_element_type=F32)

    top_scr[...] = jnp.full(top_scr.shape, NEG_BIG, F32)
    sub_row = lax.broadcasted_iota(jnp.int32, (SUBLANES, tb), 0)

    def top_values(x, dst):
        for kk in range(TOP_N):
            m = jnp.max(x, axis=0, keepdims=True)
            top_scr[dst, kk:kk + 1, :] = m
            x = jnp.where(x >= m, NEG_BIG, x)

    def body(hd, carry):
        base = pl.multiple_of(hd * 2 * N_KEYS, 2 * N_KEYS)
        s0 = _mm(keys_ref[0], q_scr[pl.ds(base, N_KEYS), :])
        s1 = _mm(keys_ref[1], q_scr[pl.ds(base + N_KEYS, N_KEYS), :])
        top_values(s0, 0)
        top_values(s1, 1)
        cand_scr[0:TOP_PAD, :] = top_scr[0, 0:1, :] + top_scr[1]
        a1 = top_scr[1, 0:SUBLANES, :]
        for p in range(1, N_MULTI):
            keep = sub_row < TOP_N // (p + 1)
            cand_scr[TOP_PAD + (p - 1) * SUBLANES:TOP_PAD + p * SUBLANES, :] = jnp.where(
                keep, top_scr[0, p:p + 1, :] + a1, NEG_BIG)
        cand_scr[CAND_ROWS - SUBLANES:CAND_ROWS, :] = top_scr[0, N_MULTI:TOP_N, :] + top_scr[1, 0:1, :]
        x = cand_scr[...]
        m0 = jnp.max(x, axis=0, keepdims=True)
        z = jnp.zeros_like(m0)
        m = m0
        for kk in range(PEER_TOPK):
            if kk > 0:
                x = jnp.where(x >= m, NEG_BIG, x)
                m = jnp.max(x, axis=0, keepdims=True)
            z = z + jnp.exp(m - m0)
        m_next = jnp.max(jnp.where(x >= m, NEG_BIG, x), axis=0, keepdims=True)
        tau = 0.5 * (m + m_next)
        max1 = top_scr[1, 0:1, :]
        thr = jnp.exp(tau - s0 - max1) / z
        e0 = jnp.exp(s0 - top_scr[0, 0:1, :])
        e1 = jnp.exp(s1 - max1) / z
        for c in range(tb // HEAD_DIM):
            ls = slice(c * HEAD_DIM, (c + 1) * HEAD_DIM)
            thr_ref[hd, c] = thr[:, ls]
            e0_ref[hd, c] = e0[:, ls]
            e1_ref[hd, c] = e1[:, ls]
        return carry

    lax.fori_loop(0, N_HEADS, body, 0)


def _router(h2, wq_t, keys, tb):
    t = h2.shape[1]
    out = pl.BlockSpec((N_HEADS, tb // HEAD_DIM, N_KEYS, HEAD_DIM), lambda i: (0, i, 0, 0))
    shp = jax.ShapeDtypeStruct((N_HEADS, t // HEAD_DIM, N_KEYS, HEAD_DIM), F32)
    return pl.pallas_call(
        _router_kernel,
        grid=(t // tb,),
        in_specs=[pl.BlockSpec((D_MODEL, tb), lambda i: (0, i)),
                  pl.BlockSpec((2 * N_KEYS * N_HEADS, D_MODEL), lambda i: (0, 0)),
                  pl.BlockSpec((2, N_KEYS, N_KEYS), lambda i: (0, 0, 0))],
        out_specs=[out, out, out],
        out_shape=[shp, shp, shp],
        scratch_shapes=[pltpu.VMEM((2 * N_KEYS * N_HEADS, tb), F32),
                        pltpu.VMEM((2, TOP_PAD, tb), F32),
                        pltpu.VMEM((CAND_ROWS, tb), F32)],
        compiler_params=_cparams(("parallel",)),
        name="peer_router",
    )(h2, wq_t, keys)


def _expert_kernel(h_ref, u0_ref, un_ref, vt_ref, thr_ref, e0_ref, e1_ref, x_ref, mod_ref, o_ref,
                   acc_scr, g_scr, act_a, act_b, *, tile_e):
    act_scr = (act_a, act_b)
    j = pl.program_id(1)
    tb = h_ref.shape[1]
    n_sub = tile_e // N_KEYS

    n_lane = tb // HEAD_DIM

    def activations(u_ref, dst):
        act = jnp.dot(u_ref[...], h_ref[...], preferred_element_type=F32)
        for c in range(n_lane):
            dst[c] = act[:, c * HEAD_DIM:(c + 1) * HEAD_DIM]

    @pl.when(j == 0)
    def _():
        acc_scr[...] = jnp.zeros_like(acc_scr)
        activations(u0_ref, act_scr[0])

    i0 = pl.multiple_of(j * n_sub, n_sub)
    n_part = tile_e // EXPERT_PART
    keys_per_part = EXPERT_PART // N_KEYS

    def step(cur, nxt):
        activations(un_ref, act_scr[nxt])
        acc = acc_scr[...]
        for part in range(n_part):
            for c in range(n_lane):
                thr = [thr_ref[hd, c, pl.ds(i0, n_sub), :] for hd in range(N_HEADS)]
                e0 = [e0_ref[hd, c, pl.ds(i0, n_sub), :] for hd in range(N_HEADS)]
                for r in range(keys_per_part):
                    ii = part * keys_per_part + r
                    rs = slice(ii * N_KEYS, (ii + 1) * N_KEYS)
                    wd = jnp.zeros((N_KEYS, HEAD_DIM), F32)
                    for hd in range(N_HEADS):
                        e1 = e1_ref[hd, c]
                        wd = wd + jnp.where(e1 >= thr[hd][ii:ii + 1, :], e1 * e0[hd][ii:ii + 1, :], 0.0)
                    g_scr[c, rs, :] = (_gelu(act_scr[cur][c, rs, :]) * wd).astype(BF16)
            rows = slice(part * EXPERT_PART, (part + 1) * EXPERT_PART)
            g_part = jnp.concatenate([g_scr[c, rows, :] for c in range(n_lane)], axis=1)
            acc = acc + jnp.dot(vt_ref[:, rows], g_part, preferred_element_type=F32)
        acc_scr[...] = acc

    slot = lax.rem(j, 2)

    @pl.when(slot == 0)
    def _():
        step(0, 1)

    @pl.when(slot == 1)
    def _():
        step(1, 0)

    @pl.when(j == pl.num_programs(1) - 1)
    def _():
        o_ref[...] = x_ref[...] + mod_ref[5:6, :] * acc_scr[...].T


def _experts(h2t, u_tab, vt_tab, routing, x, mod, mod_row, tb, tile_e):
    t = h2t.shape[1]
    n_exp = u_tab.shape[0]
    n_lane = tb // HEAD_DIM
    rt = pl.BlockSpec((N_HEADS, n_lane, N_KEYS, HEAD_DIM), lambda i, j: (0, i, 0, 0))
    n_tiles = n_exp // tile_e
    return pl.pallas_call(
        functools.partial(_expert_kernel, tile_e=tile_e),
        grid=(t // tb, n_tiles),
        in_specs=[pl.BlockSpec((D_MODEL, tb), lambda i, j: (0, i)),
                  pl.BlockSpec((tile_e, D_MODEL), lambda i, j: (0, 0)),
                  pl.BlockSpec((tile_e, D_MODEL), lambda i, j: (jnp.minimum(j + 1, n_tiles - 1), 0)),
                  pl.BlockSpec((D_MODEL, tile_e), lambda i, j: (0, j)),
                  rt, rt, rt,
                  pl.BlockSpec((tb, D_MODEL), lambda i, j: (i, 0)),
                  pl.BlockSpec((None, N_MOD, D_MODEL), lambda i, j: (mod_row(i), 0, 0))],
        out_specs=pl.BlockSpec((tb, D_MODEL), lambda i, j: (i, 0)),
        out_shape=jax.ShapeDtypeStruct((t, D_MODEL), F32),
        scratch_shapes=[pltpu.VMEM((D_MODEL, tb), F32), pltpu.VMEM((n_lane, tile_e, HEAD_DIM), BF16),
                        pltpu.VMEM((n_lane, tile_e, HEAD_DIM), F32), pltpu.VMEM((n_lane, tile_e, HEAD_DIM), F32)],
        compiler_params=_cparams(("parallel", "arbitrary")),
        name="peer_experts",
    )(h2t, u_tab, u_tab, vt_tab, *routing, x, mod)


def _final_norm_kernel(x_ref, g_ref, o_ref):
    o_ref[...] = _rms(x_ref[...]) * g_ref[...]


def _final_norm(x, g, tm):
    t = x.shape[0]
    row = pl.BlockSpec((tm, D_MODEL), lambda i: (i, 0))
    return pl.pallas_call(
        _final_norm_kernel, grid=(t // tm,),
        in_specs=[row, pl.BlockSpec((1, D_MODEL), lambda i: (0, 0))],
        out_specs=row, out_shape=jax.ShapeDtypeStruct((t, D_MODEL), F32),
        compiler_params=_cparams(("parallel",)), name="final_norm",
    )(x, g)


def _rope_tables(seq_len):
    rows = seq_len // GRID_W
    r = jnp.repeat(jnp.arange(rows, dtype=F32), GRID_W)
    col = jnp.tile(jnp.arange(GRID_W, dtype=F32), rows)
    nf = HEAD_DIM // 4
    inv = ROPE_BASE ** (-jnp.arange(nf, dtype=F32) / nf)
    ang = jnp.concatenate([r[:, None] * inv, col[:, None] * inv], axis=-1)
    cos, sin = jnp.cos(ang), jnp.sin(ang)
    return jnp.concatenate([cos, cos], axis=-1), jnp.concatenate([-sin, sin], axis=-1)


def _lanes(a):
    return jnp.broadcast_to(jnp.moveaxis(a, -1, 0)[..., None], (a.shape[-1],) + a.shape[:-1] + (HEAD_DIM,))


def kernel(x_prompt, x_sample, c, state_ret, state_lru, state_dn, c_ctx, w_mod, b_mod, norm1_g, norm2_g, w_in, ret_gamma_logit, ret_norm_g, lru_conv_w, lru_conv_b, lru_gate_w, lru_gate_b, lru_lambda, dn_conv_w, dn_a_log, dn_dt_bias, dn_norm_g, w_br, w_out, peer_w_q, peer_sub_keys, peer_u, peer_v, final_norm_g):
    n_ctx, l_ctx, _ = x_prompt.shape
    n_lat, l_lat, _ = x_sample.shape
    t_ctx, t_lat = n_ctx * l_ctx, n_lat * l_lat
    assert t_ctx % l_lat == 0
    tb = 512
    assert l_lat % tb == 0 and t_ctx % tb == 0
    ctx_blocks, per_seq = t_ctx // tb, l_lat // tb

    def mod_row(i):
        return jnp.where(i < ctx_blocks, 0, 1 + (i - ctx_blocks) // per_seq)

    tm = 256
    ctx_blocks_m, per_seq_m = t_ctx // tm, l_lat // tm

    def mod_row_m(i):
        return jnp.where(i < ctx_blocks_m, 0, 1 + (i - ctx_blocks_m) // per_seq_m)

    x = jnp.concatenate([x_prompt.reshape(t_ctx, D_MODEL), x_sample.reshape(t_lat, D_MODEL)], axis=0)
    n_cond = 16
    cond = jnp.zeros((n_cond, D_MODEL), F32).at[0].set(c_ctx).at[1:1 + n_lat].set(c)
    mods = _modulation(cond, w_mod, b_mod).reshape(DEPTH, n_cond, N_MOD, D_MODEL)

    w_in_r = jnp.concatenate(
        [w_in[:, :, :N_MAIN], w_in[:, :, N_MAIN + N_SMALL:], w_in[:, :, N_MAIN:N_MAIN + N_SMALL],
         jnp.zeros((DEPTH, D_MODEL, HEAD_DIM - N_SMALL), F32)], axis=-1).astype(BF16)
    w_br_b, w_out_b = w_br.astype(BF16), w_out.astype(BF16)
    wq_t = jnp.swapaxes(peer_w_q, 1, 2).astype(BF16)
    keys_b = peer_sub_keys.astype(BF16)
    u_b = peer_u.astype(BF16)
    vt_b = jnp.swapaxes(peer_v, 1, 2).astype(BF16)
    gam = _lanes(ret_gamma_logit)
    lane_pad = jnp.zeros((DEPTH, 2, N_HEADS), F32)
    dn_ab = jnp.stack([dn_a_log, dn_dt_bias], axis=1)
    dn_par = jnp.concatenate([dn_ab[:, :, 0], lane_pad, dn_ab[:, :, 1], lane_pad,
                              jnp.zeros((DEPTH, 2, HEAD_DIM - N_SMALL), F32)], axis=-1)
    rope_tabs = _rope_tables(l_lat)
    zero_ret = jnp.zeros((n_ctx, 2, N_HEADS, HEAD_DIM, HEAD_DIM), F32)
    zero_lru = jnp.zeros((n_ctx, 2, D_MODEL), F32)

    vec_zero = pl.BlockSpec((None, 2, HEAD_DIM), lambda b, n: (b, 0, n))
    row_off_lat = t_ctx // l_lat

    ret_states, lru_states, dn_states = [], [], []
    for l in range(DEPTH):
        vec_lat = pl.BlockSpec((None, None, 2, HEAD_DIM), lambda b, n, l=l: (b, l, 0, n))
        proj = _in_proj(x, mods[l], norm1_g[l][None], w_in_r[l],
                        lambda i: jnp.where(i < row_off_lat, 0, 1 + i - row_off_lat), l_lat)

        ng_ret = ret_norm_g[l][:, None, :]
        o_ret_c, s_ret = _retention(proj, gam[:, l], ng_ret, zero_ret, None, None, n_ctx, l_ctx, 0, 4)
        o_ret_l, _ = _retention(proj, gam[:, l], ng_ret, state_ret, l, rope_tabs, n_lat, l_lat, row_off_lat, 4)

        lru_args = (lru_conv_w[l], lru_conv_b[l][None], lru_gate_w[l], lru_gate_b[l], lru_lambda[l])
        o_lru_c, s_lru = _rglru(proj, *lru_args, zero_lru, vec_zero, n_ctx, l_ctx, 0)
        o_lru_l, _ = _rglru(proj, *lru_args, state_lru, vec_lat, n_lat, l_lat, row_off_lat)

        ng_dn = dn_norm_g[l][:, None, :]
        o_dn_c, s_dn = _deltanet(proj, dn_conv_w[l], dn_par[l], ng_dn, zero_ret, None, n_ctx, l_ctx, 0, 4)
        o_dn_l, _ = _deltanet(proj, dn_conv_w[l], dn_par[l], ng_dn, state_dn, l, n_lat, l_lat, row_off_lat, 4)

        branches = ((o_ret_c, o_ret_l), (o_lru_c, o_lru_l), (o_dn_c, o_dn_l))
        x, h2 = _merge(branches, proj, x, mods[l], norm2_g[l][None], w_br_b[l], w_out_b[l], mod_row_m, tm)
        routing = _router(h2, wq_t[l], keys_b[l], tb)
        x = _experts(h2, u_b[l], vt_b[l], routing, x, mods[l], mod_row, tb, 1024)

        ret_states.append(s_ret)
        lru_states.append(s_lru)
        dn_states.append(s_dn)

    y = _final_norm(x, final_norm_g[None], tb)
    y_prompt = y[:t_ctx].reshape(n_ctx, l_ctx, D_MODEL)
    y_sample = y[t_ctx:].reshape(n_lat, l_lat, D_MODEL)
    return (y_prompt, y_sample, jnp.stack(ret_states, axis=1), jnp.stack(lru_states, axis=1),
            jnp.stack(dn_states, axis=1))
```

```python
import functools
import math

import jax
import jax.numpy as jnp
from jax import lax
from jax.experimental import pallas as pl
from jax.experimental.pallas import tpu as pltpu

F32 = jnp.float32
BF16 = jnp.bfloat16

D_MODEL = 1024
DEPTH = 4
N_MOD = 6
EPS = 1e-6
GRID_W = 64
ROPE_BASE = 10000.0
N_HEADS = 8
HEAD_DIM = 128
RET_CHUNK = 128
DN_CHUNK = 64
DN_SUPER = 256
LRU_C = 8.0
N_KEYS = 128
PEER_TOPK = 16
N_EXPERTS = N_KEYS * N_KEYS
SUBLANES = 8
TOP_N = PEER_TOPK + 1
TOP_PAD = -(-TOP_N // SUBLANES) * SUBLANES
N_MULTI = TOP_N - SUBLANES
CAND_ROWS = TOP_PAD + N_MULTI * SUBLANES
assert TOP_N // 2 <= SUBLANES and TOP_N // (N_MULTI + 1) == 1
EXPERT_TILE = 1024
EXPERT_PART = 256
NEG_BIG = -3.0e38

COL_RQ, COL_RK, COL_RV, COL_RG = 0, 8, 16, 24
COL_LX, COL_LG = 32, 40
COL_DQ, COL_DK, COL_DV, COL_DZ = 48, 56, 64, 72
N_MAIN = 80 * 128
N_SMALL = 4 * N_HEADS
COL_BGATE = 80
COL_SMALL = 104
N_PROJ = 105 * 128

VMEM_LIMIT = 48 * 1024 * 1024
VMEM_LIMIT_WIDE = 58 * 1024 * 1024


def _cparams(sem, vmem_limit=VMEM_LIMIT):
    return pltpu.CompilerParams(dimension_semantics=sem, vmem_limit_bytes=vmem_limit)


def _mm(a, b):
    return jnp.dot(a.astype(BF16), b.astype(BF16), preferred_element_type=F32)


def _mm_nt(a, b):
    return lax.dot_general(a.astype(BF16), b.astype(BF16), (((1,), (1,)), ((), ())),
                           preferred_element_type=F32)


def _mm_tn(a, b):
    return lax.dot_general(a.astype(BF16), b.astype(BF16), (((0,), (0,)), ((), ())),
                           preferred_element_type=F32)


def _softplus(x):
    return jnp.maximum(x, 0.0) + jnp.log1p(jnp.exp(-jnp.abs(x)))


def _silu(x):
    return x * jax.nn.sigmoid(x)


def _gelu(x):
    return 0.5 * x * (1.0 + lax.erf(x * math.sqrt(0.5)))


def _rms(x):
    return x * lax.rsqrt(jnp.mean(x * x, axis=-1, keepdims=True) + EPS)


def _shift_rows(x, s, row):
    n = x.shape[0]
    if s == 0:
        return x
    y = pltpu.roll(x, (-s) % n, 0)
    ok = (row + s >= 0) & (row + s < n)
    return jnp.where(ok, y, 0.0)


def _dw_conv(x, w, row):
    y = _shift_rows(x, -2, row) * w[0:1, :]
    y = y + _shift_rows(x, -1, row) * w[1:2, :]
    y = y + x * w[2:3, :]
    y = y + _shift_rows(x, 1, row) * w[3:4, :]
    return y


def _mod_kernel(c_ref, w_ref, b_ref, o_ref):
    c = c_ref[...]
    o_ref[...] = jnp.dot(_silu(c), w_ref[...], precision=lax.Precision.HIGHEST,
                         preferred_element_type=F32) + b_ref[...]


def _modulation(cond, w_mod, b_mod):
    n_rows = cond.shape[0]
    tn = 1536
    n_out = N_MOD * D_MODEL
    return pl.pallas_call(
        _mod_kernel,
        grid=(DEPTH, n_out // tn),
        in_specs=[pl.BlockSpec((n_rows, D_MODEL), lambda l, j: (0, 0)),
                  pl.BlockSpec((None, D_MODEL, tn), lambda l, j: (l, 0, j)),
                  pl.BlockSpec((None, 1, tn), lambda l, j: (l, 0, j))],
        out_specs=pl.BlockSpec((None, n_rows, tn), lambda l, j: (l, 0, j)),
        out_shape=jax.ShapeDtypeStruct((DEPTH, n_rows, n_out), F32),
        compiler_params=_cparams(("parallel", "parallel")),
        name="modulation",
    )(cond, w_mod, b_mod.reshape(DEPTH, 1, n_out))


def _in_proj_kernel(x_ref, mod_ref, g_ref, w_ref, o_ref, h_scr):
    @pl.when(pl.program_id(1) == 0)
    def _():
        y = _rms(x_ref[...]) * g_ref[...]
        h_scr[...] = (y * (1.0 + mod_ref[1:2, :]) + mod_ref[0:1, :]).astype(BF16)

    o_ref[...] = jnp.dot(h_scr[...], w_ref[...], preferred_element_type=F32)


def _in_proj(x, mod, norm_g, w, mod_row, tm):
    t = x.shape[0]
    tn = 1920
    return pl.pallas_call(
        _in_proj_kernel,
        grid=(t // tm, N_PROJ // tn),
        in_specs=[pl.BlockSpec((tm, D_MODEL), lambda i, j: (i, 0)),
                  pl.BlockSpec((None, N_MOD, D_MODEL), lambda i, j: (mod_row(i), 0, 0)),
                  pl.BlockSpec((1, D_MODEL), lambda i, j: (0, 0)),
                  pl.BlockSpec((D_MODEL, tn), lambda i, j: (0, j))],
        out_specs=pl.BlockSpec((tm, tn), lambda i, j: (i, j)),
        out_shape=jax.ShapeDtypeStruct((t, N_PROJ), F32),
        scratch_shapes=[pltpu.VMEM((tm, D_MODEL), BF16)],
        compiler_params=_cparams(("parallel", "arbitrary")),
        name="in_proj",
    )(x, mod, norm_g, w)


def _ret_kernel(*refs, seq_len, rope, hg):
    if rope:
        (q_ref, k_ref, v_ref, g_ref, gam_ref, ng_ref, s0_ref, cs_ref, sn_ref,
         o_ref, so_ref, of_scr, ob_scr) = refs
    else:
        (q_ref, k_ref, v_ref, g_ref, gam_ref, ng_ref, s0_ref,
         o_ref, so_ref, of_scr, ob_scr) = refs
    c = RET_CHUNK
    n_chunks = seq_len // c
    r = lax.broadcasted_iota(jnp.int32, (c, HEAD_DIM), 0).astype(F32)
    ci = lax.broadcasted_iota(jnp.int32, (c, c), 0)
    si = lax.broadcasted_iota(jnp.int32, (c, c), 1)
    dmat = (ci - si).astype(F32)
    scale = HEAD_DIM ** -0.5

    chains = [(hh, d) for hh in range(hg) for d in range(2)]
    idx = range(len(chains))
    dec, qsc, ksc, gch = [], [], [], []
    for hh, d in chains:
        lg = -_softplus(-gam_ref[hh, d:d + 1, :])
        if d == 0:
            dec.append(jnp.where(dmat >= 0, jnp.exp(lg * jnp.maximum(dmat, 0.0)), 0.0))
            qsc.append(jnp.exp(lg * (r + 1.0)))
            ksc.append(jnp.exp(lg * (c - 1.0 - r)))
        else:
            dec.append(jnp.where(dmat <= 0, jnp.exp(lg * jnp.maximum(-dmat, 0.0)), 0.0))
            qsc.append(jnp.exp(lg * (c - r)))
            ksc.append(jnp.exp(lg * r))
        gch.append(jnp.exp(lg * c))

    s = [s0_ref[d, hh] for hh, d in chains]
    for stp in range(n_chunks):
        sl = [pl.ds((stp if d == 0 else n_chunks - 1 - stp) * c, c) for _, d in chains]
        cs = [slice(hh * HEAD_DIM, (hh + 1) * HEAD_DIM) for hh, _ in chains]
        q = [q_ref[sl[i], cs[i]] for i in idx]
        k = [k_ref[sl[i], cs[i]] * scale for i in idx]
        v = [v_ref[sl[i], cs[i]] for i in idx]
        if rope:
            cos = [cs_ref[sl[i], :] for i in idx]
            sin = [sn_ref[sl[i], :] for i in idx]
            q = [q[i] * cos[i] + pltpu.roll(q[i], HEAD_DIM // 2, 1) * sin[i] for i in idx]
            k = [k[i] * cos[i] + pltpu.roll(k[i], HEAD_DIM // 2, 1) * sin[i] for i in idx]
        sc = [_mm_nt(q[i], k[i]) for i in idx]
        qs = [_mm(q[i] * qsc[i], s[i]) for i in idx]
        kv = [_mm_tn(k[i] * ksc[i], v[i]) for i in idx]
        oi = [_mm(sc[i] * dec[i], v[i]) for i in idx]
        s = [s[i] * gch[i] + kv[i] for i in idx]
        for i in idx:
            if chains[i][1] == 0:
                of_scr[sl[i], cs[i]] = oi[i] + qs[i]
            else:
                ob_scr[sl[i], cs[i]] = oi[i] + qs[i]
    for i in idx:
        so_ref[chains[i][1], chains[i][0]] = s[i]
    tot = of_scr[...] + ob_scr[...]
    g = g_ref[...]
    for hh in range(hg):
        cs1 = slice(hh * HEAD_DIM, (hh + 1) * HEAD_DIM)
        o_ref[:, cs1] = _rms(tot[:, cs1]) * ng_ref[hh] * _silu(g[:, cs1])


def _retention(proj, gam, norm_g, s0, layer, rope_tabs, n_seq, seq_len, row_off, hg):
    rope = rope_tabs is not None
    wide = hg * HEAD_DIM

    def col(c0):
        return pl.BlockSpec((seq_len, wide), lambda b, g: (row_off + b, c0 // hg + g))

    in_specs = [col(COL_RQ), col(COL_RK), col(COL_RV), col(COL_RG),
                pl.BlockSpec((hg, 2, HEAD_DIM), lambda b, g: (g, 0, 0)),
                pl.BlockSpec((hg, 1, HEAD_DIM), lambda b, g: (g, 0, 0)),
                _state_spec(s0, layer, hg)]
    args = [proj, proj, proj, proj, gam, norm_g, s0]
    if rope:
        tab = pl.BlockSpec((seq_len, HEAD_DIM), lambda b, g: (0, 0))
        in_specs += [tab, tab]
        args += list(rope_tabs)
    return pl.pallas_call(
        functools.partial(_ret_kernel, seq_len=seq_len, rope=rope, hg=hg),
        grid=(n_seq, N_HEADS // hg),
        in_specs=in_specs,
        out_specs=[pl.BlockSpec((seq_len, wide), lambda b, g: (b, g)),
                   pl.BlockSpec((None, 2, hg, HEAD_DIM, HEAD_DIM), lambda b, g: (b, 0, g, 0, 0))],
        out_shape=[jax.ShapeDtypeStruct((n_seq * seq_len, D_MODEL), F32),
                   jax.ShapeDtypeStruct((n_seq, 2, N_HEADS, HEAD_DIM, HEAD_DIM), F32)],
        scratch_shapes=[pltpu.VMEM((seq_len, wide), F32), pltpu.VMEM((seq_len, wide), F32)],
        compiler_params=_cparams(("parallel", "parallel")),
        name="retention_rope" if rope else "retention",
    )(*args)


def _lru_kernel(x_ref, gate_ref, cw_ref, cb_ref, gw_ref, gb_ref, lam_ref, s0_ref, o_ref, so_ref, *, seq_len):
    n = seq_len
    row = lax.broadcasted_iota(jnp.int32, (n, HEAD_DIM), 0)
    xc = _dw_conv(x_ref[...], cw_ref[...], row) + cb_ref[...]
    lam = lam_ref[...]
    hs = []
    for d in range(2):
        r_gate = jax.nn.sigmoid(_mm(xc, gw_ref[d, 0]) + gb_ref[d, 0:1, :])
        i_gate = jax.nn.sigmoid(_mm(xc, gw_ref[d, 1]) + gb_ref[d, 1:2, :])
        log_a = -LRU_C * r_gate * _softplus(-lam[d:d + 1, :])
        a = jnp.exp(log_a)
        u = jnp.sqrt(-jnp.tanh(log_a) * (1.0 + a * a)) * i_gate * xc
        step = 1
        while step < n:
            if d == 0:
                ok = row >= step
                sh = step
            else:
                ok = row < n - step
                sh = n - step
            a_sh = jnp.where(ok, pltpu.roll(a, sh, 0), 1.0)
            u_sh = jnp.where(ok, pltpu.roll(u, sh, 0), 0.0)
            u = a * u_sh + u
            a = a * a_sh
            step *= 2
        h = u + a * s0_ref[d:d + 1, :]
        hs.append(h)
        so_ref[d:d + 1, :] = h[n - 1:n, :] if d == 0 else h[0:1, :]
    o_ref[...] = (hs[0] + hs[1]) * _gelu(gate_ref[...])


def _rglru(proj, conv_w, conv_b, gate_w, gate_b, lam, s0, s0_map, n_seq, seq_len, row_off):
    return pl.pallas_call(
        functools.partial(_lru_kernel, seq_len=seq_len),
        grid=(n_seq, N_HEADS),
        in_specs=[pl.BlockSpec((seq_len, HEAD_DIM), lambda b, n: (row_off + b, COL_LX + n)),
                  pl.BlockSpec((seq_len, HEAD_DIM), lambda b, n: (row_off + b, COL_LG + n)),
                  pl.BlockSpec((4, HEAD_DIM), lambda b, n: (0, n)),
                  pl.BlockSpec((1, HEAD_DIM), lambda b, n: (0, n)),
                  pl.BlockSpec((2, 2, None, HEAD_DIM, HEAD_DIM), lambda b, n: (0, 0, n, 0, 0)),
                  pl.BlockSpec((2, 2, HEAD_DIM), lambda b, n: (0, 0, n)),
                  pl.BlockSpec((2, HEAD_DIM), lambda b, n: (0, n)),
                  s0_map],
        out_specs=[pl.BlockSpec((seq_len, HEAD_DIM), lambda b, n: (b, n)),
                   pl.BlockSpec((None, 2, HEAD_DIM), lambda b, n: (b, 0, n))],
        out_shape=[jax.ShapeDtypeStruct((n_seq * seq_len, D_MODEL), F32),
                   jax.ShapeDtypeStruct((n_seq, 2, D_MODEL), F32)],
        compiler_params=_cparams(("parallel", "parallel")),
        name="rglru",
    )(proj, proj, conv_w, conv_b, gate_w, gate_b, lam, s0)


def _dn_kernel(q_ref, k_ref, v_ref, z_ref, sm_ref, cwq_ref, cwk_ref, cwv_ref, par_ref, ng_ref, s0_ref,
               o_ref, so_ref, q_scr, k_scr, v_scr, c_scr, b_scr, of_scr, ob_scr, *, seq_len, hg):
    n = seq_len
    cc = DN_CHUNK
    sc = DN_SUPER
    n_super = n // sc
    head0 = pl.program_id(1) * hg
    roww = lax.broadcasted_iota(jnp.int32, (n, hg * HEAD_DIM), 0)
    row = lax.broadcasted_iota(jnp.int32, (n, HEAD_DIM), 0)
    lane = lax.broadcasted_iota(jnp.int32, (n, HEAD_DIM), 1)
    pos = row & (cc - 1)

    xq = _silu(_dw_conv(q_ref[...], cwq_ref[...], roww))
    xk = _silu(_dw_conv(k_ref[...], cwk_ref[...], roww))
    v_scr[...] = _silu(_dw_conv(v_ref[...], cwv_ref[...], roww))
    small = sm_ref[...]
    par = par_ref[...]
    g_all = -jnp.exp(par[0:1, :]) * _softplus(small + par[1:2, :])
    beta_all = jax.nn.sigmoid(small)
    cum_all = [g_all, g_all]
    step = 1
    while step < cc:
        cum_all[0] = cum_all[0] + jnp.where(pos >= step, pltpu.roll(cum_all[0], step, 0), 0.0)
        cum_all[1] = cum_all[1] + jnp.where(pos < cc - step, pltpu.roll(cum_all[1], n - step, 0), 0.0)
        step *= 2
    for hh in range(hg):
        cs = slice(hh * HEAD_DIM, (hh + 1) * HEAD_DIM)
        xqh, xkh = xq[:, cs], xk[:, cs]
        q_scr[:, cs] = xqh * lax.rsqrt(jnp.sum(xqh * xqh, axis=-1, keepdims=True) + EPS) * (HEAD_DIM ** -0.5)
        k_scr[:, cs] = xkh * lax.rsqrt(jnp.sum(xkh * xkh, axis=-1, keepdims=True) + EPS)
        head = head0 + hh
        for d in range(2):
            a_lane = lane == 2 * N_HEADS * d + head
            b_lane = lane == 2 * N_HEADS * d + N_HEADS + head
            cum = jnp.sum(jnp.where(a_lane, cum_all[d], 0.0), axis=-1, keepdims=True)
            beta = jnp.sum(jnp.where(b_lane, beta_all, 0.0), axis=-1, keepdims=True)
            c_scr[d, hh] = jnp.broadcast_to(cum, (n, HEAD_DIM))
            b_scr[d, hh] = jnp.broadcast_to(beta, (n, HEAD_DIM))

    ri = lax.broadcasted_iota(jnp.int32, (sc, sc), 0)
    cj = lax.broadcasted_iota(jnp.int32, (sc, sc), 1)
    sh = cc.bit_length() - 1
    same = (ri >> sh) == (cj >> sh)
    incl = (same & (ri >= cj), same & (ri <= cj))
    strict = (same & (ri > cj), same & (ri < cj))
    eye = jnp.where(ri == cj, 1.0, 0.0)
    level = [(ri >> 3) == (cj >> 3)]
    for b in range(4, sh + 1):
        level.append(((ri >> b) == (cj >> b)) & ((ri >> (b - 1)) != (cj >> (b - 1))))

    n_ch = sc // cc

    def super_chunks(chains):
        idx = range(len(chains))
        dd = [c[0] for c in chains]
        sl = [pl.ds(c[2], sc) for c in chains]
        cs = [slice(c[1] * HEAD_DIM, (c[1] + 1) * HEAD_DIM) for c in chains]
        q = [q_scr[sl[c], cs[c]] for c in idx]
        k = [k_scr[sl[c], cs[c]] for c in idx]
        v = [v_scr[sl[c], cs[c]] for c in idx]
        cum = [c_scr[dd[c], chains[c][1], sl[c], :] for c in idx]
        beta = [b_scr[dd[c], chains[c][1], sl[c], :] for c in idx]
        kk = [_mm_nt(k[c], k[c]) for c in idx]
        qk = [_mm_nt(q[c], k[c]) for c in idx]
        decay, x, attn = [], [], []
        for c in idx:
            cb = jnp.concatenate([cum[c], cum[c]], axis=1)
            diff = cb - cb.T
            dec = jnp.where(incl[dd[c]], jnp.exp(jnp.where(incl[dd[c]], diff, 0.0)), 0.0)
            bb = jnp.concatenate([beta[c], beta[c]], axis=1)
            x.append(-jnp.where(strict[dd[c]], kk[c] * bb * dec, 0.0))
            attn.append(qk[c] * dec)
        xp = [jnp.where(level[0], x[c], 0.0) for c in idx]
        p = [eye + xp[c] for c in idx]
        for _ in range(2):
            xp = [_mm(xp[c], xp[c]) for c in idx]
            pm = [_mm(p[c], xp[c]) for c in idx]
            p = [p[c] + pm[c] for c in idx]
        for lv in range(1, len(level)):
            t1 = [_mm(jnp.where(level[lv], x[c], 0.0), p[c]) for c in idx]
            t2 = [_mm(p[c], t1[c]) for c in idx]
            p = [p[c] + t2[c] for c in idx]
        rhs = [jnp.concatenate([v[c] * beta[c], k[c] * beta[c] * jnp.exp(cum[c])], axis=1) for c in idx]
        sol = [_mm(p[c], rhs[c]) for c in idx]
        qd = [q[c] * jnp.exp(cum[c]) for c in idx]
        s = [so_ref[dd[c], chains[c][1]] for c in idx]
        v_new = [[None] * n_ch for _ in idx]
        o_inter = [[None] * n_ch for _ in idx]
        for stp in range(n_ch):
            ch = [stp if dd[c] == 0 else n_ch - 1 - stp for c in idx]
            rs = [slice(ch[c] * cc, (ch[c] + 1) * cc) for c in idx]
            ws = [_mm(sol[c][rs[c], HEAD_DIM:], s[c]) for c in idx]
            for c in idx:
                o_inter[c][ch[c]] = _mm(qd[c][rs[c], :], s[c])
            kt = []
            for c in idx:
                cum_c = cum[c][rs[c], :]
                tot = cum_c[cc - 1:cc, :] if dd[c] == 0 else cum_c[0:1, :]
                v_new[c][ch[c]] = sol[c][rs[c], :HEAD_DIM] - ws[c]
                kt.append((k[c][rs[c], :] * jnp.exp(tot - cum_c), jnp.exp(tot)))
            upd = [_mm_tn(kt[c][0], v_new[c][ch[c]]) for c in idx]
            s = [s[c] * kt[c][1] + upd[c] for c in idx]
        oi = [_mm(attn[c], jnp.concatenate(v_new[c], axis=0)) for c in idx]
        for c in idx:
            o = jnp.concatenate(o_inter[c], axis=0) + oi[c]
            if dd[c] == 0:
                of_scr[sl[c], cs[c]] = o
            else:
                ob_scr[sl[c], cs[c]] = o
            so_ref[dd[c], chains[c][1]] = s[c]

    so_ref[...] = s0_ref[...]

    def body(i, carry):
        fwd = pl.multiple_of(i * sc, sc)
        bwd = pl.multiple_of((n_super - 1 - i) * sc, sc)
        super_chunks([(d, hh, fwd if d == 0 else bwd) for hh in range(hg) for d in range(2)])
        return carry

    lax.fori_loop(0, n_super, body, 0)
    tot = of_scr[...] + ob_scr[...]
    z = z_ref[...]
    for hh in range(hg):
        cs = slice(hh * HEAD_DIM, (hh + 1) * HEAD_DIM)
        o_ref[:, cs] = _rms(tot[:, cs]) * ng_ref[hh] * _silu(z[:, cs])


def _state_spec(s0, layer, hg):
    if layer is None:
        return pl.BlockSpec((None, 2, hg, HEAD_DIM, HEAD_DIM), lambda b, g: (b, 0, g, 0, 0))
    return pl.BlockSpec((None, None, 2, hg, HEAD_DIM, HEAD_DIM), lambda b, g: (b, layer, 0, g, 0, 0))


def _deltanet(proj, conv_w, par, norm_g, s0, layer, n_seq, seq_len, row_off, hg):
    wide = hg * HEAD_DIM

    def col(c0):
        return pl.BlockSpec((seq_len, wide), lambda b, g: (row_off + b, c0 // hg + g))

    def cw(c0):
        return pl.BlockSpec((4, wide), lambda b, g: (0, c0 // hg + g))

    vm = functools.partial(pltpu.VMEM, dtype=F32)
    return pl.pallas_call(
        functools.partial(_dn_kernel, seq_len=seq_len, hg=hg),
        grid=(n_seq, N_HEADS // hg),
        in_specs=[col(COL_DQ), col(COL_DK), col(COL_DV), col(COL_DZ),
                  pl.BlockSpec((seq_len, HEAD_DIM), lambda b, g: (row_off + b, COL_SMALL)),
                  cw(0), cw(N_HEADS), cw(2 * N_HEADS),
                  pl.BlockSpec((2, HEAD_DIM), lambda b, g: (0, 0)),
                  pl.BlockSpec((hg, 1, HEAD_DIM), lambda b, g: (g, 0, 0)),
                  _state_spec(s0, layer, hg)],
        out_specs=[pl.BlockSpec((seq_len, wide), lambda b, g: (b, g)),
                   pl.BlockSpec((None, 2, hg, HEAD_DIM, HEAD_DIM), lambda b, g: (b, 0, g, 0, 0))],
        out_shape=[jax.ShapeDtypeStruct((n_seq * seq_len, D_MODEL), F32),
                   jax.ShapeDtypeStruct((n_seq, 2, N_HEADS, HEAD_DIM, HEAD_DIM), F32)],
        scratch_shapes=[vm((seq_len, wide)), vm((seq_len, wide)), vm((seq_len, wide)),
                        vm((2, hg, seq_len, HEAD_DIM)), vm((2, hg, seq_len, HEAD_DIM)),
                        vm((seq_len, wide)), vm((seq_len, wide))],
        compiler_params=_cparams(("parallel", "parallel"),
                                 VMEM_LIMIT_WIDE if seq_len * hg > 2048 else VMEM_LIMIT),
        name="deltanet",
    )(proj, proj, proj, proj, proj, conv_w, conv_w, conv_w, par, norm_g, s0)


def _merge_kernel(retc_ref, retl_ref, lruc_ref, lrul_ref, dnc_ref, dnl_ref, g0_ref, g1_ref, g2_ref, x_ref, mod_ref,
                  ng_ref, wbr_ref, wout_ref, xo_ref, h_ref, *, ctx_blocks):
    is_ctx = pl.program_id(0) < ctx_blocks
    ret = jnp.where(is_ctx, retc_ref[...], retl_ref[...])
    lru = jnp.where(is_ctx, lruc_ref[...], lrul_ref[...])
    dn = jnp.where(is_ctx, dnc_ref[...], dnl_ref[...])
    merged = jax.nn.sigmoid(g0_ref[...]) * _mm(ret, wbr_ref[0])
    merged = merged + jax.nn.sigmoid(g1_ref[...]) * _mm(lru, wbr_ref[1])
    merged = merged + jax.nn.sigmoid(g2_ref[...]) * _mm(dn, wbr_ref[2])
    x = x_ref[...] + mod_ref[2:3, :] * _mm(merged, wout_ref[...])
    xo_ref[...] = x
    h_ref[...] = (_rms(x) * ng_ref[...] * (1.0 + mod_ref[4:5, :]) + mod_ref[3:4, :]).T.astype(BF16)


def _merge(branches, proj, x, mod, norm_g, w_br, w_out, mod_row, tm):
    t = x.shape[0]
    ctx_blocks = branches[0][0].shape[0] // tm
    row = pl.BlockSpec((tm, D_MODEL), lambda i: (i, 0))
    row_c = pl.BlockSpec((tm, D_MODEL), lambda i: (jnp.minimum(i, ctx_blocks - 1), 0))
    row_l = pl.BlockSpec((tm, D_MODEL), lambda i: (jnp.maximum(i - ctx_blocks, 0), 0))
    col0 = COL_BGATE * HEAD_DIM // D_MODEL

    def gate(k):
        return pl.BlockSpec((tm, D_MODEL), lambda i: (i, col0 + k))

    return pl.pallas_call(
        functools.partial(_merge_kernel, ctx_blocks=ctx_blocks),
        grid=(t // tm,),
        in_specs=[row_c, row_l, row_c, row_l, row_c, row_l, gate(0), gate(1), gate(2), row,
                  pl.BlockSpec((None, N_MOD, D_MODEL), lambda i: (mod_row(i), 0, 0)),
                  pl.BlockSpec((1, D_MODEL), lambda i: (0, 0)),
                  pl.BlockSpec((3, D_MODEL, D_MODEL), lambda i: (0, 0, 0)),
                  pl.BlockSpec((D_MODEL, D_MODEL), lambda i: (0, 0))],
        out_specs=[row, pl.BlockSpec((D_MODEL, tm), lambda i: (0, i))],
        out_shape=[jax.ShapeDtypeStruct((t, D_MODEL), F32), jax.ShapeDtypeStruct((D_MODEL, t), BF16)],
        compiler_params=_cparams(("parallel",)),
        name="merge",
    )(*branches[0], *branches[1], *branches[2], proj, proj, proj, x, mod, norm_g, w_br, w_out)


def _router_kernel(h_ref, wq_ref, keys_ref, thr_ref, e0_ref, e1_ref, q_scr, top_scr, cand_scr):
    tb = h_ref.shape[1]
    q_scr[...] = jnp.dot(wq_ref[...], h_ref[...], preferred_element_type=F32)

    top_scr[...] = jnp.full(top_scr.shape, NEG_BIG, F32)
    sub_row = lax.broadcasted_iota(jnp.int32, (SUBLANES, tb), 0)

    def top_values(x, dst):
        for kk in range(TOP_N):
            m = jnp.max(x, axis=0, keepdims=True)
            top_scr[dst, kk:kk + 1, :] = m
            x = jnp.where(x >= m, NEG_BIG, x)

    def body(hd, carry):
        base = pl.multiple_of(hd * 2 * N_KEYS, 2 * N_KEYS)
        s0 = _mm(keys_ref[0], q_scr[pl.ds(base, N_KEYS), :])
        s1 = _mm(keys_ref[1], q_scr[pl.ds(base + N_KEYS, N_KEYS), :])
        top_values(s0, 0)
        top_values(s1, 1)
        cand_scr[0:TOP_PAD, :] = top_scr[0, 0:1, :] + top_scr[1]
        a1 = top_scr[1, 0:SUBLANES, :]
        for p in range(1, N_MULTI):
            keep = sub_row < TOP_N // (p + 1)
            cand_scr[TOP_PAD + (p - 1) * SUBLANES:TOP_PAD + p * SUBLANES, :] = jnp.where(
                keep, top_scr[0, p:p + 1, :] + a1, NEG_BIG)
        cand_scr[CAND_ROWS - SUBLANES:CAND_ROWS, :] = top_scr[0, N_MULTI:TOP_N, :] + top_scr[1, 0:1, :]
        x = cand_scr[...]
        m0 = jnp.max(x, axis=0, keepdims=True)
        z = jnp.zeros_like(m0)
        m = m0
        for kk in range(PEER_TOPK):
            if kk > 0:
                x = jnp.where(x >= m, NEG_BIG, x)
                m = jnp.max(x, axis=0, keepdims=True)
            z = z + jnp.exp(m - m0)
        m_next = jnp.max(jnp.where(x >= m, NEG_BIG, x), axis=0, keepdims=True)
        tau = 0.5 * (m + m_next)
        max1 = top_scr[1, 0:1, :]
        thr = jnp.exp(tau - s0 - max1) / z
        e0 = jnp.exp(s0 - top_scr[0, 0:1, :])
        e1 = jnp.exp(s1 - max1) / z
        for c in range(tb // HEAD_DIM):
            ls = slice(c * HEAD_DIM, (c + 1) * HEAD_DIM)
            thr_ref[hd, c] = thr[:, ls]
            e0_ref[hd, c] = e0[:, ls]
            e1_ref[hd, c] = e1[:, ls]
        return carry

    lax.fori_loop(0, N_HEADS, body, 0)


def _router(h2, wq_t, keys, tb):
    t = h2.shape[1]
    out = pl.BlockSpec((N_HEADS, tb // HEAD_DIM, N_KEYS, HEAD_DIM), lambda i: (0, i, 0, 0))
    shp = jax.ShapeDtypeStruct((N_HEADS, t // HEAD_DIM, N_KEYS, HEAD_DIM), F32)
    return pl.pallas_call(
        _router_kernel,
        grid=(t // tb,),
        in_specs=[pl.BlockSpec((D_MODEL, tb), lambda i: (0, i)),
                  pl.BlockSpec((2 * N_KEYS * N_HEADS, D_MODEL), lambda i: (0, 0)),
                  pl.BlockSpec((2, N_KEYS, N_KEYS), lambda i: (0, 0, 0))],
        out_specs=[out, out, out],
        out_shape=[shp, shp, shp],
        scratch_shapes=[pltpu.VMEM((2 * N_KEYS * N_HEADS, tb), F32),
                        pltpu.VMEM((2, TOP_PAD, tb), F32),
                        pltpu.VMEM((CAND_ROWS, tb), F32)],
        compiler_params=_cparams(("parallel",)),
        name="peer_router",
    )(h2, wq_t, keys)


def _expert_kernel(h_ref, u0_ref, un_ref, vt_ref, thr_ref, e0_ref, e1_ref, x_ref, mod_ref, o_ref,
                   acc_scr, g_scr, act_a, act_b, *, tile_e):
    act_scr = (act_a, act_b)
    j = pl.program_id(1)
    tb = h_ref.shape[1]
    n_sub = tile_e // N_KEYS

    n_lane = tb // HEAD_DIM

    def activations(u_ref, dst):
        act = jnp.dot(u_ref[...], h_ref[...], preferred_element_type=F32)
        for c in range(n_lane):
            dst[c] = act[:, c * HEAD_DIM:(c + 1) * HEAD_DIM]

    @pl.when(j == 0)
    def _():
        acc_scr[...] = jnp.zeros_like(acc_scr)
        activations(u0_ref, act_scr[0])

    i0 = pl.multiple_of(j * n_sub, n_sub)
    n_part = tile_e // EXPERT_PART
    keys_per_part = EXPERT_PART // N_KEYS

    def step(cur, nxt):
        activations(un_ref, act_scr[nxt])
        acc = acc_scr[...]
        for part in range(n_part):
            for c in range(n_lane):
                thr = [thr_ref[hd, c, pl.ds(i0, n_sub), :] for hd in range(N_HEADS)]
                e0 = [e0_ref[hd, c, pl.ds(i0, n_sub), :] for hd in range(N_HEADS)]
                for r in range(keys_per_part):
                    ii = part * keys_per_part + r
                    rs = slice(ii * N_KEYS, (ii + 1) * N_KEYS)
                    wd = jnp.zeros((N_KEYS, HEAD_DIM), F32)
                    for hd in range(N_HEADS):
                        e1 = e1_ref[hd, c]
                        wd = wd + jnp.where(e1 >= thr[hd][ii:ii + 1, :], e1 * e0[hd][ii:ii + 1, :], 0.0)
                    g_scr[c, rs, :] = (_gelu(act_scr[cur][c, rs, :]) * wd).astype(BF16)
            rows = slice(part * EXPERT_PART, (part + 1) * EXPERT_PART)
            g_part = jnp.concatenate([g_scr[c, rows, :] for c in range(n_lane)], axis=1)
            acc = acc + jnp.dot(vt_ref[:, rows], g_part, preferred_element_type=F32)
        acc_scr[...] = acc

    slot = lax.rem(j, 2)

    @pl.when(slot == 0)
    def _():
        step(0, 1)

    @pl.when(slot == 1)
    def _():
        step(1, 0)

    @pl.when(j == pl.num_programs(1) - 1)
    def _():
        o_ref[...] = x_ref[...] + mod_ref[5:6, :] * acc_scr[...].T


def _experts(h2t, u_tab, vt_tab, routing, x, mod, mod_row, tb, tile_e):
    t = h2t.shape[1]
    n_exp = u_tab.shape[0]
    n_lane = tb // HEAD_DIM
    rt = pl.BlockSpec((N_HEADS, n_lane, N_KEYS, HEAD_DIM), lambda i, j: (0, i, 0, 0))
    n_tiles = n_exp // tile_e
    return pl.pallas_call(
        functools.partial(_expert_kernel, tile_e=tile_e),
        grid=(t // tb, n_tiles),
        in_specs=[pl.BlockSpec((D_MODEL, tb), lambda i, j: (0, i)),
                  pl.BlockSpec((tile_e, D_MODEL), lambda i, j: (0, 0)),
                  pl.BlockSpec((tile_e, D_MODEL), lambda i, j: (jnp.minimum(j + 1, n_tiles - 1), 0)),
                  pl.BlockSpec((None, D_MODEL, tile_e), lambda i, j: (j, 0, 0)),
                  rt, rt, rt,
                  pl.BlockSpec((tb, D_MODEL), lambda i, j: (i, 0)),
                  pl.BlockSpec((None, N_MOD, D_MODEL), lambda i, j: (mod_row(i), 0, 0))],
        out_specs=pl.BlockSpec((tb, D_MODEL), lambda i, j: (i, 0)),
        out_shape=jax.ShapeDtypeStruct((t, D_MODEL), F32),
        scratch_shapes=[pltpu.VMEM((D_MODEL, tb), F32), pltpu.VMEM((n_lane, tile_e, HEAD_DIM), BF16),
                        pltpu.VMEM((n_lane, tile_e, HEAD_DIM), F32), pltpu.VMEM((n_lane, tile_e, HEAD_DIM), F32)],
        compiler_params=_cparams(("parallel", "arbitrary")),
        name="peer_experts",
    )(h2t, u_tab, u_tab, vt_tab, *routing, x, mod)


def _final_norm_kernel(x_ref, g_ref, o_ref):
    o_ref[...] = _rms(x_ref[...]) * g_ref[...]


def _final_norm(x, g, tm):
    t = x.shape[0]
    row = pl.BlockSpec((tm, D_MODEL), lambda i: (i, 0))
    return pl.pallas_call(
        _final_norm_kernel, grid=(t // tm,),
        in_specs=[row, pl.BlockSpec((1, D_MODEL), lambda i: (0, 0))],
        out_specs=row, out_shape=jax.ShapeDtypeStruct((t, D_MODEL), F32),
        compiler_params=_cparams(("parallel",)), name="final_norm",
    )(x, g)


def _rope_tables(seq_len):
    rows = seq_len // GRID_W
    r = jnp.repeat(jnp.arange(rows, dtype=F32), GRID_W)
    col = jnp.tile(jnp.arange(GRID_W, dtype=F32), rows)
    nf = HEAD_DIM // 4
    inv = ROPE_BASE ** (-jnp.arange(nf, dtype=F32) / nf)
    ang = jnp.concatenate([r[:, None] * inv, col[:, None] * inv], axis=-1)
    cos, sin = jnp.cos(ang), jnp.sin(ang)
    return jnp.concatenate([cos, cos], axis=-1), jnp.concatenate([-sin, sin], axis=-1)


def _lanes(a):
    return jnp.broadcast_to(jnp.moveaxis(a, -1, 0)[..., None], (a.shape[-1],) + a.shape[:-1] + (HEAD_DIM,))


def kernel(x_prompt, x_sample, c, state_ret, state_lru, state_dn, c_ctx, w_mod, b_mod, norm1_g, norm2_g, w_in, ret_gamma_logit, ret_norm_g, lru_conv_w, lru_conv_b, lru_gate_w, lru_gate_b, lru_lambda, dn_conv_w, dn_a_log, dn_dt_bias, dn_norm_g, w_br, w_out, peer_w_q, peer_sub_keys, peer_u, peer_v, final_norm_g):
    n_ctx, l_ctx, _ = x_prompt.shape
    n_lat, l_lat, _ = x_sample.shape
    t_ctx, t_lat = n_ctx * l_ctx, n_lat * l_lat
    assert t_ctx % l_lat == 0
    tb = 512
    assert l_lat % tb == 0 and t_ctx % tb == 0
    ctx_blocks, per_seq = t_ctx // tb, l_lat // tb

    def mod_row(i):
        return jnp.where(i < ctx_blocks, 0, 1 + (i - ctx_blocks) // per_seq)

    tm = 256
    ctx_blocks_m, per_seq_m = t_ctx // tm, l_lat // tm

    def mod_row_m(i):
        return jnp.where(i < ctx_blocks_m, 0, 1 + (i - ctx_blocks_m) // per_seq_m)

    x = jnp.concatenate([x_prompt.reshape(t_ctx, D_MODEL), x_sample.reshape(t_lat, D_MODEL)], axis=0)
    n_cond = 16
    cond = jnp.zeros((n_cond, D_MODEL), F32).at[0].set(c_ctx).at[1:1 + n_lat].set(c)
    mods = _modulation(cond, w_mod, b_mod).reshape(DEPTH, n_cond, N_MOD, D_MODEL)

    w_in_r = jnp.concatenate(
        [w_in[:, :, :N_MAIN], w_in[:, :, N_MAIN + N_SMALL:], w_in[:, :, N_MAIN:N_MAIN + N_SMALL],
         jnp.zeros((DEPTH, D_MODEL, HEAD_DIM - N_SMALL), F32)], axis=-1).astype(BF16)
    w_br_b, w_out_b = w_br.astype(BF16), w_out.astype(BF16)
    wq_t = jnp.swapaxes(peer_w_q, 1, 2).astype(BF16)
    keys_b = peer_sub_keys.astype(BF16)
    u_b = peer_u.astype(BF16)
    vt_b = jnp.swapaxes(peer_v.reshape(DEPTH, N_EXPERTS // EXPERT_TILE, EXPERT_TILE, D_MODEL), 2, 3).astype(BF16)
    gam = _lanes(ret_gamma_logit)
    lane_pad = jnp.zeros((DEPTH, 2, N_HEADS), F32)
    dn_ab = jnp.stack([dn_a_log, dn_dt_bias], axis=1)
    dn_par = jnp.concatenate([dn_ab[:, :, 0], lane_pad, dn_ab[:, :, 1], lane_pad,
                              jnp.zeros((DEPTH, 2, HEAD_DIM - N_SMALL), F32)], axis=-1)
    rope_tabs = _rope_tables(l_lat)
    zero_ret = jnp.zeros((n_ctx, 2, N_HEADS, HEAD_DIM, HEAD_DIM), F32)
    zero_lru = jnp.zeros((n_ctx, 2, D_MODEL), F32)

    vec_zero = pl.BlockSpec((None, 2, HEAD_DIM), lambda b, n: (b, 0, n))
    row_off_lat = t_ctx // l_lat

    ret_states, lru_states, dn_states = [], [], []
    for l in range(DEPTH):
        vec_lat = pl.BlockSpec((None, None, 2, HEAD_DIM), lambda b, n, l=l: (b, l, 0, n))
        proj = _in_proj(x, mods[l], norm1_g[l][None], w_in_r[l],
                        lambda i: jnp.where(i < row_off_lat, 0, 1 + i - row_off_lat), l_lat)

        ng_ret = ret_norm_g[l][:, None, :]
        o_ret_c, s_ret = _retention(proj, gam[:, l], ng_ret, zero_ret, None, None, n_ctx, l_ctx, 0, 4)
        o_ret_l, _ = _retention(proj, gam[:, l], ng_ret, state_ret, l, rope_tabs, n_lat, l_lat, row_off_lat, 4)

        lru_args = (lru_conv_w[l], lru_conv_b[l][None], lru_gate_w[l], lru_gate_b[l], lru_lambda[l])
        o_lru_c, s_lru = _rglru(proj, *lru_args, zero_lru, vec_zero, n_ctx, l_ctx, 0)
        o_lru_l, _ = _rglru(proj, *lru_args, state_lru, vec_lat, n_lat, l_lat, row_off_lat)

        ng_dn = dn_norm_g[l][:, None, :]
        o_dn_c, s_dn = _deltanet(proj, dn_conv_w[l], dn_par[l], ng_dn, zero_ret, None, n_ctx, l_ctx, 0, 4)
        o_dn_l, _ = _deltanet(proj, dn_conv_w[l], dn_par[l], ng_dn, state_dn, l, n_lat, l_lat, row_off_lat, 4)

        branches = ((o_ret_c, o_ret_l), (o_lru_c, o_lru_l), (o_dn_c, o_dn_l))
        x, h2 = _merge(branches, proj, x, mods[l], norm2_g[l][None], w_br_b[l], w_out_b[l], mod_row_m, tm)
        routing = _router(h2, wq_t[l], keys_b[l], tb)
        x = _experts(h2, u_b[l], vt_b[l], routing, x, mods[l], mod_row, tb, EXPERT_TILE)

        ret_states.append(s_ret)
        lru_states.append(s_lru)
        dn_states.append(s_dn)

    y = _final_norm(x, final_norm_g[None], tb)
    y_prompt = y[:t_ctx].reshape(n_ctx, l_ctx, D_MODEL)
    y_sample = y[t_ctx:].reshape(n_lat, l_lat, D_MODEL)
    return (y_prompt, y_sample, jnp.stack(ret_states, axis=1), jnp.stack(lru_states, axis=1),
            jnp.stack(dn_states, axis=1))
```

```python
import functools
import math

import jax
import jax.numpy as jnp
from jax import lax
from jax.experimental import pallas as pl
from jax.experimental.pallas import tpu as pltpu

F32 = jnp.float32
BF16 = jnp.bfloat16

D_MODEL = 1024
DEPTH = 4
N_MOD = 6
EPS = 1e-6
GRID_W = 64
ROPE_BASE = 10000.0
N_HEADS = 8
HEAD_DIM = 128
RET_CHUNK = 128
DN_CHUNK = 64
DN_SUPER = 256
LRU_C = 8.0
N_KEYS = 128
PEER_TOPK = 16
N_EXPERTS = N_KEYS * N_KEYS
SUBLANES = 8
TOP_N = PEER_TOPK + 1
TOP_PAD = -(-TOP_N // SUBLANES) * SUBLANES
N_MULTI = TOP_N - SUBLANES
CAND_ROWS = TOP_PAD + N_MULTI * SUBLANES
assert TOP_N // 2 <= SUBLANES and TOP_N // (N_MULTI + 1) == 1
EXPERT_TILE = 1024
EXPERT_PART = 256
NEG_BIG = -3.0e38

COL_RQ, COL_RK, COL_RV, COL_RG = 0, 8, 16, 24
COL_LX, COL_LG = 32, 40
COL_DQ, COL_DK, COL_DV, COL_DZ = 48, 56, 64, 72
N_MAIN = 80 * 128
N_SMALL = 4 * N_HEADS
COL_BGATE = 80
COL_SMALL = 104
N_PROJ = 105 * 128

TOKEN_BLOCK = 512
MERGE_ROWS = 256
IN_PROJ_COLS = 15 * 128
MOD_COLS = 1536
MIXER_HEADS = 4
VMEM_LIMIT = 48 * 1024 * 1024
VMEM_LIMIT_WIDE = 58 * 1024 * 1024


def _cparams(sem, vmem_limit=VMEM_LIMIT):
    return pltpu.CompilerParams(dimension_semantics=sem, vmem_limit_bytes=vmem_limit)


def _mm(a, b):
    return jnp.dot(a.astype(BF16), b.astype(BF16), preferred_element_type=F32)


def _mm_nt(a, b):
    return lax.dot_general(a.astype(BF16), b.astype(BF16), (((1,), (1,)), ((), ())),
                           preferred_element_type=F32)


def _mm_tn(a, b):
    return lax.dot_general(a.astype(BF16), b.astype(BF16), (((0,), (0,)), ((), ())),
                           preferred_element_type=F32)


def _softplus(x):
    return jnp.maximum(x, 0.0) + jnp.log1p(jnp.exp(-jnp.abs(x)))


def _silu(x):
    return x * jax.nn.sigmoid(x)


def _gelu(x):
    return 0.5 * x * (1.0 + lax.erf(x * math.sqrt(0.5)))


def _rms(x):
    return x * lax.rsqrt(jnp.mean(x * x, axis=-1, keepdims=True) + EPS)


def _shift_rows(x, s, row):
    n = x.shape[0]
    if s == 0:
        return x
    y = pltpu.roll(x, (-s) % n, 0)
    ok = (row + s >= 0) & (row + s < n)
    return jnp.where(ok, y, 0.0)


def _dw_conv(x, w, row):
    y = _shift_rows(x, -2, row) * w[0:1, :]
    y = y + _shift_rows(x, -1, row) * w[1:2, :]
    y = y + x * w[2:3, :]
    y = y + _shift_rows(x, 1, row) * w[3:4, :]
    return y


def _mod_kernel(c_ref, w_ref, b_ref, o_ref):
    c = c_ref[...]
    o_ref[...] = jnp.dot(_silu(c), w_ref[...], precision=lax.Precision.HIGHEST,
                         preferred_element_type=F32) + b_ref[...]


def _modulation(cond, w_mod, b_mod):
    n_rows = cond.shape[0]
    tn = MOD_COLS
    n_out = N_MOD * D_MODEL
    return pl.pallas_call(
        _mod_kernel,
        grid=(DEPTH, n_out // tn),
        in_specs=[pl.BlockSpec((n_rows, D_MODEL), lambda l, j: (0, 0)),
                  pl.BlockSpec((None, D_MODEL, tn), lambda l, j: (l, 0, j)),
                  pl.BlockSpec((None, 1, tn), lambda l, j: (l, 0, j))],
        out_specs=pl.BlockSpec((None, n_rows, tn), lambda l, j: (l, 0, j)),
        out_shape=jax.ShapeDtypeStruct((DEPTH, n_rows, n_out), F32),
        compiler_params=_cparams(("parallel", "parallel")),
        name="modulation",
    )(cond, w_mod, b_mod.reshape(DEPTH, 1, n_out))


def _in_proj_kernel(x_ref, mod_ref, g_ref, w_ref, o_ref, h_scr):
    @pl.when(pl.program_id(1) == 0)
    def _():
        y = _rms(x_ref[...]) * g_ref[...]
        h_scr[...] = (y * (1.0 + mod_ref[1:2, :]) + mod_ref[0:1, :]).astype(BF16)

    o_ref[...] = jnp.dot(h_scr[...], w_ref[...], preferred_element_type=F32)


def _in_proj(x, mod, norm_g, w, layer, mod_row, tm):
    t = x.shape[0]
    tn = IN_PROJ_COLS
    return pl.pallas_call(
        _in_proj_kernel,
        grid=(t // tm, N_PROJ // tn),
        in_specs=[pl.BlockSpec((tm, D_MODEL), lambda i, j: (i, 0)),
                  pl.BlockSpec((None, None, N_MOD, D_MODEL), lambda i, j: (layer, mod_row(i), 0, 0)),
                  pl.BlockSpec((1, D_MODEL), lambda i, j: (0, 0)),
                  pl.BlockSpec((None, D_MODEL, tn), lambda i, j: (layer, 0, j))],
        out_specs=pl.BlockSpec((tm, tn), lambda i, j: (i, j)),
        out_shape=jax.ShapeDtypeStruct((t, N_PROJ), F32),
        scratch_shapes=[pltpu.VMEM((tm, D_MODEL), BF16)],
        compiler_params=_cparams(("parallel", "arbitrary")),
        name="in_proj",
    )(x, mod, norm_g, w)


def _ret_kernel(*refs, seq_len, rope, hg):
    if rope:
        (q_ref, k_ref, v_ref, g_ref, gam_ref, ng_ref, s0_ref, cs_ref, sn_ref,
         o_ref, so_ref, of_scr, ob_scr) = refs
    else:
        (q_ref, k_ref, v_ref, g_ref, gam_ref, ng_ref, s0_ref,
         o_ref, so_ref, of_scr, ob_scr) = refs
    c = RET_CHUNK
    n_chunks = seq_len // c
    r = lax.broadcasted_iota(jnp.int32, (c, HEAD_DIM), 0).astype(F32)
    ci = lax.broadcasted_iota(jnp.int32, (c, c), 0)
    si = lax.broadcasted_iota(jnp.int32, (c, c), 1)
    dmat = (ci - si).astype(F32)
    scale = HEAD_DIM ** -0.5

    chains = [(hh, d) for hh in range(hg) for d in range(2)]
    idx = range(len(chains))
    dec, qsc, ksc, gch = [], [], [], []
    for hh, d in chains:
        lg = -_softplus(-gam_ref[hh, d:d + 1, :])
        if d == 0:
            dec.append(jnp.where(dmat >= 0, jnp.exp(lg * jnp.maximum(dmat, 0.0)), 0.0))
            qsc.append(jnp.exp(lg * (r + 1.0)))
            ksc.append(jnp.exp(lg * (c - 1.0 - r)))
        else:
            dec.append(jnp.where(dmat <= 0, jnp.exp(lg * jnp.maximum(-dmat, 0.0)), 0.0))
            qsc.append(jnp.exp(lg * (c - r)))
            ksc.append(jnp.exp(lg * r))
        gch.append(jnp.exp(lg * c))

    s = [s0_ref[d, hh] for hh, d in chains]
    for stp in range(n_chunks):
        sl = [pl.ds((stp if d == 0 else n_chunks - 1 - stp) * c, c) for _, d in chains]
        cs = [slice(hh * HEAD_DIM, (hh + 1) * HEAD_DIM) for hh, _ in chains]
        q = [q_ref[sl[i], cs[i]] for i in idx]
        k = [k_ref[sl[i], cs[i]] * scale for i in idx]
        v = [v_ref[sl[i], cs[i]] for i in idx]
        if rope:
            cos = [cs_ref[sl[i], :] for i in idx]
            sin = [sn_ref[sl[i], :] for i in idx]
            q = [q[i] * cos[i] + pltpu.roll(q[i], HEAD_DIM // 2, 1) * sin[i] for i in idx]
            k = [k[i] * cos[i] + pltpu.roll(k[i], HEAD_DIM // 2, 1) * sin[i] for i in idx]
        sc = [_mm_nt(q[i], k[i]) for i in idx]
        qs = [_mm(q[i] * qsc[i], s[i]) for i in idx]
        kv = [_mm_tn(k[i] * ksc[i], v[i]) for i in idx]
        oi = [_mm(sc[i] * dec[i], v[i]) for i in idx]
        s = [s[i] * gch[i] + kv[i] for i in idx]
        for i in idx:
            if chains[i][1] == 0:
                of_scr[sl[i], cs[i]] = oi[i] + qs[i]
            else:
                ob_scr[sl[i], cs[i]] = oi[i] + qs[i]
    for i in idx:
        so_ref[chains[i][1], chains[i][0]] = s[i]
    tot = of_scr[...] + ob_scr[...]
    g = g_ref[...]
    for hh in range(hg):
        cs1 = slice(hh * HEAD_DIM, (hh + 1) * HEAD_DIM)
        o_ref[:, cs1] = _rms(tot[:, cs1]) * ng_ref[hh] * _silu(g[:, cs1])


def _retention(proj, gam, norm_g, s0, layer, rope_tabs, n_seq, seq_len, row_off, hg):
    rope = rope_tabs is not None
    wide = hg * HEAD_DIM

    def col(c0):
        return pl.BlockSpec((seq_len, wide), lambda b, g: (row_off + b, c0 // hg + g))

    in_specs = [col(COL_RQ), col(COL_RK), col(COL_RV), col(COL_RG),
                pl.BlockSpec((hg, 2, HEAD_DIM), lambda b, g: (g, 0, 0)),
                pl.BlockSpec((hg, 1, HEAD_DIM), lambda b, g: (g, 0, 0)),
                _state_spec(s0, layer, hg)]
    args = [proj, proj, proj, proj, gam, norm_g, s0]
    if rope:
        tab = pl.BlockSpec((seq_len, HEAD_DIM), lambda b, g: (0, 0))
        in_specs += [tab, tab]
        args += list(rope_tabs)
    return pl.pallas_call(
        functools.partial(_ret_kernel, seq_len=seq_len, rope=rope, hg=hg),
        grid=(n_seq, N_HEADS // hg),
        in_specs=in_specs,
        out_specs=[pl.BlockSpec((seq_len, wide), lambda b, g: (b, g)),
                   pl.BlockSpec((None, 2, hg, HEAD_DIM, HEAD_DIM), lambda b, g: (b, 0, g, 0, 0))],
        out_shape=[jax.ShapeDtypeStruct((n_seq * seq_len, D_MODEL), F32),
                   jax.ShapeDtypeStruct((n_seq, 2, N_HEADS, HEAD_DIM, HEAD_DIM), F32)],
        scratch_shapes=[pltpu.VMEM((seq_len, wide), F32), pltpu.VMEM((seq_len, wide), F32)],
        compiler_params=_cparams(("parallel", "parallel")),
        name="retention_rope" if rope else "retention",
    )(*args)


def _lru_kernel(x_ref, gate_ref, cw_ref, cb_ref, gw_ref, gb_ref, lam_ref, s0_ref, o_ref, so_ref, *, seq_len):
    n = seq_len
    row = lax.broadcasted_iota(jnp.int32, (n, HEAD_DIM), 0)
    xc = _dw_conv(x_ref[...], cw_ref[...], row) + cb_ref[...]
    lam = lam_ref[...]
    hs = []
    for d in range(2):
        r_gate = jax.nn.sigmoid(_mm(xc, gw_ref[d, 0]) + gb_ref[d, 0:1, :])
        i_gate = jax.nn.sigmoid(_mm(xc, gw_ref[d, 1]) + gb_ref[d, 1:2, :])
        log_a = -LRU_C * r_gate * _softplus(-lam[d:d + 1, :])
        a = jnp.exp(log_a)
        u = jnp.sqrt(-jnp.tanh(log_a) * (1.0 + a * a)) * i_gate * xc
        step = 1
        while step < n:
            if d == 0:
                ok = row >= step
                sh = step
            else:
                ok = row < n - step
                sh = n - step
            a_sh = jnp.where(ok, pltpu.roll(a, sh, 0), 1.0)
            u_sh = jnp.where(ok, pltpu.roll(u, sh, 0), 0.0)
            u = a * u_sh + u
            a = a * a_sh
            step *= 2
        h = u + a * s0_ref[d:d + 1, :]
        hs.append(h)
        so_ref[d:d + 1, :] = h[n - 1:n, :] if d == 0 else h[0:1, :]
    o_ref[...] = (hs[0] + hs[1]) * _gelu(gate_ref[...])


def _rglru(proj, conv_w, conv_b, gate_w, gate_b, lam, s0, s0_map, n_seq, seq_len, row_off):
    return pl.pallas_call(
        functools.partial(_lru_kernel, seq_len=seq_len),
        grid=(n_seq, N_HEADS),
        in_specs=[pl.BlockSpec((seq_len, HEAD_DIM), lambda b, n: (row_off + b, COL_LX + n)),
                  pl.BlockSpec((seq_len, HEAD_DIM), lambda b, n: (row_off + b, COL_LG + n)),
                  pl.BlockSpec((4, HEAD_DIM), lambda b, n: (0, n)),
                  pl.BlockSpec((1, HEAD_DIM), lambda b, n: (0, n)),
                  pl.BlockSpec((2, 2, None, HEAD_DIM, HEAD_DIM), lambda b, n: (0, 0, n, 0, 0)),
                  pl.BlockSpec((2, 2, HEAD_DIM), lambda b, n: (0, 0, n)),
                  pl.BlockSpec((2, HEAD_DIM), lambda b, n: (0, n)),
                  s0_map],
        out_specs=[pl.BlockSpec((seq_len, HEAD_DIM), lambda b, n: (b, n)),
                   pl.BlockSpec((None, 2, HEAD_DIM), lambda b, n: (b, 0, n))],
        out_shape=[jax.ShapeDtypeStruct((n_seq * seq_len, D_MODEL), F32),
                   jax.ShapeDtypeStruct((n_seq, 2, D_MODEL), F32)],
        compiler_params=_cparams(("parallel", "parallel")),
        name="rglru",
    )(proj, proj, conv_w, conv_b, gate_w, gate_b, lam, s0)


def _dn_kernel(q_ref, k_ref, v_ref, z_ref, sm_ref, cwq_ref, cwk_ref, cwv_ref, par_ref, ng_ref, s0_ref,
               o_ref, so_ref, q_scr, k_scr, v_scr, c_scr, b_scr, of_scr, ob_scr, *, seq_len, hg):
    n = seq_len
    cc = DN_CHUNK
    sc = DN_SUPER
    n_super = n // sc
    head0 = pl.program_id(1) * hg
    roww = lax.broadcasted_iota(jnp.int32, (n, hg * HEAD_DIM), 0)
    row = lax.broadcasted_iota(jnp.int32, (n, HEAD_DIM), 0)
    lane = lax.broadcasted_iota(jnp.int32, (n, HEAD_DIM), 1)
    pos = row & (cc - 1)

    xq = _silu(_dw_conv(q_ref[...], cwq_ref[...], roww))
    xk = _silu(_dw_conv(k_ref[...], cwk_ref[...], roww))
    v_scr[...] = _silu(_dw_conv(v_ref[...], cwv_ref[...], roww))
    small = sm_ref[...]
    par = par_ref[...]
    g_all = -jnp.exp(par[0:1, :]) * _softplus(small + par[1:2, :])
    beta_all = jax.nn.sigmoid(small)
    cum_all = [g_all, g_all]
    step = 1
    while step < cc:
        cum_all[0] = cum_all[0] + jnp.where(pos >= step, pltpu.roll(cum_all[0], step, 0), 0.0)
        cum_all[1] = cum_all[1] + jnp.where(pos < cc - step, pltpu.roll(cum_all[1], n - step, 0), 0.0)
        step *= 2
    for hh in range(hg):
        cs = slice(hh * HEAD_DIM, (hh + 1) * HEAD_DIM)
        xqh, xkh = xq[:, cs], xk[:, cs]
        q_scr[:, cs] = xqh * lax.rsqrt(jnp.sum(xqh * xqh, axis=-1, keepdims=True) + EPS) * (HEAD_DIM ** -0.5)
        k_scr[:, cs] = xkh * lax.rsqrt(jnp.sum(xkh * xkh, axis=-1, keepdims=True) + EPS)
        head = head0 + hh
        for d in range(2):
            a_lane = lane == 2 * N_HEADS * d + head
            b_lane = lane == 2 * N_HEADS * d + N_HEADS + head
            cum = jnp.sum(jnp.where(a_lane, cum_all[d], 0.0), axis=-1, keepdims=True)
            beta = jnp.sum(jnp.where(b_lane, beta_all, 0.0), axis=-1, keepdims=True)
            c_scr[d, hh] = jnp.broadcast_to(cum, (n, HEAD_DIM))
            b_scr[d, hh] = jnp.broadcast_to(beta, (n, HEAD_DIM))

    ri = lax.broadcasted_iota(jnp.int32, (sc, sc), 0)
    cj = lax.broadcasted_iota(jnp.int32, (sc, sc), 1)
    sh = cc.bit_length() - 1
    same = (ri >> sh) == (cj >> sh)
    incl = (same & (ri >= cj), same & (ri <= cj))
    strict = (same & (ri > cj), same & (ri < cj))
    eye = jnp.where(ri == cj, 1.0, 0.0)
    level = [(ri >> 3) == (cj >> 3)]
    for b in range(4, sh + 1):
        level.append(((ri >> b) == (cj >> b)) & ((ri >> (b - 1)) != (cj >> (b - 1))))

    n_ch = sc // cc

    def super_chunks(chains):
        idx = range(len(chains))
        dd = [c[0] for c in chains]
        sl = [pl.ds(c[2], sc) for c in chains]
        cs = [slice(c[1] * HEAD_DIM, (c[1] + 1) * HEAD_DIM) for c in chains]
        q = [q_scr[sl[c], cs[c]] for c in idx]
        k = [k_scr[sl[c], cs[c]] for c in idx]
        v = [v_scr[sl[c], cs[c]] for c in idx]
        cum = [c_scr[dd[c], chains[c][1], sl[c], :] for c in idx]
        beta = [b_scr[dd[c], chains[c][1], sl[c], :] for c in idx]
        kk = [_mm_nt(k[c], k[c]) for c in idx]
        qk = [_mm_nt(q[c], k[c]) for c in idx]
        decay, x, attn = [], [], []
        for c in idx:
            cb = jnp.concatenate([cum[c], cum[c]], axis=1)
            diff = cb - cb.T
            dec = jnp.where(incl[dd[c]], jnp.exp(jnp.where(incl[dd[c]], diff, 0.0)), 0.0)
            bb = jnp.concatenate([beta[c], beta[c]], axis=1)
            x.append(-jnp.where(strict[dd[c]], kk[c] * bb * dec, 0.0))
            attn.append(qk[c] * dec)
        xp = [jnp.where(level[0], x[c], 0.0) for c in idx]
        p = [eye + xp[c] for c in idx]
        for _ in range(2):
            xp = [_mm(xp[c], xp[c]) for c in idx]
            pm = [_mm(p[c], xp[c]) for c in idx]
            p = [p[c] + pm[c] for c in idx]
        for lv in range(1, len(level)):
            t1 = [_mm(jnp.where(level[lv], x[c], 0.0), p[c]) for c in idx]
            t2 = [_mm(p[c], t1[c]) for c in idx]
            p = [p[c] + t2[c] for c in idx]
        rhs = [jnp.concatenate([v[c] * beta[c], k[c] * beta[c] * jnp.exp(cum[c])], axis=1) for c in idx]
        sol = [_mm(p[c], rhs[c]) for c in idx]
        qd = [q[c] * jnp.exp(cum[c]) for c in idx]
        s = [so_ref[dd[c], chains[c][1]] for c in idx]
        v_new = [[None] * n_ch for _ in idx]
        o_inter = [[None] * n_ch for _ in idx]
        for stp in range(n_ch):
            ch = [stp if dd[c] == 0 else n_ch - 1 - stp for c in idx]
            rs = [slice(ch[c] * cc, (ch[c] + 1) * cc) for c in idx]
            ws = [_mm(sol[c][rs[c], HEAD_DIM:], s[c]) for c in idx]
            for c in idx:
                o_inter[c][ch[c]] = _mm(qd[c][rs[c], :], s[c])
            kt = []
            for c in idx:
                cum_c = cum[c][rs[c], :]
                tot = cum_c[cc - 1:cc, :] if dd[c] == 0 else cum_c[0:1, :]
                v_new[c][ch[c]] = sol[c][rs[c], :HEAD_DIM] - ws[c]
                kt.append((k[c][rs[c], :] * jnp.exp(tot - cum_c), jnp.exp(tot)))
            upd = [_mm_tn(kt[c][0], v_new[c][ch[c]]) for c in idx]
            s = [s[c] * kt[c][1] + upd[c] for c in idx]
        oi = [_mm(attn[c], jnp.concatenate(v_new[c], axis=0)) for c in idx]
        for c in idx:
            o = jnp.concatenate(o_inter[c], axis=0) + oi[c]
            if dd[c] == 0:
                of_scr[sl[c], cs[c]] = o
            else:
                ob_scr[sl[c], cs[c]] = o
            so_ref[dd[c], chains[c][1]] = s[c]

    so_ref[...] = s0_ref[...]

    def body(i, carry):
        fwd = pl.multiple_of(i * sc, sc)
        bwd = pl.multiple_of((n_super - 1 - i) * sc, sc)
        super_chunks([(d, hh, fwd if d == 0 else bwd) for hh in range(hg) for d in range(2)])
        return carry

    lax.fori_loop(0, n_super, body, 0)
    tot = of_scr[...] + ob_scr[...]
    z = z_ref[...]
    for hh in range(hg):
        cs = slice(hh * HEAD_DIM, (hh + 1) * HEAD_DIM)
        o_ref[:, cs] = _rms(tot[:, cs]) * ng_ref[hh] * _silu(z[:, cs])


def _state_spec(s0, layer, hg):
    if layer is None:
        return pl.BlockSpec((None, 2, hg, HEAD_DIM, HEAD_DIM), lambda b, g: (b, 0, g, 0, 0))
    return pl.BlockSpec((None, None, 2, hg, HEAD_DIM, HEAD_DIM), lambda b, g: (b, layer, 0, g, 0, 0))


def _deltanet(proj, conv_w, par, norm_g, s0, layer, n_seq, seq_len, row_off, hg):
    wide = hg * HEAD_DIM

    def col(c0):
        return pl.BlockSpec((seq_len, wide), lambda b, g: (row_off + b, c0 // hg + g))

    def cw(c0):
        return pl.BlockSpec((4, wide), lambda b, g: (0, c0 // hg + g))

    vm = functools.partial(pltpu.VMEM, dtype=F32)
    return pl.pallas_call(
        functools.partial(_dn_kernel, seq_len=seq_len, hg=hg),
        grid=(n_seq, N_HEADS // hg),
        in_specs=[col(COL_DQ), col(COL_DK), col(COL_DV), col(COL_DZ),
                  pl.BlockSpec((seq_len, HEAD_DIM), lambda b, g: (row_off + b, COL_SMALL)),
                  cw(0), cw(N_HEADS), cw(2 * N_HEADS),
                  pl.BlockSpec((2, HEAD_DIM), lambda b, g: (0, 0)),
                  pl.BlockSpec((hg, 1, HEAD_DIM), lambda b, g: (g, 0, 0)),
                  _state_spec(s0, layer, hg)],
        out_specs=[pl.BlockSpec((seq_len, wide), lambda b, g: (b, g)),
                   pl.BlockSpec((None, 2, hg, HEAD_DIM, HEAD_DIM), lambda b, g: (b, 0, g, 0, 0))],
        out_shape=[jax.ShapeDtypeStruct((n_seq * seq_len, D_MODEL), F32),
                   jax.ShapeDtypeStruct((n_seq, 2, N_HEADS, HEAD_DIM, HEAD_DIM), F32)],
        scratch_shapes=[vm((seq_len, wide)), vm((seq_len, wide)), vm((seq_len, wide)),
                        vm((2, hg, seq_len, HEAD_DIM)), vm((2, hg, seq_len, HEAD_DIM)),
                        vm((seq_len, wide)), vm((seq_len, wide))],
        compiler_params=_cparams(("parallel", "parallel"),
                                 VMEM_LIMIT_WIDE if seq_len * hg > 2048 else VMEM_LIMIT),
        name="deltanet",
    )(proj, proj, proj, proj, proj, conv_w, conv_w, conv_w, par, norm_g, s0)


def _merge_kernel(retc_ref, retl_ref, lruc_ref, lrul_ref, dnc_ref, dnl_ref, g0_ref, g1_ref, g2_ref, x_ref, mod_ref,
                  ng_ref, wbr_ref, wout_ref, xo_ref, h_ref, *, ctx_blocks):
    is_ctx = pl.program_id(0) < ctx_blocks
    ret = jnp.where(is_ctx, retc_ref[...], retl_ref[...])
    lru = jnp.where(is_ctx, lruc_ref[...], lrul_ref[...])
    dn = jnp.where(is_ctx, dnc_ref[...], dnl_ref[...])
    merged = jax.nn.sigmoid(g0_ref[...]) * _mm(ret, wbr_ref[0])
    merged = merged + jax.nn.sigmoid(g1_ref[...]) * _mm(lru, wbr_ref[1])
    merged = merged + jax.nn.sigmoid(g2_ref[...]) * _mm(dn, wbr_ref[2])
    x = x_ref[...] + mod_ref[2:3, :] * _mm(merged, wout_ref[...])
    xo_ref[...] = x
    h_ref[...] = (_rms(x) * ng_ref[...] * (1.0 + mod_ref[4:5, :]) + mod_ref[3:4, :]).T.astype(BF16)


def _merge(branches, proj, x, mod, norm_g, w_br, w_out, layer, mod_row, tm):
    t = x.shape[0]
    ctx_blocks = branches[0][0].shape[0] // tm
    row = pl.BlockSpec((tm, D_MODEL), lambda i: (i, 0))
    row_c = pl.BlockSpec((tm, D_MODEL), lambda i: (jnp.minimum(i, ctx_blocks - 1), 0))
    row_l = pl.BlockSpec((tm, D_MODEL), lambda i: (jnp.maximum(i - ctx_blocks, 0), 0))
    col0 = COL_BGATE * HEAD_DIM // D_MODEL

    def gate(k):
        return pl.BlockSpec((tm, D_MODEL), lambda i: (i, col0 + k))

    return pl.pallas_call(
        functools.partial(_merge_kernel, ctx_blocks=ctx_blocks),
        grid=(t // tm,),
        in_specs=[row_c, row_l, row_c, row_l, row_c, row_l, gate(0), gate(1), gate(2), row,
                  pl.BlockSpec((None, None, N_MOD, D_MODEL), lambda i: (layer, mod_row(i), 0, 0)),
                  pl.BlockSpec((1, D_MODEL), lambda i: (0, 0)),
                  pl.BlockSpec((None, 3, D_MODEL, D_MODEL), lambda i: (layer, 0, 0, 0)),
                  pl.BlockSpec((None, D_MODEL, D_MODEL), lambda i: (layer, 0, 0))],
        out_specs=[row, pl.BlockSpec((D_MODEL, tm), lambda i: (0, i))],
        out_shape=[jax.ShapeDtypeStruct((t, D_MODEL), F32), jax.ShapeDtypeStruct((D_MODEL, t), BF16)],
        compiler_params=_cparams(("parallel",)),
        name="merge",
    )(*branches[0], *branches[1], *branches[2], proj, proj, proj, x, mod, norm_g, w_br, w_out)


def _router_kernel(h_ref, wq_ref, keys_ref, thr_ref, e0_ref, e1_ref, q_scr, top_scr, cand_scr):
    tb = h_ref.shape[1]
    q_scr[...] = jnp.dot(wq_ref[...], h_ref[...], preferred_element_type=F32)

    top_scr[...] = jnp.full(top_scr.shape, NEG_BIG, F32)
    sub_row = lax.broadcasted_iota(jnp.int32, (SUBLANES, tb), 0)

    def top_values(x, dst):
        for kk in range(TOP_N):
            m = jnp.max(x, axis=0, keepdims=True)
            top_scr[dst, kk:kk + 1, :] = m
            x = jnp.where(x >= m, NEG_BIG, x)

    def body(hd, carry):
        base = pl.multiple_of(hd * 2 * N_KEYS, 2 * N_KEYS)
        s0 = _mm(keys_ref[0], q_scr[pl.ds(base, N_KEYS), :])
        s1 = _mm(keys_ref[1], q_scr[pl.ds(base + N_KEYS, N_KEYS), :])
        top_values(s0, 0)
        top_values(s1, 1)
        cand_scr[0:TOP_PAD, :] = top_scr[0, 0:1, :] + top_scr[1]
        a1 = top_scr[1, 0:SUBLANES, :]
        for p in range(1, N_MULTI):
            keep = sub_row < TOP_N // (p + 1)
            cand_scr[TOP_PAD + (p - 1) * SUBLANES:TOP_PAD + p * SUBLANES, :] = jnp.where(
                keep, top_scr[0, p:p + 1, :] + a1, NEG_BIG)
        cand_scr[CAND_ROWS - SUBLANES:CAND_ROWS, :] = top_scr[0, N_MULTI:TOP_N, :] + top_scr[1, 0:1, :]
        x = cand_scr[...]
        m0 = jnp.max(x, axis=0, keepdims=True)
        z = jnp.zeros_like(m0)
        m = m0
        for kk in range(PEER_TOPK):
            if kk > 0:
                x = jnp.where(x >= m, NEG_BIG, x)
                m = jnp.max(x, axis=0, keepdims=True)
            z = z + jnp.exp(m - m0)
        m_next = jnp.max(jnp.where(x >= m, NEG_BIG, x), axis=0, keepdims=True)
        tau = 0.5 * (m + m_next)
        max1 = top_scr[1, 0:1, :]
        thr = jnp.exp(tau - s0 - max1) / z
        e0 = 0.5 * jnp.exp(s0 - top_scr[0, 0:1, :])
        e1 = jnp.exp(s1 - max1) / z
        for c in range(tb // HEAD_DIM):
            ls = slice(c * HEAD_DIM, (c + 1) * HEAD_DIM)
            thr_ref[hd, c] = thr[:, ls]
            e0_ref[hd, c] = e0[:, ls]
            e1_ref[hd, c] = e1[:, ls]
        return carry

    lax.fori_loop(0, N_HEADS, body, 0)


def _router(h2, wq_t, keys, layer, tb):
    t = h2.shape[1]
    out = pl.BlockSpec((N_HEADS, tb // HEAD_DIM, N_KEYS, HEAD_DIM), lambda i: (0, i, 0, 0))
    shp = jax.ShapeDtypeStruct((N_HEADS, t // HEAD_DIM, N_KEYS, HEAD_DIM), F32)
    return pl.pallas_call(
        _router_kernel,
        grid=(t // tb,),
        in_specs=[pl.BlockSpec((D_MODEL, tb), lambda i: (0, i)),
                  pl.BlockSpec((None, 2 * N_KEYS * N_HEADS, D_MODEL), lambda i: (layer, 0, 0)),
                  pl.BlockSpec((None, 2, N_KEYS, N_KEYS), lambda i: (layer, 0, 0, 0))],
        out_specs=[out, out, out],
        out_shape=[shp, shp, shp],
        scratch_shapes=[pltpu.VMEM((2 * N_KEYS * N_HEADS, tb), F32),
                        pltpu.VMEM((2, TOP_PAD, tb), F32),
                        pltpu.VMEM((CAND_ROWS, tb), F32)],
        compiler_params=_cparams(("parallel",)),
        name="peer_router",
    )(h2, wq_t, keys)


def _expert_kernel(h_ref, u0_ref, un_ref, vt_ref, thr_ref, e0_ref, e1_ref, x_ref, mod_ref, o_ref,
                   acc_scr, g_scr, act_a, act_b, *, tile_e):
    act_scr = (act_a, act_b)
    j = pl.program_id(1)
    tb = h_ref.shape[1]
    n_sub = tile_e // N_KEYS

    n_lane = tb // HEAD_DIM

    def activations(u_ref, dst):
        act = jnp.dot(u_ref[...], h_ref[...], preferred_element_type=F32)
        for c in range(n_lane):
            dst[c] = act[:, c * HEAD_DIM:(c + 1) * HEAD_DIM]

    @pl.when(j == 0)
    def _():
        acc_scr[...] = jnp.zeros_like(acc_scr)
        activations(u0_ref, act_scr[0])

    i0 = pl.multiple_of(j * n_sub, n_sub)
    n_part = tile_e // EXPERT_PART
    keys_per_part = EXPERT_PART // N_KEYS

    def step(cur, nxt):
        activations(un_ref, act_scr[nxt])
        acc = acc_scr[...]
        for part in range(n_part):
            for c in range(n_lane):
                thr = [thr_ref[hd, c, pl.ds(i0, n_sub), :] for hd in range(N_HEADS)]
                e0 = [e0_ref[hd, c, pl.ds(i0, n_sub), :] for hd in range(N_HEADS)]
                for r in range(keys_per_part):
                    ii = part * keys_per_part + r
                    rs = slice(ii * N_KEYS, (ii + 1) * N_KEYS)
                    wd = None
                    for hd in range(N_HEADS):
                        e1 = e1_ref[hd, c]
                        w = jnp.where(e1 >= thr[hd][ii:ii + 1, :], e1 * e0[hd][ii:ii + 1, :], 0.0)
                        wd = w if wd is None else wd + w
                    x = act_scr[cur][c, rs, :]
                    g_scr[c, rs, :] = ((x * wd) * (1.0 + lax.erf(x * math.sqrt(0.5)))).astype(BF16)
            rows = slice(part * EXPERT_PART, (part + 1) * EXPERT_PART)
            g_part = jnp.concatenate([g_scr[c, rows, :] for c in range(n_lane)], axis=1)
            acc = acc + jnp.dot(vt_ref[:, rows], g_part, preferred_element_type=F32)
        acc_scr[...] = acc

    slot = lax.rem(j, 2)

    @pl.when(slot == 0)
    def _():
        step(0, 1)

    @pl.when(slot == 1)
    def _():
        step(1, 0)

    @pl.when(j == pl.num_programs(1) - 1)
    def _():
        o_ref[...] = x_ref[...] + mod_ref[5:6, :] * acc_scr[...].T


def _experts(h2t, u_tab, vt_tab, routing, x, mod, layer, mod_row, tb, tile_e):
    t = h2t.shape[1]
    n_exp = u_tab.shape[1]
    n_lane = tb // HEAD_DIM
    rt = pl.BlockSpec((N_HEADS, n_lane, N_KEYS, HEAD_DIM), lambda i, j: (0, i, 0, 0))
    n_tiles = n_exp // tile_e
    return pl.pallas_call(
        functools.partial(_expert_kernel, tile_e=tile_e),
        grid=(t // tb, n_tiles),
        in_specs=[pl.BlockSpec((D_MODEL, tb), lambda i, j: (0, i)),
                  pl.BlockSpec((None, tile_e, D_MODEL), lambda i, j: (layer, 0, 0)),
                  pl.BlockSpec((None, tile_e, D_MODEL), lambda i, j: (layer, jnp.minimum(j + 1, n_tiles - 1), 0)),
                  pl.BlockSpec((None, None, D_MODEL, tile_e), lambda i, j: (layer, j, 0, 0)),
                  rt, rt, rt,
                  pl.BlockSpec((tb, D_MODEL), lambda i, j: (i, 0)),
                  pl.BlockSpec((None, None, N_MOD, D_MODEL), lambda i, j: (layer, mod_row(i), 0, 0))],
        out_specs=pl.BlockSpec((tb, D_MODEL), lambda i, j: (i, 0)),
        out_shape=jax.ShapeDtypeStruct((t, D_MODEL), F32),
        scratch_shapes=[pltpu.VMEM((D_MODEL, tb), F32), pltpu.VMEM((n_lane, tile_e, HEAD_DIM), BF16),
                        pltpu.VMEM((n_lane, tile_e, HEAD_DIM), F32), pltpu.VMEM((n_lane, tile_e, HEAD_DIM), F32)],
        compiler_params=_cparams(("parallel", "arbitrary")),
        name="peer_experts",
    )(h2t, u_tab, u_tab, vt_tab, *routing, x, mod)


def _final_norm_kernel(x_ref, g_ref, o_ref):
    o_ref[...] = _rms(x_ref[...]) * g_ref[...]


def _final_norm(x, g, tm):
    t = x.shape[0]
    row = pl.BlockSpec((tm, D_MODEL), lambda i: (i, 0))
    return pl.pallas_call(
        _final_norm_kernel, grid=(t // tm,),
        in_specs=[row, pl.BlockSpec((1, D_MODEL), lambda i: (0, 0))],
        out_specs=row, out_shape=jax.ShapeDtypeStruct((t, D_MODEL), F32),
        compiler_params=_cparams(("parallel",)), name="final_norm",
    )(x, g)


def _rope_tables(seq_len):
    rows = seq_len // GRID_W
    r = jnp.repeat(jnp.arange(rows, dtype=F32), GRID_W)
    col = jnp.tile(jnp.arange(GRID_W, dtype=F32), rows)
    nf = HEAD_DIM // 4
    inv = ROPE_BASE ** (-jnp.arange(nf, dtype=F32) / nf)
    ang = jnp.concatenate([r[:, None] * inv, col[:, None] * inv], axis=-1)
    cos, sin = jnp.cos(ang), jnp.sin(ang)
    return jnp.concatenate([cos, cos], axis=-1), jnp.concatenate([-sin, sin], axis=-1)


def _lanes(a):
    return jnp.broadcast_to(jnp.moveaxis(a, -1, 0)[..., None], (a.shape[-1],) + a.shape[:-1] + (HEAD_DIM,))


def kernel(x_prompt, x_sample, c, state_ret, state_lru, state_dn, c_ctx, w_mod, b_mod, norm1_g, norm2_g, w_in, ret_gamma_logit, ret_norm_g, lru_conv_w, lru_conv_b, lru_gate_w, lru_gate_b, lru_lambda, dn_conv_w, dn_a_log, dn_dt_bias, dn_norm_g, w_br, w_out, peer_w_q, peer_sub_keys, peer_u, peer_v, final_norm_g):
    n_ctx, l_ctx, _ = x_prompt.shape
    n_lat, l_lat, _ = x_sample.shape
    t_ctx, t_lat = n_ctx * l_ctx, n_lat * l_lat
    assert t_ctx % l_lat == 0
    tb = TOKEN_BLOCK
    assert l_lat % tb == 0 and t_ctx % tb == 0
    ctx_blocks, per_seq = t_ctx // tb, l_lat // tb

    def mod_row(i):
        return jnp.where(i < ctx_blocks, 0, 1 + (i - ctx_blocks) // per_seq)

    tm = MERGE_ROWS
    ctx_blocks_m, per_seq_m = t_ctx // tm, l_lat // tm

    def mod_row_m(i):
        return jnp.where(i < ctx_blocks_m, 0, 1 + (i - ctx_blocks_m) // per_seq_m)

    x = jnp.concatenate([x_prompt.reshape(t_ctx, D_MODEL), x_sample.reshape(t_lat, D_MODEL)], axis=0)
    n_cond = 16
    cond = jnp.zeros((n_cond, D_MODEL), F32).at[0].set(c_ctx).at[1:1 + n_lat].set(c)
    mods = _modulation(cond, w_mod, b_mod).reshape(DEPTH, n_cond, N_MOD, D_MODEL)

    w_in_r = jnp.concatenate(
        [w_in[:, :, :N_MAIN], w_in[:, :, N_MAIN + N_SMALL:], w_in[:, :, N_MAIN:N_MAIN + N_SMALL],
         jnp.zeros((DEPTH, D_MODEL, HEAD_DIM - N_SMALL), F32)], axis=-1).astype(BF16)
    w_br_b, w_out_b = w_br.astype(BF16), w_out.astype(BF16)
    wq_t = jnp.swapaxes(peer_w_q, 1, 2).astype(BF16)
    keys_b = peer_sub_keys.astype(BF16)
    u_b = peer_u.astype(BF16)
    vt_b = jnp.swapaxes(peer_v.reshape(DEPTH, N_EXPERTS // EXPERT_TILE, EXPERT_TILE, D_MODEL), 2, 3).astype(BF16)
    gam = _lanes(ret_gamma_logit)
    lane_pad = jnp.zeros((DEPTH, 2, N_HEADS), F32)
    dn_ab = jnp.stack([dn_a_log, dn_dt_bias], axis=1)
    dn_par = jnp.concatenate([dn_ab[:, :, 0], lane_pad, dn_ab[:, :, 1], lane_pad,
                              jnp.zeros((DEPTH, 2, HEAD_DIM - N_SMALL), F32)], axis=-1)
    rope_tabs = _rope_tables(l_lat)
    zero_ret = jnp.zeros((n_ctx, 2, N_HEADS, HEAD_DIM, HEAD_DIM), F32)
    zero_lru = jnp.zeros((n_ctx, 2, D_MODEL), F32)

    vec_zero = pl.BlockSpec((None, 2, HEAD_DIM), lambda b, n: (b, 0, n))
    row_off_lat = t_ctx // l_lat

    ret_states, lru_states, dn_states = [], [], []
    for l in range(DEPTH):
        vec_lat = pl.BlockSpec((None, None, 2, HEAD_DIM), lambda b, n, l=l: (b, l, 0, n))
        proj = _in_proj(x, mods, norm1_g[l][None], w_in_r, l,
                        lambda i: jnp.where(i < row_off_lat, 0, 1 + i - row_off_lat), l_lat)

        ng_ret = ret_norm_g[l][:, None, :]
        o_ret_c, s_ret = _retention(proj, gam[:, l], ng_ret, zero_ret, None, None, n_ctx, l_ctx, 0, MIXER_HEADS)
        o_ret_l, _ = _retention(proj, gam[:, l], ng_ret, state_ret, l, rope_tabs, n_lat, l_lat, row_off_lat, MIXER_HEADS)

        lru_args = (lru_conv_w[l], lru_conv_b[l][None], lru_gate_w[l], lru_gate_b[l], lru_lambda[l])
        o_lru_c, s_lru = _rglru(proj, *lru_args, zero_lru, vec_zero, n_ctx, l_ctx, 0)
        o_lru_l, _ = _rglru(proj, *lru_args, state_lru, vec_lat, n_lat, l_lat, row_off_lat)

        ng_dn = dn_norm_g[l][:, None, :]
        o_dn_c, s_dn = _deltanet(proj, dn_conv_w[l], dn_par[l], ng_dn, zero_ret, None, n_ctx, l_ctx, 0, MIXER_HEADS)
        o_dn_l, _ = _deltanet(proj, dn_conv_w[l], dn_par[l], ng_dn, state_dn, l, n_lat, l_lat, row_off_lat, MIXER_HEADS)

        branches = ((o_ret_c, o_ret_l), (o_lru_c, o_lru_l), (o_dn_c, o_dn_l))
        x, h2 = _merge(branches, proj, x, mods, norm2_g[l][None], w_br_b, w_out_b, l, mod_row_m, tm)
        routing = _router(h2, wq_t, keys_b, l, tb)
        x = _experts(h2, u_b, vt_b, routing, x, mods, l, mod_row, tb, EXPERT_TILE)

        ret_states.append(s_ret)
        lru_states.append(s_lru)
        dn_states.append(s_dn)

    y = _final_norm(x, final_norm_g[None], tb)
    y_prompt = y[:t_ctx].reshape(n_ctx, l_ctx, D_MODEL)
    y_sample = y[t_ctx:].reshape(n_lat, l_lat, D_MODEL)
    return (y_prompt, y_sample, jnp.stack(ret_states, axis=1), jnp.stack(lru_states, axis=1),
            jnp.stack(dn_states, axis=1))
```

```python
import functools
import math

import jax
import jax.numpy as jnp
from jax import lax
from jax.experimental import pallas as pl
from jax.experimental.pallas import tpu as pltpu

F32 = jnp.float32
BF16 = jnp.bfloat16

D_MODEL = 1024
DEPTH = 4
N_MOD = 6
EPS = 1e-6
GRID_W = 64
ROPE_BASE = 10000.0
N_HEADS = 8
HEAD_DIM = 128
RET_CHUNK = 128
DN_CHUNK = 64
DN_SUPER = 256
LRU_C = 8.0
N_KEYS = 128
PEER_TOPK = 16
N_EXPERTS = N_KEYS * N_KEYS
SUBLANES = 8
TOP_N = PEER_TOPK + 1
TOP_PAD = -(-TOP_N // SUBLANES) * SUBLANES
N_MULTI = TOP_N - SUBLANES
CAND_ROWS = TOP_PAD + N_MULTI * SUBLANES
assert TOP_N // 2 <= SUBLANES and TOP_N // (N_MULTI + 1) == 1
EXPERT_TILE = 1024
EXPERT_PART = 256
NEG_BIG = -3.0e38

COL_RQ, COL_RK, COL_RV, COL_RG = 0, 8, 16, 24
COL_LX, COL_LG = 32, 40
COL_DQ, COL_DK, COL_DV, COL_DZ = 48, 56, 64, 72
N_MAIN = 80 * 128
N_SMALL = 4 * N_HEADS
COL_BGATE = 80
COL_SMALL = 104
N_PROJ = 105 * 128

TOKEN_BLOCK = 512
MERGE_ROWS = 256
IN_PROJ_COLS = 15 * 128
MOD_COLS = 1536
MIXER_HEADS = 4
LRU_STEP_ROWS = 2048
VMEM_LIMIT = 48 * 1024 * 1024
VMEM_LIMIT_WIDE = 58 * 1024 * 1024


def _cparams(sem, vmem_limit=VMEM_LIMIT):
    return pltpu.CompilerParams(dimension_semantics=sem, vmem_limit_bytes=vmem_limit)


def _mm(a, b):
    return jnp.dot(a.astype(BF16), b.astype(BF16), preferred_element_type=F32)


def _mm_nt(a, b):
    return lax.dot_general(a.astype(BF16), b.astype(BF16), (((1,), (1,)), ((), ())),
                           preferred_element_type=F32)


def _mm_tn(a, b):
    return lax.dot_general(a.astype(BF16), b.astype(BF16), (((0,), (0,)), ((), ())),
                           preferred_element_type=F32)


def _softplus(x):
    return jnp.maximum(x, 0.0) + jnp.log1p(jnp.exp(-jnp.abs(x)))


def _silu(x):
    return x * jax.nn.sigmoid(x)


def _gelu(x):
    return 0.5 * x * (1.0 + lax.erf(x * math.sqrt(0.5)))


def _rms(x):
    return x * lax.rsqrt(jnp.mean(x * x, axis=-1, keepdims=True) + EPS)


def _shift_rows(x, s, row):
    n = x.shape[0]
    if s == 0:
        return x
    y = pltpu.roll(x, (-s) % n, 0)
    ok = (row + s >= 0) & (row + s < n)
    return jnp.where(ok, y, 0.0)


def _dw_conv(x, w, row):
    y = _shift_rows(x, -2, row) * w[0:1, :]
    y = y + _shift_rows(x, -1, row) * w[1:2, :]
    y = y + x * w[2:3, :]
    y = y + _shift_rows(x, 1, row) * w[3:4, :]
    return y


def _mod_kernel(c_ref, w_ref, b_ref, o_ref):
    c = c_ref[...]
    o_ref[...] = jnp.dot(_silu(c), w_ref[...], precision=lax.Precision.HIGHEST,
                         preferred_element_type=F32) + b_ref[...]


def _modulation(cond, w_mod, b_mod):
    n_rows = cond.shape[0]
    tn = MOD_COLS
    n_out = N_MOD * D_MODEL
    return pl.pallas_call(
        _mod_kernel,
        grid=(DEPTH, n_out // tn),
        in_specs=[pl.BlockSpec((n_rows, D_MODEL), lambda l, j: (0, 0)),
                  pl.BlockSpec((None, D_MODEL, tn), lambda l, j: (l, 0, j)),
                  pl.BlockSpec((None, 1, tn), lambda l, j: (l, 0, j))],
        out_specs=pl.BlockSpec((None, n_rows, tn), lambda l, j: (l, 0, j)),
        out_shape=jax.ShapeDtypeStruct((DEPTH, n_rows, n_out), F32),
        compiler_params=_cparams(("parallel", "parallel")),
        name="modulation",
    )(cond, w_mod, b_mod.reshape(DEPTH, 1, n_out))


def _in_proj_kernel(x_ref, mod_ref, g_ref, w_ref, o_ref, h_scr):
    @pl.when(pl.program_id(1) == 0)
    def _():
        y = _rms(x_ref[...]) * g_ref[...]
        h_scr[...] = (y * (1.0 + mod_ref[1:2, :]) + mod_ref[0:1, :]).astype(BF16)

    o_ref[...] = jnp.dot(h_scr[...], w_ref[...], preferred_element_type=F32)


def _in_proj(x, mod, norm_g, w, layer, mod_row, tm):
    t = x.shape[0]
    tn = IN_PROJ_COLS
    return pl.pallas_call(
        _in_proj_kernel,
        grid=(t // tm, N_PROJ // tn),
        in_specs=[pl.BlockSpec((tm, D_MODEL), lambda i, j: (i, 0)),
                  pl.BlockSpec((None, None, N_MOD, D_MODEL), lambda i, j: (layer, mod_row(i), 0, 0)),
                  pl.BlockSpec((1, D_MODEL), lambda i, j: (0, 0)),
                  pl.BlockSpec((None, D_MODEL, tn), lambda i, j: (layer, 0, j))],
        out_specs=pl.BlockSpec((tm, tn), lambda i, j: (i, j)),
        out_shape=jax.ShapeDtypeStruct((t, N_PROJ), F32),
        scratch_shapes=[pltpu.VMEM((tm, D_MODEL), BF16)],
        compiler_params=_cparams(("parallel", "arbitrary")),
        name="in_proj",
    )(x, mod, norm_g, w)


def _ret_kernel(*refs, seq_len, rope, hg):
    if rope:
        (q_ref, k_ref, v_ref, g_ref, gam_ref, ng_ref, s0_ref, cs_ref, sn_ref,
         o_ref, so_ref, of_scr, ob_scr) = refs
    else:
        (q_ref, k_ref, v_ref, g_ref, gam_ref, ng_ref, s0_ref,
         o_ref, so_ref, of_scr, ob_scr) = refs
    c = RET_CHUNK
    n_chunks = seq_len // c
    r = lax.broadcasted_iota(jnp.int32, (c, HEAD_DIM), 0).astype(F32)
    ci = lax.broadcasted_iota(jnp.int32, (c, c), 0)
    si = lax.broadcasted_iota(jnp.int32, (c, c), 1)
    dmat = (ci - si).astype(F32)
    scale = HEAD_DIM ** -0.5

    chains = [(hh, d) for hh in range(hg) for d in range(2)]
    idx = range(len(chains))
    dec, qsc, ksc, gch = [], [], [], []
    for hh, d in chains:
        lg = -_softplus(-gam_ref[hh, d:d + 1, :])
        if d == 0:
            dec.append(jnp.where(dmat >= 0, jnp.exp(lg * jnp.maximum(dmat, 0.0)), 0.0))
            qsc.append(jnp.exp(lg * (r + 1.0)))
            ksc.append(jnp.exp(lg * (c - 1.0 - r)))
        else:
            dec.append(jnp.where(dmat <= 0, jnp.exp(lg * jnp.maximum(-dmat, 0.0)), 0.0))
            qsc.append(jnp.exp(lg * (c - r)))
            ksc.append(jnp.exp(lg * r))
        gch.append(jnp.exp(lg * c))

    s = [s0_ref[d, hh] for hh, d in chains]
    for stp in range(n_chunks):
        sl = [pl.ds((stp if d == 0 else n_chunks - 1 - stp) * c, c) for _, d in chains]
        cs = [slice(hh * HEAD_DIM, (hh + 1) * HEAD_DIM) for hh, _ in chains]
        q = [q_ref[sl[i], cs[i]] for i in idx]
        k = [k_ref[sl[i], cs[i]] * scale for i in idx]
        v = [v_ref[sl[i], cs[i]] for i in idx]
        if rope:
            cos = [cs_ref[sl[i], :] for i in idx]
            sin = [sn_ref[sl[i], :] for i in idx]
            q = [q[i] * cos[i] + pltpu.roll(q[i], HEAD_DIM // 2, 1) * sin[i] for i in idx]
            k = [k[i] * cos[i] + pltpu.roll(k[i], HEAD_DIM // 2, 1) * sin[i] for i in idx]
        sc = [_mm_nt(q[i], k[i]) for i in idx]
        qs = [_mm(q[i] * qsc[i], s[i]) for i in idx]
        kv = [_mm_tn(k[i] * ksc[i], v[i]) for i in idx]
        oi = [_mm(sc[i] * dec[i], v[i]) for i in idx]
        s = [s[i] * gch[i] + kv[i] for i in idx]
        for i in idx:
            if chains[i][1] == 0:
                of_scr[sl[i], cs[i]] = oi[i] + qs[i]
            else:
                ob_scr[sl[i], cs[i]] = oi[i] + qs[i]
    for i in idx:
        so_ref[chains[i][1], chains[i][0]] = s[i]
    tot = of_scr[...] + ob_scr[...]
    g = g_ref[...]
    for hh in range(hg):
        cs1 = slice(hh * HEAD_DIM, (hh + 1) * HEAD_DIM)
        o_ref[:, cs1] = _rms(tot[:, cs1]) * ng_ref[hh] * _silu(g[:, cs1])


def _retention(proj, gam, norm_g, s0, layer, rope_tabs, n_seq, seq_len, row_off, hg):
    rope = rope_tabs is not None
    wide = hg * HEAD_DIM

    def col(c0):
        return pl.BlockSpec((seq_len, wide), lambda b, g: (row_off + b, c0 // hg + g))

    in_specs = [col(COL_RQ), col(COL_RK), col(COL_RV), col(COL_RG),
                pl.BlockSpec((hg, 2, HEAD_DIM), lambda b, g: (g, 0, 0)),
                pl.BlockSpec((hg, 1, HEAD_DIM), lambda b, g: (g, 0, 0)),
                _state_spec(s0, layer, hg)]
    args = [proj, proj, proj, proj, gam, norm_g, s0]
    if rope:
        tab = pl.BlockSpec((seq_len, HEAD_DIM), lambda b, g: (0, 0))
        in_specs += [tab, tab]
        args += list(rope_tabs)
    return pl.pallas_call(
        functools.partial(_ret_kernel, seq_len=seq_len, rope=rope, hg=hg),
        grid=(n_seq, N_HEADS // hg),
        in_specs=in_specs,
        out_specs=[pl.BlockSpec((seq_len, wide), lambda b, g: (b, g)),
                   pl.BlockSpec((None, 2, hg, HEAD_DIM, HEAD_DIM), lambda b, g: (b, 0, g, 0, 0))],
        out_shape=[jax.ShapeDtypeStruct((n_seq * seq_len, D_MODEL), F32),
                   jax.ShapeDtypeStruct((n_seq, 2, N_HEADS, HEAD_DIM, HEAD_DIM), F32)],
        scratch_shapes=[pltpu.VMEM((seq_len, wide), F32), pltpu.VMEM((seq_len, wide), F32)],
        compiler_params=_cparams(("parallel", "parallel")),
        name="retention_rope" if rope else "retention",
    )(*args)


def _lru_kernel(x_ref, gate_ref, cw_ref, cb_ref, gw_ref, gb_ref, lam_ref, s0_ref, o_ref, so_ref, *, seq_len, nb):
    n = seq_len
    row = lax.broadcasted_iota(jnp.int32, (n, nb * HEAD_DIM), 0)
    xc = _dw_conv(x_ref[...], cw_ref[...], row) + cb_ref[...]
    lam = lam_ref[...]

    def gate(d, which):
        cols = [_mm(xc[:, s * HEAD_DIM:(s + 1) * HEAD_DIM], gw_ref[d, which, s]) for s in range(nb)]
        return jax.nn.sigmoid(jnp.concatenate(cols, axis=1) + gb_ref[d, which:which + 1, :])

    hs = []
    for d in range(2):
        r_gate = gate(d, 0)
        i_gate = gate(d, 1)
        log_a = -LRU_C * r_gate * _softplus(-lam[d:d + 1, :])
        a = jnp.exp(log_a)
        u = jnp.sqrt(-jnp.tanh(log_a) * (1.0 + a * a)) * i_gate * xc
        step = 1
        while step < n:
            if d == 0:
                ok = row >= step
                sh = step
            else:
                ok = row < n - step
                sh = n - step
            a_sh = jnp.where(ok, pltpu.roll(a, sh, 0), 1.0)
            u_sh = jnp.where(ok, pltpu.roll(u, sh, 0), 0.0)
            u = a * u_sh + u
            a = a * a_sh
            step *= 2
        h = u + a * s0_ref[d:d + 1, :]
        hs.append(h)
        so_ref[d:d + 1, :] = h[n - 1:n, :] if d == 0 else h[0:1, :]
    o_ref[...] = (hs[0] + hs[1]) * _gelu(gate_ref[...])


def _lru_blocks(seq_len):
    return max(1, min(N_HEADS, LRU_STEP_ROWS // seq_len))


def _rglru(proj, conv_w, conv_b, gate_w, gate_b, lam, s0, layer, n_seq, seq_len, row_off, nb):
    wide = nb * HEAD_DIM
    if layer is None:
        s0_spec = pl.BlockSpec((None, 2, wide), lambda b, n: (b, 0, n))
    else:
        s0_spec = pl.BlockSpec((None, None, 2, wide), lambda b, n: (b, layer, 0, n))
    return pl.pallas_call(
        functools.partial(_lru_kernel, seq_len=seq_len, nb=nb),
        grid=(n_seq, N_HEADS // nb),
        in_specs=[pl.BlockSpec((seq_len, wide), lambda b, n: (row_off + b, COL_LX // nb + n)),
                  pl.BlockSpec((seq_len, wide), lambda b, n: (row_off + b, COL_LG // nb + n)),
                  pl.BlockSpec((4, wide), lambda b, n: (0, n)),
                  pl.BlockSpec((1, wide), lambda b, n: (0, n)),
                  pl.BlockSpec((2, 2, nb, HEAD_DIM, HEAD_DIM), lambda b, n: (0, 0, n, 0, 0)),
                  pl.BlockSpec((2, 2, wide), lambda b, n: (0, 0, n)),
                  pl.BlockSpec((2, wide), lambda b, n: (0, n)),
                  s0_spec],
        out_specs=[pl.BlockSpec((seq_len, wide), lambda b, n: (b, n)),
                   pl.BlockSpec((None, 2, wide), lambda b, n: (b, 0, n))],
        out_shape=[jax.ShapeDtypeStruct((n_seq * seq_len, D_MODEL), F32),
                   jax.ShapeDtypeStruct((n_seq, 2, D_MODEL), F32)],
        compiler_params=_cparams(("parallel", "parallel")),
        name="rglru",
    )(proj, proj, conv_w, conv_b, gate_w, gate_b, lam, s0)


def _dn_kernel(q_ref, k_ref, v_ref, z_ref, sm_ref, cwq_ref, cwk_ref, cwv_ref, par_ref, ng_ref, s0_ref,
               o_ref, so_ref, q_scr, k_scr, v_scr, c_scr, b_scr, of_scr, ob_scr, *, seq_len, hg):
    n = seq_len
    cc = DN_CHUNK
    sc = DN_SUPER
    n_super = n // sc
    head0 = pl.program_id(1) * hg
    roww = lax.broadcasted_iota(jnp.int32, (n, hg * HEAD_DIM), 0)
    row = lax.broadcasted_iota(jnp.int32, (n, HEAD_DIM), 0)
    lane = lax.broadcasted_iota(jnp.int32, (n, HEAD_DIM), 1)
    pos = row & (cc - 1)

    xq = _silu(_dw_conv(q_ref[...], cwq_ref[...], roww))
    xk = _silu(_dw_conv(k_ref[...], cwk_ref[...], roww))
    v_scr[...] = _silu(_dw_conv(v_ref[...], cwv_ref[...], roww))
    small = sm_ref[...]
    par = par_ref[...]
    g_all = -jnp.exp(par[0:1, :]) * _softplus(small + par[1:2, :])
    beta_all = jax.nn.sigmoid(small)
    cum_all = [g_all, g_all]
    step = 1
    while step < cc:
        cum_all[0] = cum_all[0] + jnp.where(pos >= step, pltpu.roll(cum_all[0], step, 0), 0.0)
        cum_all[1] = cum_all[1] + jnp.where(pos < cc - step, pltpu.roll(cum_all[1], n - step, 0), 0.0)
        step *= 2
    for hh in range(hg):
        cs = slice(hh * HEAD_DIM, (hh + 1) * HEAD_DIM)
        xqh, xkh = xq[:, cs], xk[:, cs]
        q_scr[:, cs] = xqh * lax.rsqrt(jnp.sum(xqh * xqh, axis=-1, keepdims=True) + EPS) * (HEAD_DIM ** -0.5)
        k_scr[:, cs] = xkh * lax.rsqrt(jnp.sum(xkh * xkh, axis=-1, keepdims=True) + EPS)
        head = head0 + hh
        for d in range(2):
            a_lane = lane == 2 * N_HEADS * d + head
            b_lane = lane == 2 * N_HEADS * d + N_HEADS + head
            cum = jnp.sum(jnp.where(a_lane, cum_all[d], 0.0), axis=-1, keepdims=True)
            beta = jnp.sum(jnp.where(b_lane, beta_all, 0.0), axis=-1, keepdims=True)
            c_scr[d, hh] = jnp.broadcast_to(cum, (n, HEAD_DIM))
            b_scr[d, hh] = jnp.broadcast_to(beta, (n, HEAD_DIM))

    ri = lax.broadcasted_iota(jnp.int32, (sc, sc), 0)
    cj = lax.broadcasted_iota(jnp.int32, (sc, sc), 1)
    sh = cc.bit_length() - 1
    same = (ri >> sh) == (cj >> sh)
    incl = (same & (ri >= cj), same & (ri <= cj))
    strict = (same & (ri > cj), same & (ri < cj))
    eye = jnp.where(ri == cj, 1.0, 0.0)
    level = [(ri >> 3) == (cj >> 3)]
    for b in range(4, sh + 1):
        level.append(((ri >> b) == (cj >> b)) & ((ri >> (b - 1)) != (cj >> (b - 1))))

    n_ch = sc // cc

    def super_chunks(chains):
        idx = range(len(chains))
        dd = [c[0] for c in chains]
        sl = [pl.ds(c[2], sc) for c in chains]
        cs = [slice(c[1] * HEAD_DIM, (c[1] + 1) * HEAD_DIM) for c in chains]
        q = [q_scr[sl[c], cs[c]] for c in idx]
        k = [k_scr[sl[c], cs[c]] for c in idx]
        v = [v_scr[sl[c], cs[c]] for c in idx]
        cum = [c_scr[dd[c], chains[c][1], sl[c], :] for c in idx]
        beta = [b_scr[dd[c], chains[c][1], sl[c], :] for c in idx]
        kk = [_mm_nt(k[c], k[c]) for c in idx]
        qk = [_mm_nt(q[c], k[c]) for c in idx]
        decay, x, attn = [], [], []
        for c in idx:
            cb = jnp.concatenate([cum[c], cum[c]], axis=1)
            diff = cb - cb.T
            dec = jnp.where(incl[dd[c]], jnp.exp(jnp.where(incl[dd[c]], diff, 0.0)), 0.0)
            bb = jnp.concatenate([beta[c], beta[c]], axis=1)
            x.append(-jnp.where(strict[dd[c]], kk[c] * bb * dec, 0.0))
            attn.append(qk[c] * dec)
        xp = [jnp.where(level[0], x[c], 0.0) for c in idx]
        p = [eye + xp[c] for c in idx]
        for _ in range(2):
            xp = [_mm(xp[c], xp[c]) for c in idx]
            pm = [_mm(p[c], xp[c]) for c in idx]
            p = [p[c] + pm[c] for c in idx]
        for lv in range(1, len(level)):
            t1 = [_mm(jnp.where(level[lv], x[c], 0.0), p[c]) for c in idx]
            t2 = [_mm(p[c], t1[c]) for c in idx]
            p = [p[c] + t2[c] for c in idx]
        rhs = [jnp.concatenate([v[c] * beta[c], k[c] * beta[c] * jnp.exp(cum[c])], axis=1) for c in idx]
        sol = [_mm(p[c], rhs[c]) for c in idx]
        qd = [q[c] * jnp.exp(cum[c]) for c in idx]
        s = [so_ref[dd[c], chains[c][1]] for c in idx]
        v_new = [[None] * n_ch for _ in idx]
        o_inter = [[None] * n_ch for _ in idx]
        for stp in range(n_ch):
            ch = [stp if dd[c] == 0 else n_ch - 1 - stp for c in idx]
            rs = [slice(ch[c] * cc, (ch[c] + 1) * cc) for c in idx]
            ws = [_mm(sol[c][rs[c], HEAD_DIM:], s[c]) for c in idx]
            for c in idx:
                o_inter[c][ch[c]] = _mm(qd[c][rs[c], :], s[c])
            kt = []
            for c in idx:
                cum_c = cum[c][rs[c], :]
                tot = cum_c[cc - 1:cc, :] if dd[c] == 0 else cum_c[0:1, :]
                v_new[c][ch[c]] = sol[c][rs[c], :HEAD_DIM] - ws[c]
                kt.append((k[c][rs[c], :] * jnp.exp(tot - cum_c), jnp.exp(tot)))
            upd = [_mm_tn(kt[c][0], v_new[c][ch[c]]) for c in idx]
            s = [s[c] * kt[c][1] + upd[c] for c in idx]
        oi = [_mm(attn[c], jnp.concatenate(v_new[c], axis=0)) for c in idx]
        for c in idx:
            o = jnp.concatenate(o_inter[c], axis=0) + oi[c]
            if dd[c] == 0:
                of_scr[sl[c], cs[c]] = o
            else:
                ob_scr[sl[c], cs[c]] = o
            so_ref[dd[c], chains[c][1]] = s[c]

    so_ref[...] = s0_ref[...]

    def body(i, carry):
        fwd = pl.multiple_of(i * sc, sc)
        bwd = pl.multiple_of((n_super - 1 - i) * sc, sc)
        super_chunks([(d, hh, fwd if d == 0 else bwd) for hh in range(hg) for d in range(2)])
        return carry

    lax.fori_loop(0, n_super, body, 0)
    tot = of_scr[...] + ob_scr[...]
    z = z_ref[...]
    for hh in range(hg):
        cs = slice(hh * HEAD_DIM, (hh + 1) * HEAD_DIM)
        o_ref[:, cs] = _rms(tot[:, cs]) * ng_ref[hh] * _silu(z[:, cs])


def _state_spec(s0, layer, hg):
    if layer is None:
        return pl.BlockSpec((None, 2, hg, HEAD_DIM, HEAD_DIM), lambda b, g: (b, 0, g, 0, 0))
    return pl.BlockSpec((None, None, 2, hg, HEAD_DIM, HEAD_DIM), lambda b, g: (b, layer, 0, g, 0, 0))


def _deltanet(proj, conv_w, par, norm_g, s0, layer, n_seq, seq_len, row_off, hg):
    wide = hg * HEAD_DIM

    def col(c0):
        return pl.BlockSpec((seq_len, wide), lambda b, g: (row_off + b, c0 // hg + g))

    def cw(c0):
        return pl.BlockSpec((4, wide), lambda b, g: (0, c0 // hg + g))

    vm = functools.partial(pltpu.VMEM, dtype=F32)
    return pl.pallas_call(
        functools.partial(_dn_kernel, seq_len=seq_len, hg=hg),
        grid=(n_seq, N_HEADS // hg),
        in_specs=[col(COL_DQ), col(COL_DK), col(COL_DV), col(COL_DZ),
                  pl.BlockSpec((seq_len, HEAD_DIM), lambda b, g: (row_off + b, COL_SMALL)),
                  cw(0), cw(N_HEADS), cw(2 * N_HEADS),
                  pl.BlockSpec((2, HEAD_DIM), lambda b, g: (0, 0)),
                  pl.BlockSpec((hg, 1, HEAD_DIM), lambda b, g: (g, 0, 0)),
                  _state_spec(s0, layer, hg)],
        out_specs=[pl.BlockSpec((seq_len, wide), lambda b, g: (b, g)),
                   pl.BlockSpec((None, 2, hg, HEAD_DIM, HEAD_DIM), lambda b, g: (b, 0, g, 0, 0))],
        out_shape=[jax.ShapeDtypeStruct((n_seq * seq_len, D_MODEL), F32),
                   jax.ShapeDtypeStruct((n_seq, 2, N_HEADS, HEAD_DIM, HEAD_DIM), F32)],
        scratch_shapes=[vm((seq_len, wide)), vm((seq_len, wide)), vm((seq_len, wide)),
                        vm((2, hg, seq_len, HEAD_DIM)), vm((2, hg, seq_len, HEAD_DIM)),
                        vm((seq_len, wide)), vm((seq_len, wide))],
        compiler_params=_cparams(("parallel", "parallel"),
                                 VMEM_LIMIT_WIDE if seq_len * hg > 2048 else VMEM_LIMIT),
        name="deltanet",
    )(proj, proj, proj, proj, proj, conv_w, conv_w, conv_w, par, norm_g, s0)


def _merge_kernel(retc_ref, retl_ref, lruc_ref, lrul_ref, dnc_ref, dnl_ref, g0_ref, g1_ref, g2_ref, x_ref, mod_ref,
                  ng_ref, wbr_ref, wout_ref, xo_ref, h_ref, *, ctx_blocks):
    is_ctx = pl.program_id(0) < ctx_blocks
    ret = jnp.where(is_ctx, retc_ref[...], retl_ref[...])
    lru = jnp.where(is_ctx, lruc_ref[...], lrul_ref[...])
    dn = jnp.where(is_ctx, dnc_ref[...], dnl_ref[...])
    merged = jax.nn.sigmoid(g0_ref[...]) * _mm(ret, wbr_ref[0])
    merged = merged + jax.nn.sigmoid(g1_ref[...]) * _mm(lru, wbr_ref[1])
    merged = merged + jax.nn.sigmoid(g2_ref[...]) * _mm(dn, wbr_ref[2])
    x = x_ref[...] + mod_ref[2:3, :] * _mm(merged, wout_ref[...])
    xo_ref[...] = x
    h_ref[...] = (_rms(x) * ng_ref[...] * (1.0 + mod_ref[4:5, :]) + mod_ref[3:4, :]).T.astype(BF16)


def _merge(branches, proj, x, mod, norm_g, w_br, w_out, layer, mod_row, tm):
    t = x.shape[0]
    ctx_blocks = branches[0][0].shape[0] // tm
    row = pl.BlockSpec((tm, D_MODEL), lambda i: (i, 0))
    row_c = pl.BlockSpec((tm, D_MODEL), lambda i: (jnp.minimum(i, ctx_blocks - 1), 0))
    row_l = pl.BlockSpec((tm, D_MODEL), lambda i: (jnp.maximum(i - ctx_blocks, 0), 0))
    col0 = COL_BGATE * HEAD_DIM // D_MODEL

    def gate(k):
        return pl.BlockSpec((tm, D_MODEL), lambda i: (i, col0 + k))

    return pl.pallas_call(
        functools.partial(_merge_kernel, ctx_blocks=ctx_blocks),
        grid=(t // tm,),
        in_specs=[row_c, row_l, row_c, row_l, row_c, row_l, gate(0), gate(1), gate(2), row,
                  pl.BlockSpec((None, None, N_MOD, D_MODEL), lambda i: (layer, mod_row(i), 0, 0)),
                  pl.BlockSpec((1, D_MODEL), lambda i: (0, 0)),
                  pl.BlockSpec((None, 3, D_MODEL, D_MODEL), lambda i: (layer, 0, 0, 0)),
                  pl.BlockSpec((None, D_MODEL, D_MODEL), lambda i: (layer, 0, 0))],
        out_specs=[row, pl.BlockSpec((D_MODEL, tm), lambda i: (0, i))],
        out_shape=[jax.ShapeDtypeStruct((t, D_MODEL), F32), jax.ShapeDtypeStruct((D_MODEL, t), BF16)],
        compiler_params=_cparams(("parallel",)),
        name="merge",
    )(*branches[0], *branches[1], *branches[2], proj, proj, proj, x, mod, norm_g, w_br, w_out)


def _router_kernel(h_ref, wq_ref, keys_ref, thr_ref, e0_ref, e1_ref, q_scr, top_scr, cand_scr):
    tb = h_ref.shape[1]
    q_scr[...] = jnp.dot(wq_ref[...], h_ref[...], preferred_element_type=F32)

    top_scr[...] = jnp.full(top_scr.shape, NEG_BIG, F32)
    sub_row = lax.broadcasted_iota(jnp.int32, (SUBLANES, tb), 0)

    def top_values(x, dst):
        for kk in range(TOP_N):
            m = jnp.max(x, axis=0, keepdims=True)
            top_scr[dst, kk:kk + 1, :] = m
            x = jnp.where(x >= m, NEG_BIG, x)

    def body(hd, carry):
        base = pl.multiple_of(hd * 2 * N_KEYS, 2 * N_KEYS)
        s0 = _mm(keys_ref[0], q_scr[pl.ds(base, N_KEYS), :])
        s1 = _mm(keys_ref[1], q_scr[pl.ds(base + N_KEYS, N_KEYS), :])
        top_values(s0, 0)
        top_values(s1, 1)
        cand_scr[0:TOP_PAD, :] = top_scr[0, 0:1, :] + top_scr[1]
        a1 = top_scr[1, 0:SUBLANES, :]
        for p in range(1, N_MULTI):
            keep = sub_row < TOP_N // (p + 1)
            cand_scr[TOP_PAD + (p - 1) * SUBLANES:TOP_PAD + p * SUBLANES, :] = jnp.where(
                keep, top_scr[0, p:p + 1, :] + a1, NEG_BIG)
        cand_scr[CAND_ROWS - SUBLANES:CAND_ROWS, :] = top_scr[0, N_MULTI:TOP_N, :] + top_scr[1, 0:1, :]
        x = cand_scr[...]
        m0 = jnp.max(x, axis=0, keepdims=True)
        z = jnp.zeros_like(m0)
        m = m0
        for kk in range(PEER_TOPK):
            if kk > 0:
                x = jnp.where(x >= m, NEG_BIG, x)
                m = jnp.max(x, axis=0, keepdims=True)
            z = z + jnp.exp(m - m0)
        m_next = jnp.max(jnp.where(x >= m, NEG_BIG, x), axis=0, keepdims=True)
        tau = 0.5 * (m + m_next)
        max1 = top_scr[1, 0:1, :]
        thr = jnp.exp(tau - s0 - max1) / z
        e0 = 0.5 * jnp.exp(s0 - top_scr[0, 0:1, :])
        e1 = jnp.exp(s1 - max1) / z
        for c in range(tb // HEAD_DIM):
            ls = slice(c * HEAD_DIM, (c + 1) * HEAD_DIM)
            thr_ref[hd, c] = thr[:, ls]
            e0_ref[hd, c] = e0[:, ls]
            e1_ref[hd, c] = e1[:, ls]
        return carry

    lax.fori_loop(0, N_HEADS, body, 0)


def _router(h2, wq_t, keys, layer, tb):
    t = h2.shape[1]
    out = pl.BlockSpec((N_HEADS, tb // HEAD_DIM, N_KEYS, HEAD_DIM), lambda i: (0, i, 0, 0))
    shp = jax.ShapeDtypeStruct((N_HEADS, t // HEAD_DIM, N_KEYS, HEAD_DIM), F32)
    return pl.pallas_call(
        _router_kernel,
        grid=(t // tb,),
        in_specs=[pl.BlockSpec((D_MODEL, tb), lambda i: (0, i)),
                  pl.BlockSpec((None, 2 * N_KEYS * N_HEADS, D_MODEL), lambda i: (layer, 0, 0)),
                  pl.BlockSpec((None, 2, N_KEYS, N_KEYS), lambda i: (layer, 0, 0, 0))],
        out_specs=[out, out, out],
        out_shape=[shp, shp, shp],
        scratch_shapes=[pltpu.VMEM((2 * N_KEYS * N_HEADS, tb), F32),
                        pltpu.VMEM((2, TOP_PAD, tb), F32),
                        pltpu.VMEM((CAND_ROWS, tb), F32)],
        compiler_params=_cparams(("parallel",)),
        name="peer_router",
    )(h2, wq_t, keys)


def _expert_kernel(h_ref, u0_ref, un_ref, vt_ref, thr_ref, e0_ref, e1_ref, x_ref, mod_ref, o_ref,
                   acc_scr, g_scr, act_a, act_b, *, tile_e):
    act_scr = (act_a, act_b)
    j = pl.program_id(1)
    tb = h_ref.shape[1]
    n_sub = tile_e // N_KEYS

    n_lane = tb // HEAD_DIM

    def activations(u_ref, dst):
        act = jnp.dot(u_ref[...], h_ref[...], preferred_element_type=F32)
        for c in range(n_lane):
            dst[c] = act[:, c * HEAD_DIM:(c + 1) * HEAD_DIM]

    @pl.when(j == 0)
    def _():
        acc_scr[...] = jnp.zeros_like(acc_scr)
        activations(u0_ref, act_scr[0])

    i0 = pl.multiple_of(j * n_sub, n_sub)
    n_part = tile_e // EXPERT_PART
    keys_per_part = EXPERT_PART // N_KEYS

    def step(cur, nxt):
        activations(un_ref, act_scr[nxt])
        acc = acc_scr[...]
        for part in range(n_part):
            for c in range(n_lane):
                thr = [thr_ref[hd, c, pl.ds(i0, n_sub), :] for hd in range(N_HEADS)]
                e0 = [e0_ref[hd, c, pl.ds(i0, n_sub), :] for hd in range(N_HEADS)]
                for r in range(keys_per_part):
                    ii = part * keys_per_part + r
                    rs = slice(ii * N_KEYS, (ii + 1) * N_KEYS)
                    wd = None
                    for hd in range(N_HEADS):
                        e1 = e1_ref[hd, c]
                        w = jnp.where(e1 >= thr[hd][ii:ii + 1, :], e1 * e0[hd][ii:ii + 1, :], 0.0)
                        wd = w if wd is None else wd + w
                    x = act_scr[cur][c, rs, :]
                    g_scr[c, rs, :] = ((x * wd) * (1.0 + lax.erf(x * math.sqrt(0.5)))).astype(BF16)
            rows = slice(part * EXPERT_PART, (part + 1) * EXPERT_PART)
            g_part = jnp.concatenate([g_scr[c, rows, :] for c in range(n_lane)], axis=1)
            acc = acc + jnp.dot(vt_ref[:, rows], g_part, preferred_element_type=F32)
        acc_scr[...] = acc

    slot = lax.rem(j, 2)

    @pl.when(slot == 0)
    def _():
        step(0, 1)

    @pl.when(slot == 1)
    def _():
        step(1, 0)

    @pl.when(j == pl.num_programs(1) - 1)
    def _():
        o_ref[...] = x_ref[...] + mod_ref[5:6, :] * acc_scr[...].T


def _experts(h2t, u_tab, vt_tab, routing, x, mod, layer, mod_row, tb, tile_e):
    t = h2t.shape[1]
    n_exp = u_tab.shape[1]
    n_lane = tb // HEAD_DIM
    rt = pl.BlockSpec((N_HEADS, n_lane, N_KEYS, HEAD_DIM), lambda i, j: (0, i, 0, 0))
    n_tiles = n_exp // tile_e
    return pl.pallas_call(
        functools.partial(_expert_kernel, tile_e=tile_e),
        grid=(t // tb, n_tiles),
        in_specs=[pl.BlockSpec((D_MODEL, tb), lambda i, j: (0, i)),
                  pl.BlockSpec((None, tile_e, D_MODEL), lambda i, j: (layer, 0, 0)),
                  pl.BlockSpec((None, tile_e, D_MODEL), lambda i, j: (layer, jnp.minimum(j + 1, n_tiles - 1), 0)),
                  pl.BlockSpec((None, None, D_MODEL, tile_e), lambda i, j: (layer, j, 0, 0)),
                  rt, rt, rt,
                  pl.BlockSpec((tb, D_MODEL), lambda i, j: (i, 0)),
                  pl.BlockSpec((None, None, N_MOD, D_MODEL), lambda i, j: (layer, mod_row(i), 0, 0))],
        out_specs=pl.BlockSpec((tb, D_MODEL), lambda i, j: (i, 0)),
        out_shape=jax.ShapeDtypeStruct((t, D_MODEL), F32),
        scratch_shapes=[pltpu.VMEM((D_MODEL, tb), F32), pltpu.VMEM((n_lane, tile_e, HEAD_DIM), BF16),
                        pltpu.VMEM((n_lane, tile_e, HEAD_DIM), F32), pltpu.VMEM((n_lane, tile_e, HEAD_DIM), F32)],
        compiler_params=_cparams(("parallel", "arbitrary")),
        name="peer_experts",
    )(h2t, u_tab, u_tab, vt_tab, *routing, x, mod)


def _final_norm_kernel(x_ref, g_ref, o_ref):
    o_ref[...] = _rms(x_ref[...]) * g_ref[...]


def _final_norm(x, g, tm):
    t = x.shape[0]
    row = pl.BlockSpec((tm, D_MODEL), lambda i: (i, 0))
    return pl.pallas_call(
        _final_norm_kernel, grid=(t // tm,),
        in_specs=[row, pl.BlockSpec((1, D_MODEL), lambda i: (0, 0))],
        out_specs=row, out_shape=jax.ShapeDtypeStruct((t, D_MODEL), F32),
        compiler_params=_cparams(("parallel",)), name="final_norm",
    )(x, g)


def _rope_tables(seq_len):
    rows = seq_len // GRID_W
    r = jnp.repeat(jnp.arange(rows, dtype=F32), GRID_W)
    col = jnp.tile(jnp.arange(GRID_W, dtype=F32), rows)
    nf = HEAD_DIM // 4
    inv = ROPE_BASE ** (-jnp.arange(nf, dtype=F32) / nf)
    ang = jnp.concatenate([r[:, None] * inv, col[:, None] * inv], axis=-1)
    cos, sin = jnp.cos(ang), jnp.sin(ang)
    return jnp.concatenate([cos, cos], axis=-1), jnp.concatenate([-sin, sin], axis=-1)


def _lanes(a):
    return jnp.broadcast_to(jnp.moveaxis(a, -1, 0)[..., None], (a.shape[-1],) + a.shape[:-1] + (HEAD_DIM,))


def kernel(x_prompt, x_sample, c, state_ret, state_lru, state_dn, c_ctx, w_mod, b_mod, norm1_g, norm2_g, w_in, ret_gamma_logit, ret_norm_g, lru_conv_w, lru_conv_b, lru_gate_w, lru_gate_b, lru_lambda, dn_conv_w, dn_a_log, dn_dt_bias, dn_norm_g, w_br, w_out, peer_w_q, peer_sub_keys, peer_u, peer_v, final_norm_g):
    n_ctx, l_ctx, _ = x_prompt.shape
    n_lat, l_lat, _ = x_sample.shape
    t_ctx, t_lat = n_ctx * l_ctx, n_lat * l_lat
    assert t_ctx % l_lat == 0
    tb = TOKEN_BLOCK
    assert l_lat % tb == 0 and t_ctx % tb == 0
    ctx_blocks, per_seq = t_ctx // tb, l_lat // tb

    def mod_row(i):
        return jnp.where(i < ctx_blocks, 0, 1 + (i - ctx_blocks) // per_seq)

    tm = MERGE_ROWS
    ctx_blocks_m, per_seq_m = t_ctx // tm, l_lat // tm

    def mod_row_m(i):
        return jnp.where(i < ctx_blocks_m, 0, 1 + (i - ctx_blocks_m) // per_seq_m)

    x = jnp.concatenate([x_prompt.reshape(t_ctx, D_MODEL), x_sample.reshape(t_lat, D_MODEL)], axis=0)
    n_cond = 16
    cond = jnp.zeros((n_cond, D_MODEL), F32).at[0].set(c_ctx).at[1:1 + n_lat].set(c)
    mods = _modulation(cond, w_mod, b_mod).reshape(DEPTH, n_cond, N_MOD, D_MODEL)

    w_in_b = w_in.astype(BF16)
    w_in_r = jnp.concatenate(
        [w_in_b[:, :, :N_MAIN], w_in_b[:, :, N_MAIN + N_SMALL:], w_in_b[:, :, N_MAIN:N_MAIN + N_SMALL],
         jnp.zeros((DEPTH, D_MODEL, HEAD_DIM - N_SMALL), BF16)], axis=-1)
    w_br_b, w_out_b = w_br.astype(BF16), w_out.astype(BF16)
    wq_t = jnp.swapaxes(peer_w_q, 1, 2).astype(BF16)
    keys_b = peer_sub_keys.astype(BF16)
    u_b = peer_u.astype(BF16)
    vt_b = jnp.swapaxes(peer_v.reshape(DEPTH, N_EXPERTS // EXPERT_TILE, EXPERT_TILE, D_MODEL), 2, 3).astype(BF16)
    gam = _lanes(ret_gamma_logit)
    lane_pad = jnp.zeros((DEPTH, 2, N_HEADS), F32)
    dn_ab = jnp.stack([dn_a_log, dn_dt_bias], axis=1)
    dn_par = jnp.concatenate([dn_ab[:, :, 0], lane_pad, dn_ab[:, :, 1], lane_pad,
                              jnp.zeros((DEPTH, 2, HEAD_DIM - N_SMALL), F32)], axis=-1)
    rope_tabs = _rope_tables(l_lat)
    zero_ret = jnp.zeros((n_ctx, 2, N_HEADS, HEAD_DIM, HEAD_DIM), F32)
    zero_lru = jnp.zeros((n_ctx, 2, D_MODEL), F32)

    row_off_lat = t_ctx // l_lat

    ret_states, lru_states, dn_states = [], [], []
    for l in range(DEPTH):
        proj = _in_proj(x, mods, norm1_g[l][None], w_in_r, l,
                        lambda i: jnp.where(i < row_off_lat, 0, 1 + i - row_off_lat), l_lat)

        ng_ret = ret_norm_g[l][:, None, :]
        o_ret_c, s_ret = _retention(proj, gam[:, l], ng_ret, zero_ret, None, None, n_ctx, l_ctx, 0, MIXER_HEADS)
        o_ret_l, _ = _retention(proj, gam[:, l], ng_ret, state_ret, l, rope_tabs, n_lat, l_lat, row_off_lat, MIXER_HEADS)

        lru_args = (lru_conv_w[l], lru_conv_b[l][None], lru_gate_w[l], lru_gate_b[l], lru_lambda[l])
        o_lru_c, s_lru = _rglru(proj, *lru_args, zero_lru, None, n_ctx, l_ctx, 0, _lru_blocks(l_ctx))
        o_lru_l, _ = _rglru(proj, *lru_args, state_lru, l, n_lat, l_lat, row_off_lat, _lru_blocks(l_lat))

        ng_dn = dn_norm_g[l][:, None, :]
        o_dn_c, s_dn = _deltanet(proj, dn_conv_w[l], dn_par[l], ng_dn, zero_ret, None, n_ctx, l_ctx, 0, MIXER_HEADS)
        o_dn_l, _ = _deltanet(proj, dn_conv_w[l], dn_par[l], ng_dn, state_dn, l, n_lat, l_lat, row_off_lat, MIXER_HEADS)

        branches = ((o_ret_c, o_ret_l), (o_lru_c, o_lru_l), (o_dn_c, o_dn_l))
        x, h2 = _merge(branches, proj, x, mods, norm2_g[l][None], w_br_b, w_out_b, l, mod_row_m, tm)
        routing = _router(h2, wq_t, keys_b, l, tb)
        x = _experts(h2, u_b, vt_b, routing, x, mods, l, mod_row, tb, EXPERT_TILE)

        ret_states.append(s_ret)
        lru_states.append(s_lru)
        dn_states.append(s_dn)

    y = _final_norm(x, final_norm_g[None], tb)
    y_prompt = y[:t_ctx].reshape(n_ctx, l_ctx, D_MODEL)
    y_sample = y[t_ctx:].reshape(n_lat, l_lat, D_MODEL)
    return (y_prompt, y_sample, jnp.stack(ret_states, axis=1), jnp.stack(lru_states, axis=1),
            jnp.stack(dn_states, axis=1))
```

```python
import functools
import math

import jax
import jax.numpy as jnp
from jax import lax
from jax.experimental import pallas as pl
from jax.experimental.pallas import tpu as pltpu

F32 = jnp.float32
BF16 = jnp.bfloat16

D_MODEL = 1024
DEPTH = 4
N_MOD = 6
EPS = 1e-6
GRID_W = 64
ROPE_BASE = 10000.0
N_HEADS = 8
HEAD_DIM = 128
RET_CHUNK = 128
DN_CHUNK = 64
DN_SUPER = 256
LRU_C = 8.0
N_KEYS = 128
PEER_TOPK = 16
N_EXPERTS = N_KEYS * N_KEYS
SUBLANES = 8
TOP_N = PEER_TOPK + 1
TOP_PAD = -(-TOP_N // SUBLANES) * SUBLANES
N_MULTI = TOP_N - SUBLANES
CAND_ROWS = TOP_PAD + N_MULTI * SUBLANES
assert TOP_N // 2 <= SUBLANES and TOP_N // (N_MULTI + 1) == 1
EXPERT_TILE = 2048
NEG_BIG = -3.0e38

COL_RQ, COL_RK, COL_RV, COL_RG = 0, 8, 16, 24
COL_LX, COL_LG = 32, 40
COL_DQ, COL_DK, COL_DV, COL_DZ = 48, 56, 64, 72
N_MAIN = 80 * 128
N_SMALL = 4 * N_HEADS
COL_BGATE = 80
COL_SMALL = 104
N_PROJ = 105 * 128

TOKEN_BLOCK = 512
MERGE_ROWS = 256
IN_PROJ_COLS = 15 * 128
MOD_COLS = 1536
MIXER_HEADS = 4
LRU_STEP_ROWS = 2048
VMEM_LIMIT = 48 * 1024 * 1024
VMEM_LIMIT_WIDE = 58 * 1024 * 1024


def _cparams(sem, vmem_limit=VMEM_LIMIT):
    return pltpu.CompilerParams(dimension_semantics=sem, vmem_limit_bytes=vmem_limit)


def _mm(a, b):
    return jnp.dot(a.astype(BF16), b.astype(BF16), preferred_element_type=F32)


def _mm_nt(a, b):
    return lax.dot_general(a.astype(BF16), b.astype(BF16), (((1,), (1,)), ((), ())),
                           preferred_element_type=F32)


def _mm_tn(a, b):
    return lax.dot_general(a.astype(BF16), b.astype(BF16), (((0,), (0,)), ((), ())),
                           preferred_element_type=F32)


def _softplus(x):
    return jnp.maximum(x, 0.0) + jnp.log1p(jnp.exp(-jnp.abs(x)))


def _silu(x):
    return x * jax.nn.sigmoid(x)


def _gelu(x):
    return 0.5 * x * (1.0 + lax.erf(x * math.sqrt(0.5)))


def _rms(x):
    return x * lax.rsqrt(jnp.mean(x * x, axis=-1, keepdims=True) + EPS)


def _shift_rows(x, s, row):
    n = x.shape[0]
    if s == 0:
        return x
    y = pltpu.roll(x, (-s) % n, 0)
    ok = (row + s >= 0) & (row + s < n)
    return jnp.where(ok, y, 0.0)


def _dw_conv(x, w, row):
    y = _shift_rows(x, -2, row) * w[0:1, :]
    y = y + _shift_rows(x, -1, row) * w[1:2, :]
    y = y + x * w[2:3, :]
    y = y + _shift_rows(x, 1, row) * w[3:4, :]
    return y


def _mod_kernel(c_ref, w_ref, b_ref, o_ref):
    c = c_ref[...]
    o_ref[...] = jnp.dot(_silu(c), w_ref[...], precision=lax.Precision.HIGHEST,
                         preferred_element_type=F32) + b_ref[...]


def _modulation(cond, w_mod, b_mod):
    n_rows = cond.shape[0]
    tn = MOD_COLS
    n_out = N_MOD * D_MODEL
    return pl.pallas_call(
        _mod_kernel,
        grid=(DEPTH, n_out // tn),
        in_specs=[pl.BlockSpec((n_rows, D_MODEL), lambda l, j: (0, 0)),
                  pl.BlockSpec((None, D_MODEL, tn), lambda l, j: (l, 0, j)),
                  pl.BlockSpec((None, 1, tn), lambda l, j: (l, 0, j))],
        out_specs=pl.BlockSpec((None, n_rows, tn), lambda l, j: (l, 0, j)),
        out_shape=jax.ShapeDtypeStruct((DEPTH, n_rows, n_out), F32),
        compiler_params=_cparams(("parallel", "parallel")),
        name="modulation",
    )(cond, w_mod, b_mod.reshape(DEPTH, 1, n_out))


def _in_proj_kernel(x_ref, mod_ref, g_ref, w_ref, o_ref, h_scr):
    @pl.when(pl.program_id(1) == 0)
    def _():
        y = _rms(x_ref[...]) * g_ref[...]
        h_scr[...] = (y * (1.0 + mod_ref[1:2, :]) + mod_ref[0:1, :]).astype(BF16)

    o_ref[...] = jnp.dot(h_scr[...], w_ref[...], preferred_element_type=F32)


def _in_proj(x, mod, norm_g, w, layer, mod_row, tm):
    t = x.shape[0]
    tn = IN_PROJ_COLS
    return pl.pallas_call(
        _in_proj_kernel,
        grid=(t // tm, N_PROJ // tn),
        in_specs=[pl.BlockSpec((tm, D_MODEL), lambda i, j: (i, 0)),
                  pl.BlockSpec((None, None, N_MOD, D_MODEL), lambda i, j: (layer, mod_row(i), 0, 0)),
                  pl.BlockSpec((1, D_MODEL), lambda i, j: (0, 0)),
                  pl.BlockSpec((None, D_MODEL, tn), lambda i, j: (layer, 0, j))],
        out_specs=pl.BlockSpec((tm, tn), lambda i, j: (i, j)),
        out_shape=jax.ShapeDtypeStruct((t, N_PROJ), F32),
        scratch_shapes=[pltpu.VMEM((tm, D_MODEL), BF16)],
        compiler_params=_cparams(("parallel", "arbitrary")),
        name="in_proj",
    )(x, mod, norm_g, w)


def _ret_kernel(*refs, seq_len, rope, hg):
    if rope:
        (q_ref, k_ref, v_ref, g_ref, gam_ref, ng_ref, s0_ref, cs_ref, sn_ref,
         o_ref, so_ref, of_scr, ob_scr) = refs
    else:
        (q_ref, k_ref, v_ref, g_ref, gam_ref, ng_ref, s0_ref,
         o_ref, so_ref, of_scr, ob_scr) = refs
    c = RET_CHUNK
    n_chunks = seq_len // c
    r = lax.broadcasted_iota(jnp.int32, (c, HEAD_DIM), 0).astype(F32)
    ci = lax.broadcasted_iota(jnp.int32, (c, c), 0)
    si = lax.broadcasted_iota(jnp.int32, (c, c), 1)
    dmat = (ci - si).astype(F32)
    scale = HEAD_DIM ** -0.5

    chains = [(hh, d) for hh in range(hg) for d in range(2)]
    idx = range(len(chains))
    dec, qsc, ksc, gch = [], [], [], []
    for hh, d in chains:
        lg = -_softplus(-gam_ref[hh, d:d + 1, :])
        if d == 0:
            dec.append(jnp.where(dmat >= 0, jnp.exp(lg * jnp.maximum(dmat, 0.0)), 0.0))
            qsc.append(jnp.exp(lg * (r + 1.0)))
            ksc.append(jnp.exp(lg * (c - 1.0 - r)))
        else:
            dec.append(jnp.where(dmat <= 0, jnp.exp(lg * jnp.maximum(-dmat, 0.0)), 0.0))
            qsc.append(jnp.exp(lg * (c - r)))
            ksc.append(jnp.exp(lg * r))
        gch.append(jnp.exp(lg * c))

    s = [s0_ref[d, hh] for hh, d in chains]
    for stp in range(n_chunks):
        sl = [pl.ds((stp if d == 0 else n_chunks - 1 - stp) * c, c) for _, d in chains]
        cs = [slice(hh * HEAD_DIM, (hh + 1) * HEAD_DIM) for hh, _ in chains]
        q = [q_ref[sl[i], cs[i]] for i in idx]
        k = [k_ref[sl[i], cs[i]] * scale for i in idx]
        v = [v_ref[sl[i], cs[i]] for i in idx]
        if rope:
            cos = [cs_ref[sl[i], :] for i in idx]
            sin = [sn_ref[sl[i], :] for i in idx]
            q = [q[i] * cos[i] + pltpu.roll(q[i], HEAD_DIM // 2, 1) * sin[i] for i in idx]
            k = [k[i] * cos[i] + pltpu.roll(k[i], HEAD_DIM // 2, 1) * sin[i] for i in idx]
        sc = [_mm_nt(q[i], k[i]) for i in idx]
        qs = [_mm(q[i] * qsc[i], s[i]) for i in idx]
        kv = [_mm_tn(k[i] * ksc[i], v[i]) for i in idx]
        oi = [_mm(sc[i] * dec[i], v[i]) for i in idx]
        s = [s[i] * gch[i] + kv[i] for i in idx]
        for i in idx:
            if chains[i][1] == 0:
                of_scr[sl[i], cs[i]] = oi[i] + qs[i]
            else:
                ob_scr[sl[i], cs[i]] = oi[i] + qs[i]
    for i in idx:
        so_ref[chains[i][1], chains[i][0]] = s[i]
    tot = of_scr[...] + ob_scr[...]
    g = g_ref[...]
    for hh in range(hg):
        cs1 = slice(hh * HEAD_DIM, (hh + 1) * HEAD_DIM)
        o_ref[:, cs1] = _rms(tot[:, cs1]) * ng_ref[hh] * _silu(g[:, cs1])


def _retention(proj, gam, norm_g, s0, layer, rope_tabs, n_seq, seq_len, row_off, hg):
    rope = rope_tabs is not None
    wide = hg * HEAD_DIM

    def col(c0):
        return pl.BlockSpec((seq_len, wide), lambda b, g: (row_off + b, c0 // hg + g))

    in_specs = [col(COL_RQ), col(COL_RK), col(COL_RV), col(COL_RG),
                pl.BlockSpec((hg, 2, HEAD_DIM), lambda b, g: (g, 0, 0)),
                pl.BlockSpec((hg, 1, HEAD_DIM), lambda b, g: (g, 0, 0)),
                _state_spec(s0, layer, hg)]
    args = [proj, proj, proj, proj, gam, norm_g, s0]
    if rope:
        tab = pl.BlockSpec((seq_len, HEAD_DIM), lambda b, g: (0, 0))
        in_specs += [tab, tab]
        args += list(rope_tabs)
    return pl.pallas_call(
        functools.partial(_ret_kernel, seq_len=seq_len, rope=rope, hg=hg),
        grid=(n_seq, N_HEADS // hg),
        in_specs=in_specs,
        out_specs=[pl.BlockSpec((seq_len, wide), lambda b, g: (b, g)),
                   pl.BlockSpec((None, 2, hg, HEAD_DIM, HEAD_DIM), lambda b, g: (b, 0, g, 0, 0))],
        out_shape=[jax.ShapeDtypeStruct((n_seq * seq_len, D_MODEL), F32),
                   jax.ShapeDtypeStruct((n_seq, 2, N_HEADS, HEAD_DIM, HEAD_DIM), F32)],
        scratch_shapes=[pltpu.VMEM((seq_len, wide), F32), pltpu.VMEM((seq_len, wide), F32)],
        compiler_params=_cparams(("parallel", "parallel")),
        name="retention_rope" if rope else "retention",
    )(*args)


def _lru_kernel(x_ref, gate_ref, cw_ref, cb_ref, gw_ref, gb_ref, lam_ref, s0_ref, o_ref, so_ref, *, seq_len, nb):
    n = seq_len
    row = lax.broadcasted_iota(jnp.int32, (n, nb * HEAD_DIM), 0)
    xc = _dw_conv(x_ref[...], cw_ref[...], row) + cb_ref[...]
    lam = lam_ref[...]

    def gate(d, which):
        cols = [_mm(xc[:, s * HEAD_DIM:(s + 1) * HEAD_DIM], gw_ref[d, which, s]) for s in range(nb)]
        return jax.nn.sigmoid(jnp.concatenate(cols, axis=1) + gb_ref[d, which:which + 1, :])

    hs = []
    for d in range(2):
        r_gate = gate(d, 0)
        i_gate = gate(d, 1)
        log_a = -LRU_C * r_gate * _softplus(-lam[d:d + 1, :])
        a = jnp.exp(log_a)
        u = jnp.sqrt(-jnp.tanh(log_a) * (1.0 + a * a)) * i_gate * xc
        step = 1
        while step < n:
            if d == 0:
                ok = row >= step
                sh = step
            else:
                ok = row < n - step
                sh = n - step
            a_sh = jnp.where(ok, pltpu.roll(a, sh, 0), 1.0)
            u_sh = jnp.where(ok, pltpu.roll(u, sh, 0), 0.0)
            u = a * u_sh + u
            a = a * a_sh
            step *= 2
        h = u + a * s0_ref[d:d + 1, :]
        hs.append(h)
        so_ref[d:d + 1, :] = h[n - 1:n, :] if d == 0 else h[0:1, :]
    o_ref[...] = (hs[0] + hs[1]) * _gelu(gate_ref[...])


def _lru_blocks(seq_len):
    return max(1, min(N_HEADS, LRU_STEP_ROWS // seq_len))


def _rglru(proj, conv_w, conv_b, gate_w, gate_b, lam, s0, layer, n_seq, seq_len, row_off, nb):
    wide = nb * HEAD_DIM
    if layer is None:
        s0_spec = pl.BlockSpec((None, 2, wide), lambda b, n: (b, 0, n))
    else:
        s0_spec = pl.BlockSpec((None, None, 2, wide), lambda b, n: (b, layer, 0, n))
    return pl.pallas_call(
        functools.partial(_lru_kernel, seq_len=seq_len, nb=nb),
        grid=(n_seq, N_HEADS // nb),
        in_specs=[pl.BlockSpec((seq_len, wide), lambda b, n: (row_off + b, COL_LX // nb + n)),
                  pl.BlockSpec((seq_len, wide), lambda b, n: (row_off + b, COL_LG // nb + n)),
                  pl.BlockSpec((4, wide), lambda b, n: (0, n)),
                  pl.BlockSpec((1, wide), lambda b, n: (0, n)),
                  pl.BlockSpec((2, 2, nb, HEAD_DIM, HEAD_DIM), lambda b, n: (0, 0, n, 0, 0)),
                  pl.BlockSpec((2, 2, wide), lambda b, n: (0, 0, n)),
                  pl.BlockSpec((2, wide), lambda b, n: (0, n)),
                  s0_spec],
        out_specs=[pl.BlockSpec((seq_len, wide), lambda b, n: (b, n)),
                   pl.BlockSpec((None, 2, wide), lambda b, n: (b, 0, n))],
        out_shape=[jax.ShapeDtypeStruct((n_seq * seq_len, D_MODEL), F32),
                   jax.ShapeDtypeStruct((n_seq, 2, D_MODEL), F32)],
        compiler_params=_cparams(("parallel", "parallel")),
        name="rglru",
    )(proj, proj, conv_w, conv_b, gate_w, gate_b, lam, s0)


def _dn_kernel(q_ref, k_ref, v_ref, z_ref, sm_ref, cwq_ref, cwk_ref, cwv_ref, par_ref, ng_ref, s0_ref,
               o_ref, so_ref, q_scr, k_scr, v_scr, c_scr, b_scr, of_scr, ob_scr, *, seq_len, hg):
    n = seq_len
    cc = DN_CHUNK
    sc = DN_SUPER
    n_super = n // sc
    head0 = pl.program_id(1) * hg
    roww = lax.broadcasted_iota(jnp.int32, (n, hg * HEAD_DIM), 0)
    row = lax.broadcasted_iota(jnp.int32, (n, HEAD_DIM), 0)
    lane = lax.broadcasted_iota(jnp.int32, (n, HEAD_DIM), 1)
    pos = row & (cc - 1)

    xq = _silu(_dw_conv(q_ref[...], cwq_ref[...], roww))
    xk = _silu(_dw_conv(k_ref[...], cwk_ref[...], roww))
    v_scr[...] = _silu(_dw_conv(v_ref[...], cwv_ref[...], roww))
    small = sm_ref[...]
    par = par_ref[...]
    g_all = -jnp.exp(par[0:1, :]) * _softplus(small + par[1:2, :])
    beta_all = jax.nn.sigmoid(small)
    cum_all = [g_all, g_all]
    step = 1
    while step < cc:
        cum_all[0] = cum_all[0] + jnp.where(pos >= step, pltpu.roll(cum_all[0], step, 0), 0.0)
        cum_all[1] = cum_all[1] + jnp.where(pos < cc - step, pltpu.roll(cum_all[1], n - step, 0), 0.0)
        step *= 2
    for hh in range(hg):
        cs = slice(hh * HEAD_DIM, (hh + 1) * HEAD_DIM)
        xqh, xkh = xq[:, cs], xk[:, cs]
        q_scr[:, cs] = xqh * lax.rsqrt(jnp.sum(xqh * xqh, axis=-1, keepdims=True) + EPS) * (HEAD_DIM ** -0.5)
        k_scr[:, cs] = xkh * lax.rsqrt(jnp.sum(xkh * xkh, axis=-1, keepdims=True) + EPS)
        head = head0 + hh
        for d in range(2):
            a_lane = lane == 2 * N_HEADS * d + head
            b_lane = lane == 2 * N_HEADS * d + N_HEADS + head
            cum = jnp.sum(jnp.where(a_lane, cum_all[d], 0.0), axis=-1, keepdims=True)
            beta = jnp.sum(jnp.where(b_lane, beta_all, 0.0), axis=-1, keepdims=True)
            c_scr[d, hh] = jnp.broadcast_to(cum, (n, HEAD_DIM))
            b_scr[d, hh] = jnp.broadcast_to(beta, (n, HEAD_DIM))

    ri = lax.broadcasted_iota(jnp.int32, (sc, sc), 0)
    cj = lax.broadcasted_iota(jnp.int32, (sc, sc), 1)
    sh = cc.bit_length() - 1
    same = (ri >> sh) == (cj >> sh)
    incl = (same & (ri >= cj), same & (ri <= cj))
    strict = (same & (ri > cj), same & (ri < cj))
    eye = jnp.where(ri == cj, 1.0, 0.0)
    level = [(ri >> 3) == (cj >> 3)]
    for b in range(4, sh + 1):
        level.append(((ri >> b) == (cj >> b)) & ((ri >> (b - 1)) != (cj >> (b - 1))))

    n_ch = sc // cc

    def super_chunks(chains):
        idx = range(len(chains))
        dd = [c[0] for c in chains]
        sl = [pl.ds(c[2], sc) for c in chains]
        cs = [slice(c[1] * HEAD_DIM, (c[1] + 1) * HEAD_DIM) for c in chains]
        q = [q_scr[sl[c], cs[c]] for c in idx]
        k = [k_scr[sl[c], cs[c]] for c in idx]
        v = [v_scr[sl[c], cs[c]] for c in idx]
        cum = [c_scr[dd[c], chains[c][1], sl[c], :] for c in idx]
        beta = [b_scr[dd[c], chains[c][1], sl[c], :] for c in idx]
        kk = [_mm_nt(k[c], k[c]) for c in idx]
        qk = [_mm_nt(q[c], k[c]) for c in idx]
        decay, x, attn = [], [], []
        for c in idx:
            cb = jnp.concatenate([cum[c], cum[c]], axis=1)
            diff = cb - cb.T
            dec = jnp.where(incl[dd[c]], jnp.exp(jnp.where(incl[dd[c]], diff, 0.0)), 0.0)
            bb = jnp.concatenate([beta[c], beta[c]], axis=1)
            x.append(-jnp.where(strict[dd[c]], kk[c] * bb * dec, 0.0))
            attn.append(qk[c] * dec)
        xp = [jnp.where(level[0], x[c], 0.0) for c in idx]
        p = [eye + xp[c] for c in idx]
        for _ in range(2):
            xp = [_mm(xp[c], xp[c]) for c in idx]
            pm = [_mm(p[c], xp[c]) for c in idx]
            p = [p[c] + pm[c] for c in idx]
        for lv in range(1, len(level)):
            t1 = [_mm(jnp.where(level[lv], x[c], 0.0), p[c]) for c in idx]
            t2 = [_mm(p[c], t1[c]) for c in idx]
            p = [p[c] + t2[c] for c in idx]
        rhs = [jnp.concatenate([v[c] * beta[c], k[c] * beta[c] * jnp.exp(cum[c])], axis=1) for c in idx]
        sol = [_mm(p[c], rhs[c]) for c in idx]
        qd = [q[c] * jnp.exp(cum[c]) for c in idx]
        s = [so_ref[dd[c], chains[c][1]] for c in idx]
        v_new = [[None] * n_ch for _ in idx]
        o_inter = [[None] * n_ch for _ in idx]
        for stp in range(n_ch):
            ch = [stp if dd[c] == 0 else n_ch - 1 - stp for c in idx]
            rs = [slice(ch[c] * cc, (ch[c] + 1) * cc) for c in idx]
            ws = [_mm(sol[c][rs[c], HEAD_DIM:], s[c]) for c in idx]
            for c in idx:
                o_inter[c][ch[c]] = _mm(qd[c][rs[c], :], s[c])
            kt = []
            for c in idx:
                cum_c = cum[c][rs[c], :]
                tot = cum_c[cc - 1:cc, :] if dd[c] == 0 else cum_c[0:1, :]
                v_new[c][ch[c]] = sol[c][rs[c], :HEAD_DIM] - ws[c]
                kt.append((k[c][rs[c], :] * jnp.exp(tot - cum_c), jnp.exp(tot)))
            upd = [_mm_tn(kt[c][0], v_new[c][ch[c]]) for c in idx]
            s = [s[c] * kt[c][1] + upd[c] for c in idx]
        oi = [_mm(attn[c], jnp.concatenate(v_new[c], axis=0)) for c in idx]
        for c in idx:
            o = jnp.concatenate(o_inter[c], axis=0) + oi[c]
            if dd[c] == 0:
                of_scr[sl[c], cs[c]] = o
            else:
                ob_scr[sl[c], cs[c]] = o
            so_ref[dd[c], chains[c][1]] = s[c]

    so_ref[...] = s0_ref[...]

    def body(i, carry):
        fwd = pl.multiple_of(i * sc, sc)
        bwd = pl.multiple_of((n_super - 1 - i) * sc, sc)
        super_chunks([(d, hh, fwd if d == 0 else bwd) for hh in range(hg) for d in range(2)])
        return carry

    lax.fori_loop(0, n_super, body, 0)
    tot = of_scr[...] + ob_scr[...]
    z = z_ref[...]
    for hh in range(hg):
        cs = slice(hh * HEAD_DIM, (hh + 1) * HEAD_DIM)
        o_ref[:, cs] = _rms(tot[:, cs]) * ng_ref[hh] * _silu(z[:, cs])


def _state_spec(s0, layer, hg):
    if layer is None:
        return pl.BlockSpec((None, 2, hg, HEAD_DIM, HEAD_DIM), lambda b, g: (b, 0, g, 0, 0))
    return pl.BlockSpec((None, None, 2, hg, HEAD_DIM, HEAD_DIM), lambda b, g: (b, layer, 0, g, 0, 0))


def _deltanet(proj, conv_w, par, norm_g, s0, layer, n_seq, seq_len, row_off, hg):
    wide = hg * HEAD_DIM

    def col(c0):
        return pl.BlockSpec((seq_len, wide), lambda b, g: (row_off + b, c0 // hg + g))

    def cw(c0):
        return pl.BlockSpec((4, wide), lambda b, g: (0, c0 // hg + g))

    vm = functools.partial(pltpu.VMEM, dtype=F32)
    return pl.pallas_call(
        functools.partial(_dn_kernel, seq_len=seq_len, hg=hg),
        grid=(n_seq, N_HEADS // hg),
        in_specs=[col(COL_DQ), col(COL_DK), col(COL_DV), col(COL_DZ),
                  pl.BlockSpec((seq_len, HEAD_DIM), lambda b, g: (row_off + b, COL_SMALL)),
                  cw(0), cw(N_HEADS), cw(2 * N_HEADS),
                  pl.BlockSpec((2, HEAD_DIM), lambda b, g: (0, 0)),
                  pl.BlockSpec((hg, 1, HEAD_DIM), lambda b, g: (g, 0, 0)),
                  _state_spec(s0, layer, hg)],
        out_specs=[pl.BlockSpec((seq_len, wide), lambda b, g: (b, g)),
                   pl.BlockSpec((None, 2, hg, HEAD_DIM, HEAD_DIM), lambda b, g: (b, 0, g, 0, 0))],
        out_shape=[jax.ShapeDtypeStruct((n_seq * seq_len, D_MODEL), F32),
                   jax.ShapeDtypeStruct((n_seq, 2, N_HEADS, HEAD_DIM, HEAD_DIM), F32)],
        scratch_shapes=[vm((seq_len, wide)), vm((seq_len, wide)), vm((seq_len, wide)),
                        vm((2, hg, seq_len, HEAD_DIM)), vm((2, hg, seq_len, HEAD_DIM)),
                        vm((seq_len, wide)), vm((seq_len, wide))],
        compiler_params=_cparams(("parallel", "parallel"),
                                 VMEM_LIMIT_WIDE if seq_len * hg > 2048 else VMEM_LIMIT),
        name="deltanet",
    )(proj, proj, proj, proj, proj, conv_w, conv_w, conv_w, par, norm_g, s0)


def _merge_kernel(retc_ref, retl_ref, lruc_ref, lrul_ref, dnc_ref, dnl_ref, g0_ref, g1_ref, g2_ref, x_ref, mod_ref,
                  ng_ref, wbr_ref, wout_ref, xo_ref, h_ref, *, ctx_blocks):
    is_ctx = pl.program_id(0) < ctx_blocks
    ret = jnp.where(is_ctx, retc_ref[...], retl_ref[...])
    lru = jnp.where(is_ctx, lruc_ref[...], lrul_ref[...])
    dn = jnp.where(is_ctx, dnc_ref[...], dnl_ref[...])
    merged = jax.nn.sigmoid(g0_ref[...]) * _mm(ret, wbr_ref[0])
    merged = merged + jax.nn.sigmoid(g1_ref[...]) * _mm(lru, wbr_ref[1])
    merged = merged + jax.nn.sigmoid(g2_ref[...]) * _mm(dn, wbr_ref[2])
    x = x_ref[...] + mod_ref[2:3, :] * _mm(merged, wout_ref[...])
    xo_ref[...] = x
    h_ref[...] = (_rms(x) * ng_ref[...] * (1.0 + mod_ref[4:5, :]) + mod_ref[3:4, :]).T.astype(BF16)


def _merge(branches, proj, x, mod, norm_g, w_br, w_out, layer, mod_row, tm):
    t = x.shape[0]
    ctx_blocks = branches[0][0].shape[0] // tm
    row = pl.BlockSpec((tm, D_MODEL), lambda i: (i, 0))
    row_c = pl.BlockSpec((tm, D_MODEL), lambda i: (jnp.minimum(i, ctx_blocks - 1), 0))
    row_l = pl.BlockSpec((tm, D_MODEL), lambda i: (jnp.maximum(i - ctx_blocks, 0), 0))
    col0 = COL_BGATE * HEAD_DIM // D_MODEL

    def gate(k):
        return pl.BlockSpec((tm, D_MODEL), lambda i: (i, col0 + k))

    return pl.pallas_call(
        functools.partial(_merge_kernel, ctx_blocks=ctx_blocks),
        grid=(t // tm,),
        in_specs=[row_c, row_l, row_c, row_l, row_c, row_l, gate(0), gate(1), gate(2), row,
                  pl.BlockSpec((None, None, N_MOD, D_MODEL), lambda i: (layer, mod_row(i), 0, 0)),
                  pl.BlockSpec((1, D_MODEL), lambda i: (0, 0)),
                  pl.BlockSpec((None, 3, D_MODEL, D_MODEL), lambda i: (layer, 0, 0, 0)),
                  pl.BlockSpec((None, D_MODEL, D_MODEL), lambda i: (layer, 0, 0))],
        out_specs=[row, pl.BlockSpec((D_MODEL, tm), lambda i: (0, i))],
        out_shape=[jax.ShapeDtypeStruct((t, D_MODEL), F32), jax.ShapeDtypeStruct((D_MODEL, t), BF16)],
        compiler_params=_cparams(("parallel",)),
        name="merge",
    )(*branches[0], *branches[1], *branches[2], proj, proj, proj, x, mod, norm_g, w_br, w_out)


def _router_kernel(h_ref, wq_ref, keys_ref, thr_ref, e0_ref, e1_ref, q_scr, top_scr, cand_scr):
    tb = h_ref.shape[1]
    q_scr[...] = jnp.dot(wq_ref[...], h_ref[...], preferred_element_type=F32)

    top_scr[...] = jnp.full(top_scr.shape, NEG_BIG, F32)
    sub_row = lax.broadcasted_iota(jnp.int32, (SUBLANES, tb), 0)

    def top_values(x, dst):
        for kk in range(TOP_N):
            m = jnp.max(x, axis=0, keepdims=True)
            top_scr[dst, kk:kk + 1, :] = m
            x = jnp.where(x >= m, NEG_BIG, x)

    def body(hd, carry):
        base = pl.multiple_of(hd * 2 * N_KEYS, 2 * N_KEYS)
        s0 = _mm(keys_ref[0], q_scr[pl.ds(base, N_KEYS), :])
        s1 = _mm(keys_ref[1], q_scr[pl.ds(base + N_KEYS, N_KEYS), :])
        top_values(s0, 0)
        top_values(s1, 1)
        cand_scr[0:TOP_PAD, :] = top_scr[0, 0:1, :] + top_scr[1]
        a1 = top_scr[1, 0:SUBLANES, :]
        for p in range(1, N_MULTI):
            keep = sub_row < TOP_N // (p + 1)
            cand_scr[TOP_PAD + (p - 1) * SUBLANES:TOP_PAD + p * SUBLANES, :] = jnp.where(
                keep, top_scr[0, p:p + 1, :] + a1, NEG_BIG)
        cand_scr[CAND_ROWS - SUBLANES:CAND_ROWS, :] = top_scr[0, N_MULTI:TOP_N, :] + top_scr[1, 0:1, :]
        x = cand_scr[...]
        m0 = jnp.max(x, axis=0, keepdims=True)
        z = jnp.zeros_like(m0)
        m = m0
        for kk in range(PEER_TOPK):
            if kk > 0:
                x = jnp.where(x >= m, NEG_BIG, x)
                m = jnp.max(x, axis=0, keepdims=True)
            z = z + jnp.exp(m - m0)
        m_next = jnp.max(jnp.where(x >= m, NEG_BIG, x), axis=0, keepdims=True)
        tau = 0.5 * (m + m_next)
        max1 = top_scr[1, 0:1, :]
        thr = jnp.exp(tau - s0 - max1) / z
        e0 = 0.5 * jnp.exp(s0 - top_scr[0, 0:1, :])
        e1 = jnp.exp(s1 - max1) / z
        for c in range(tb // HEAD_DIM):
            ls = slice(c * HEAD_DIM, (c + 1) * HEAD_DIM)
            thr_ref[hd, c] = thr[:, ls]
            e0_ref[hd, c] = e0[:, ls]
            e1_ref[hd, c] = e1[:, ls]
        return carry

    lax.fori_loop(0, N_HEADS, body, 0)


def _router(h2, wq_t, keys, layer, tb):
    t = h2.shape[1]
    out = pl.BlockSpec((N_HEADS, tb // HEAD_DIM, N_KEYS, HEAD_DIM), lambda i: (0, i, 0, 0))
    shp = jax.ShapeDtypeStruct((N_HEADS, t // HEAD_DIM, N_KEYS, HEAD_DIM), F32)
    return pl.pallas_call(
        _router_kernel,
        grid=(t // tb,),
        in_specs=[pl.BlockSpec((D_MODEL, tb), lambda i: (0, i)),
                  pl.BlockSpec((None, 2 * N_KEYS * N_HEADS, D_MODEL), lambda i: (layer, 0, 0)),
                  pl.BlockSpec((None, 2, N_KEYS, N_KEYS), lambda i: (layer, 0, 0, 0))],
        out_specs=[out, out, out],
        out_shape=[shp, shp, shp],
        scratch_shapes=[pltpu.VMEM((2 * N_KEYS * N_HEADS, tb), F32),
                        pltpu.VMEM((2, TOP_PAD, tb), F32),
                        pltpu.VMEM((CAND_ROWS, tb), F32)],
        compiler_params=_cparams(("parallel",)),
        name="peer_router",
    )(h2, wq_t, keys)


def _expert_kernel(h_ref, u_ref, vt_ref, thr_ref, e0_ref, e1_ref, x_ref, mod_ref, o_ref,
                   acc_scr, g_scr, act_scr, *, tile_e):
    j = pl.program_id(1)
    tb = h_ref.shape[1]
    n_sub = tile_e // N_KEYS
    n_lane = tb // HEAD_DIM

    @pl.when(j == 0)
    def _():
        acc_scr[...] = jnp.zeros_like(acc_scr)

    act = jnp.dot(u_ref[...], h_ref[...], preferred_element_type=F32)
    for c in range(n_lane):
        act_scr[c] = act[:, c * HEAD_DIM:(c + 1) * HEAD_DIM]

    i0 = pl.multiple_of(j * n_sub, n_sub)
    for c in range(n_lane):
        thr = [thr_ref[hd, c, pl.ds(i0, n_sub), :] for hd in range(N_HEADS)]
        e0 = [e0_ref[hd, c, pl.ds(i0, n_sub), :] for hd in range(N_HEADS)]
        for ii in range(n_sub):
            rs = slice(ii * N_KEYS, (ii + 1) * N_KEYS)
            wd = None
            for hd in range(N_HEADS):
                e1 = e1_ref[hd, c]
                w = jnp.where(e1 >= thr[hd][ii:ii + 1, :], e1 * e0[hd][ii:ii + 1, :], 0.0)
                wd = w if wd is None else wd + w
            x = act_scr[c, rs, :]
            g_scr[c, rs, :] = ((x * wd) * (1.0 + lax.erf(x * math.sqrt(0.5)))).astype(BF16)
    g = jnp.concatenate([g_scr[c] for c in range(n_lane)], axis=1)
    acc_scr[...] += jnp.dot(vt_ref[...], g, preferred_element_type=F32)

    @pl.when(j == pl.num_programs(1) - 1)
    def _():
        o_ref[...] = x_ref[...] + mod_ref[5:6, :] * acc_scr[...].T


def _experts(h2t, u_tab, vt_tab, routing, x, mod, layer, mod_row, tb, tile_e):
    t = h2t.shape[1]
    n_exp = u_tab.shape[1]
    n_lane = tb // HEAD_DIM
    rt = pl.BlockSpec((N_HEADS, n_lane, N_KEYS, HEAD_DIM), lambda i, j: (0, i, 0, 0))
    n_tiles = n_exp // tile_e
    return pl.pallas_call(
        functools.partial(_expert_kernel, tile_e=tile_e),
        grid=(t // tb, n_tiles),
        in_specs=[pl.BlockSpec((D_MODEL, tb), lambda i, j: (0, i)),
                  pl.BlockSpec((None, tile_e, D_MODEL), lambda i, j: (layer, j, 0)),
                  pl.BlockSpec((None, None, D_MODEL, tile_e), lambda i, j: (layer, j, 0, 0)),
                  rt, rt, rt,
                  pl.BlockSpec((tb, D_MODEL), lambda i, j: (i, 0)),
                  pl.BlockSpec((None, None, N_MOD, D_MODEL), lambda i, j: (layer, mod_row(i), 0, 0))],
        out_specs=pl.BlockSpec((tb, D_MODEL), lambda i, j: (i, 0)),
        out_shape=jax.ShapeDtypeStruct((t, D_MODEL), F32),
        scratch_shapes=[pltpu.VMEM((D_MODEL, tb), F32), pltpu.VMEM((n_lane, tile_e, HEAD_DIM), BF16),
                        pltpu.VMEM((n_lane, tile_e, HEAD_DIM), F32)],
        compiler_params=_cparams(("parallel", "arbitrary")),
        name="peer_experts",
    )(h2t, u_tab, vt_tab, *routing, x, mod)


def _final_norm_kernel(x_ref, g_ref, o_ref):
    o_ref[...] = _rms(x_ref[...]) * g_ref[...]


def _final_norm(x, g, tm):
    t = x.shape[0]
    row = pl.BlockSpec((tm, D_MODEL), lambda i: (i, 0))
    return pl.pallas_call(
        _final_norm_kernel, grid=(t // tm,),
        in_specs=[row, pl.BlockSpec((1, D_MODEL), lambda i: (0, 0))],
        out_specs=row, out_shape=jax.ShapeDtypeStruct((t, D_MODEL), F32),
        compiler_params=_cparams(("parallel",)), name="final_norm",
    )(x, g)


def _rope_tables(seq_len):
    rows = seq_len // GRID_W
    r = jnp.repeat(jnp.arange(rows, dtype=F32), GRID_W)
    col = jnp.tile(jnp.arange(GRID_W, dtype=F32), rows)
    nf = HEAD_DIM // 4
    inv = ROPE_BASE ** (-jnp.arange(nf, dtype=F32) / nf)
    ang = jnp.concatenate([r[:, None] * inv, col[:, None] * inv], axis=-1)
    cos, sin = jnp.cos(ang), jnp.sin(ang)
    return jnp.concatenate([cos, cos], axis=-1), jnp.concatenate([-sin, sin], axis=-1)


def _lanes(a):
    return jnp.broadcast_to(jnp.moveaxis(a, -1, 0)[..., None], (a.shape[-1],) + a.shape[:-1] + (HEAD_DIM,))


def kernel(x_prompt, x_sample, c, state_ret, state_lru, state_dn, c_ctx, w_mod, b_mod, norm1_g, norm2_g, w_in, ret_gamma_logit, ret_norm_g, lru_conv_w, lru_conv_b, lru_gate_w, lru_gate_b, lru_lambda, dn_conv_w, dn_a_log, dn_dt_bias, dn_norm_g, w_br, w_out, peer_w_q, peer_sub_keys, peer_u, peer_v, final_norm_g):
    n_ctx, l_ctx, _ = x_prompt.shape
    n_lat, l_lat, _ = x_sample.shape
    t_ctx, t_lat = n_ctx * l_ctx, n_lat * l_lat
    assert t_ctx % l_lat == 0
    tb = TOKEN_BLOCK
    assert l_lat % tb == 0 and t_ctx % tb == 0
    ctx_blocks, per_seq = t_ctx // tb, l_lat // tb

    def mod_row(i):
        return jnp.where(i < ctx_blocks, 0, 1 + (i - ctx_blocks) // per_seq)

    tm = MERGE_ROWS
    ctx_blocks_m, per_seq_m = t_ctx // tm, l_lat // tm

    def mod_row_m(i):
        return jnp.where(i < ctx_blocks_m, 0, 1 + (i - ctx_blocks_m) // per_seq_m)

    x = jnp.concatenate([x_prompt.reshape(t_ctx, D_MODEL), x_sample.reshape(t_lat, D_MODEL)], axis=0)
    n_cond = 16
    cond = jnp.zeros((n_cond, D_MODEL), F32).at[0].set(c_ctx).at[1:1 + n_lat].set(c)
    mods = _modulation(cond, w_mod, b_mod).reshape(DEPTH, n_cond, N_MOD, D_MODEL)

    w_in_b = w_in.astype(BF16)
    w_in_r = jnp.concatenate(
        [w_in_b[:, :, :N_MAIN], w_in_b[:, :, N_MAIN + N_SMALL:], w_in_b[:, :, N_MAIN:N_MAIN + N_SMALL],
         jnp.zeros((DEPTH, D_MODEL, HEAD_DIM - N_SMALL), BF16)], axis=-1)
    w_br_b, w_out_b = w_br.astype(BF16), w_out.astype(BF16)
    wq_t = jnp.swapaxes(peer_w_q, 1, 2).astype(BF16)
    keys_b = peer_sub_keys.astype(BF16)
    u_b = peer_u.astype(BF16)
    vt_b = jnp.swapaxes(peer_v.reshape(DEPTH, N_EXPERTS // EXPERT_TILE, EXPERT_TILE, D_MODEL), 2, 3).astype(BF16)
    gam = _lanes(ret_gamma_logit)
    lane_pad = jnp.zeros((DEPTH, 2, N_HEADS), F32)
    dn_ab = jnp.stack([dn_a_log, dn_dt_bias], axis=1)
    dn_par = jnp.concatenate([dn_ab[:, :, 0], lane_pad, dn_ab[:, :, 1], lane_pad,
                              jnp.zeros((DEPTH, 2, HEAD_DIM - N_SMALL), F32)], axis=-1)
    rope_tabs = _rope_tables(l_lat)
    zero_ret = jnp.zeros((n_ctx, 2, N_HEADS, HEAD_DIM, HEAD_DIM), F32)
    zero_lru = jnp.zeros((n_ctx, 2, D_MODEL), F32)

    row_off_lat = t_ctx // l_lat

    ret_states, lru_states, dn_states = [], [], []
    for l in range(DEPTH):
        proj = _in_proj(x, mods, norm1_g[l][None], w_in_r, l,
                        lambda i: jnp.where(i < row_off_lat, 0, 1 + i - row_off_lat), l_lat)

        ng_ret = ret_norm_g[l][:, None, :]
        o_ret_c, s_ret = _retention(proj, gam[:, l], ng_ret, zero_ret, None, None, n_ctx, l_ctx, 0, MIXER_HEADS)
        o_ret_l, _ = _retention(proj, gam[:, l], ng_ret, state_ret, l, rope_tabs, n_lat, l_lat, row_off_lat, MIXER_HEADS)

        lru_args = (lru_conv_w[l], lru_conv_b[l][None], lru_gate_w[l], lru_gate_b[l], lru_lambda[l])
        o_lru_c, s_lru = _rglru(proj, *lru_args, zero_lru, None, n_ctx, l_ctx, 0, _lru_blocks(l_ctx))
        o_lru_l, _ = _rglru(proj, *lru_args, state_lru, l, n_lat, l_lat, row_off_lat, _lru_blocks(l_lat))

        ng_dn = dn_norm_g[l][:, None, :]
        o_dn_c, s_dn = _deltanet(proj, dn_conv_w[l], dn_par[l], ng_dn, zero_ret, None, n_ctx, l_ctx, 0, MIXER_HEADS)
        o_dn_l, _ = _deltanet(proj, dn_conv_w[l], dn_par[l], ng_dn, state_dn, l, n_lat, l_lat, row_off_lat, MIXER_HEADS)

        branches = ((o_ret_c, o_ret_l), (o_lru_c, o_lru_l), (o_dn_c, o_dn_l))
        x, h2 = _merge(branches, proj, x, mods, norm2_g[l][None], w_br_b, w_out_b, l, mod_row_m, tm)
        routing = _router(h2, wq_t, keys_b, l, tb)
        x = _experts(h2, u_b, vt_b, routing, x, mods, l, mod_row, tb, EXPERT_TILE)

        ret_states.append(s_ret)
        lru_states.append(s_lru)
        dn_states.append(s_dn)

    y = _final_norm(x, final_norm_g[None], tb)
    y_prompt = y[:t_ctx].reshape(n_ctx, l_ctx, D_MODEL)
    y_sample = y[t_ctx:].reshape(n_lat, l_lat, D_MODEL)
    return (y_prompt, y_sample, jnp.stack(ret_states, axis=1), jnp.stack(lru_states, axis=1),
            jnp.stack(dn_states, axis=1))
```

```python
import functools
import math

import jax
import jax.numpy as jnp
from jax import lax
from jax.experimental import pallas as pl
from jax.experimental.pallas import tpu as pltpu

F32 = jnp.float32
BF16 = jnp.bfloat16
BRANCH_DTYPE = BF16

D_MODEL = 1024
DEPTH = 4
N_MOD = 6
EPS = 1e-6
GRID_W = 64
ROPE_BASE = 10000.0
N_HEADS = 8
HEAD_DIM = 128
RET_CHUNK = 128
DN_CHUNK = 64
DN_SUPER = 256
LRU_C = 8.0
N_KEYS = 128
PEER_TOPK = 16
N_EXPERTS = N_KEYS * N_KEYS
SUBLANES = 8
TOP_N = PEER_TOPK + 1
TOP_PAD = -(-TOP_N // SUBLANES) * SUBLANES
N_MULTI = TOP_N - SUBLANES
CAND_ROWS = TOP_PAD + N_MULTI * SUBLANES
assert TOP_N // 2 <= SUBLANES and TOP_N // (N_MULTI + 1) == 1
EXPERT_TILE = 2048
NEG_BIG = -3.0e38

COL_RQ, COL_RK, COL_RV, COL_RG = 0, 8, 16, 24
COL_LX, COL_LG = 32, 40
COL_DQ, COL_DK, COL_DV, COL_DZ = 48, 56, 64, 72
N_MAIN = 80 * 128
N_SMALL = 4 * N_HEADS
COL_BGATE = 80
COL_SMALL = 104
N_PROJ = 105 * 128

TOKEN_BLOCK = 512
MERGE_ROWS = 256
IN_PROJ_COLS = 15 * 128
MOD_COLS = 1536
MIXER_HEADS = 4
LRU_STEP_ROWS = 2048
VMEM_LIMIT = 48 * 1024 * 1024
VMEM_LIMIT_WIDE = 58 * 1024 * 1024


def _cparams(sem, vmem_limit=VMEM_LIMIT):
    return pltpu.CompilerParams(dimension_semantics=sem, vmem_limit_bytes=vmem_limit)


def _mm(a, b):
    return jnp.dot(a.astype(BF16), b.astype(BF16), preferred_element_type=F32)


def _mm_nt(a, b):
    return lax.dot_general(a.astype(BF16), b.astype(BF16), (((1,), (1,)), ((), ())),
                           preferred_element_type=F32)


def _mm_tn(a, b):
    return lax.dot_general(a.astype(BF16), b.astype(BF16), (((0,), (0,)), ((), ())),
                           preferred_element_type=F32)


def _softplus(x):
    return jnp.maximum(x, 0.0) + jnp.log1p(jnp.exp(-jnp.abs(x)))


def _silu(x):
    return x * jax.nn.sigmoid(x)


def _gelu(x):
    return 0.5 * x * (1.0 + lax.erf(x * math.sqrt(0.5)))


def _rms(x):
    return x * lax.rsqrt(jnp.mean(x * x, axis=-1, keepdims=True) + EPS)


def _shift_rows(x, s, row):
    n = x.shape[0]
    if s == 0:
        return x
    y = pltpu.roll(x, (-s) % n, 0)
    ok = (row + s >= 0) & (row + s < n)
    return jnp.where(ok, y, 0.0)


def _dw_conv(x, w, row):
    y = _shift_rows(x, -2, row) * w[0:1, :]
    y = y + _shift_rows(x, -1, row) * w[1:2, :]
    y = y + x * w[2:3, :]
    y = y + _shift_rows(x, 1, row) * w[3:4, :]
    return y


def _mod_kernel(c_ref, w_ref, b_ref, o_ref):
    c = c_ref[...]
    o_ref[...] = jnp.dot(_silu(c), w_ref[...], precision=lax.Precision.HIGHEST,
                         preferred_element_type=F32) + b_ref[...]


def _modulation(cond, w_mod, b_mod):
    n_rows = cond.shape[0]
    tn = MOD_COLS
    n_out = N_MOD * D_MODEL
    return pl.pallas_call(
        _mod_kernel,
        grid=(DEPTH, n_out // tn),
        in_specs=[pl.BlockSpec((n_rows, D_MODEL), lambda l, j: (0, 0)),
                  pl.BlockSpec((None, D_MODEL, tn), lambda l, j: (l, 0, j)),
                  pl.BlockSpec((None, 1, tn), lambda l, j: (l, 0, j))],
        out_specs=pl.BlockSpec((None, n_rows, tn), lambda l, j: (l, 0, j)),
        out_shape=jax.ShapeDtypeStruct((DEPTH, n_rows, n_out), F32),
        compiler_params=_cparams(("parallel", "parallel")),
        name="modulation",
    )(cond, w_mod, b_mod.reshape(DEPTH, 1, n_out))


def _in_proj_kernel(x_ref, mod_ref, g_ref, w_ref, o_ref, h_scr):
    @pl.when(pl.program_id(1) == 0)
    def _():
        y = _rms(x_ref[...]) * g_ref[...]
        h_scr[...] = (y * (1.0 + mod_ref[1:2, :]) + mod_ref[0:1, :]).astype(BF16)

    o_ref[...] = jnp.dot(h_scr[...], w_ref[...], preferred_element_type=F32)


def _in_proj(x, mod, norm_g, w, layer, mod_row, tm):
    t = x.shape[0]
    tn = IN_PROJ_COLS
    return pl.pallas_call(
        _in_proj_kernel,
        grid=(t // tm, N_PROJ // tn),
        in_specs=[pl.BlockSpec((tm, D_MODEL), lambda i, j: (i, 0)),
                  pl.BlockSpec((None, None, N_MOD, D_MODEL), lambda i, j: (layer, mod_row(i), 0, 0)),
                  pl.BlockSpec((1, D_MODEL), lambda i, j: (0, 0)),
                  pl.BlockSpec((None, D_MODEL, tn), lambda i, j: (layer, 0, j))],
        out_specs=pl.BlockSpec((tm, tn), lambda i, j: (i, j)),
        out_shape=jax.ShapeDtypeStruct((t, N_PROJ), F32),
        scratch_shapes=[pltpu.VMEM((tm, D_MODEL), BF16)],
        compiler_params=_cparams(("parallel", "arbitrary")),
        name="in_proj",
    )(x, mod, norm_g, w)


def _ret_kernel(*refs, seq_len, rope, hg):
    if rope:
        (q_ref, k_ref, v_ref, g_ref, gam_ref, ng_ref, s0_ref, cs_ref, sn_ref,
         o_ref, so_ref, of_scr, ob_scr) = refs
    else:
        (q_ref, k_ref, v_ref, g_ref, gam_ref, ng_ref, s0_ref,
         o_ref, so_ref, of_scr, ob_scr) = refs
    c = RET_CHUNK
    n_chunks = seq_len // c
    r = lax.broadcasted_iota(jnp.int32, (c, HEAD_DIM), 0).astype(F32)
    ci = lax.broadcasted_iota(jnp.int32, (c, c), 0)
    si = lax.broadcasted_iota(jnp.int32, (c, c), 1)
    dmat = (ci - si).astype(F32)
    scale = HEAD_DIM ** -0.5

    chains = [(hh, d) for hh in range(hg) for d in range(2)]
    idx = range(len(chains))
    dec, qsc, ksc, gch = [], [], [], []
    for hh, d in chains:
        lg = -_softplus(-gam_ref[hh, d:d + 1, :])
        if d == 0:
            dec.append(jnp.where(dmat >= 0, jnp.exp(lg * jnp.maximum(dmat, 0.0)), 0.0))
            qsc.append(jnp.exp(lg * (r + 1.0)))
            ksc.append(jnp.exp(lg * (c - 1.0 - r)))
        else:
            dec.append(jnp.where(dmat <= 0, jnp.exp(lg * jnp.maximum(-dmat, 0.0)), 0.0))
            qsc.append(jnp.exp(lg * (c - r)))
            ksc.append(jnp.exp(lg * r))
        gch.append(jnp.exp(lg * c))

    s = [s0_ref[d, hh] for hh, d in chains]
    for stp in range(n_chunks):
        sl = [pl.ds((stp if d == 0 else n_chunks - 1 - stp) * c, c) for _, d in chains]
        cs = [slice(hh * HEAD_DIM, (hh + 1) * HEAD_DIM) for hh, _ in chains]
        q = [q_ref[sl[i], cs[i]] for i in idx]
        k = [k_ref[sl[i], cs[i]] * scale for i in idx]
        v = [v_ref[sl[i], cs[i]] for i in idx]
        if rope:
            cos = [cs_ref[sl[i], :] for i in idx]
            sin = [sn_ref[sl[i], :] for i in idx]
            q = [q[i] * cos[i] + pltpu.roll(q[i], HEAD_DIM // 2, 1) * sin[i] for i in idx]
            k = [k[i] * cos[i] + pltpu.roll(k[i], HEAD_DIM // 2, 1) * sin[i] for i in idx]
        sc = [_mm_nt(q[i], k[i]) for i in idx]
        qs = [_mm(q[i] * qsc[i], s[i]) for i in idx]
        kv = [_mm_tn(k[i] * ksc[i], v[i]) for i in idx]
        oi = [_mm(sc[i] * dec[i], v[i]) for i in idx]
        s = [s[i] * gch[i] + kv[i] for i in idx]
        for i in idx:
            if chains[i][1] == 0:
                of_scr[sl[i], cs[i]] = oi[i] + qs[i]
            else:
                ob_scr[sl[i], cs[i]] = oi[i] + qs[i]
    for i in idx:
        so_ref[chains[i][1], chains[i][0]] = s[i]
    tot = of_scr[...] + ob_scr[...]
    g = g_ref[...]
    for hh in range(hg):
        cs1 = slice(hh * HEAD_DIM, (hh + 1) * HEAD_DIM)
        o_ref[:, cs1] = (_rms(tot[:, cs1]) * ng_ref[hh] * _silu(g[:, cs1])).astype(o_ref.dtype)


def _retention(proj, gam, norm_g, s0, layer, rope_tabs, n_seq, seq_len, row_off, hg):
    rope = rope_tabs is not None
    wide = hg * HEAD_DIM

    def col(c0):
        return pl.BlockSpec((seq_len, wide), lambda b, g: (row_off + b, c0 // hg + g))

    in_specs = [col(COL_RQ), col(COL_RK), col(COL_RV), col(COL_RG),
                pl.BlockSpec((hg, 2, HEAD_DIM), lambda b, g: (g, 0, 0)),
                pl.BlockSpec((hg, 1, HEAD_DIM), lambda b, g: (g, 0, 0)),
                _state_spec(s0, layer, hg)]
    args = [proj, proj, proj, proj, gam, norm_g, s0]
    if rope:
        tab = pl.BlockSpec((seq_len, HEAD_DIM), lambda b, g: (0, 0))
        in_specs += [tab, tab]
        args += list(rope_tabs)
    return pl.pallas_call(
        functools.partial(_ret_kernel, seq_len=seq_len, rope=rope, hg=hg),
        grid=(n_seq, N_HEADS // hg),
        in_specs=in_specs,
        out_specs=[pl.BlockSpec((seq_len, wide), lambda b, g: (b, g)),
                   pl.BlockSpec((None, 2, hg, HEAD_DIM, HEAD_DIM), lambda b, g: (b, 0, g, 0, 0))],
        out_shape=[jax.ShapeDtypeStruct((n_seq * seq_len, D_MODEL), BRANCH_DTYPE),
                   jax.ShapeDtypeStruct((n_seq, 2, N_HEADS, HEAD_DIM, HEAD_DIM), F32)],
        scratch_shapes=[pltpu.VMEM((seq_len, wide), F32), pltpu.VMEM((seq_len, wide), F32)],
        compiler_params=_cparams(("parallel", "parallel")),
        name="retention_rope" if rope else "retention",
    )(*args)


def _lru_kernel(x_ref, gate_ref, cw_ref, cb_ref, gw_ref, gb_ref, lam_ref, s0_ref, o_ref, so_ref, *, seq_len, nb):
    n = seq_len
    row = lax.broadcasted_iota(jnp.int32, (n, nb * HEAD_DIM), 0)
    xc = _dw_conv(x_ref[...], cw_ref[...], row) + cb_ref[...]
    lam = lam_ref[...]

    def gate(d, which):
        cols = [_mm(xc[:, s * HEAD_DIM:(s + 1) * HEAD_DIM], gw_ref[d, which, s]) for s in range(nb)]
        return jax.nn.sigmoid(jnp.concatenate(cols, axis=1) + gb_ref[d, which:which + 1, :])

    hs = []
    for d in range(2):
        r_gate = gate(d, 0)
        i_gate = gate(d, 1)
        log_a = -LRU_C * r_gate * _softplus(-lam[d:d + 1, :])
        a = jnp.exp(log_a)
        u = jnp.sqrt(-jnp.tanh(log_a) * (1.0 + a * a)) * i_gate * xc
        step = 1
        while step < n:
            if d == 0:
                ok = row >= step
                sh = step
            else:
                ok = row < n - step
                sh = n - step
            a_sh = jnp.where(ok, pltpu.roll(a, sh, 0), 1.0)
            u_sh = jnp.where(ok, pltpu.roll(u, sh, 0), 0.0)
            u = a * u_sh + u
            a = a * a_sh
            step *= 2
        h = u + a * s0_ref[d:d + 1, :]
        hs.append(h)
        so_ref[d:d + 1, :] = h[n - 1:n, :] if d == 0 else h[0:1, :]
    o_ref[...] = ((hs[0] + hs[1]) * _gelu(gate_ref[...])).astype(o_ref.dtype)


def _lru_blocks(seq_len):
    return max(1, min(N_HEADS, LRU_STEP_ROWS // seq_len))


def _rglru(proj, conv_w, conv_b, gate_w, gate_b, lam, s0, layer, n_seq, seq_len, row_off, nb):
    wide = nb * HEAD_DIM
    if layer is None:
        s0_spec = pl.BlockSpec((None, 2, wide), lambda b, n: (b, 0, n))
    else:
        s0_spec = pl.BlockSpec((None, None, 2, wide), lambda b, n: (b, layer, 0, n))
    return pl.pallas_call(
        functools.partial(_lru_kernel, seq_len=seq_len, nb=nb),
        grid=(n_seq, N_HEADS // nb),
        in_specs=[pl.BlockSpec((seq_len, wide), lambda b, n: (row_off + b, COL_LX // nb + n)),
                  pl.BlockSpec((seq_len, wide), lambda b, n: (row_off + b, COL_LG // nb + n)),
                  pl.BlockSpec((4, wide), lambda b, n: (0, n)),
                  pl.BlockSpec((1, wide), lambda b, n: (0, n)),
                  pl.BlockSpec((2, 2, nb, HEAD_DIM, HEAD_DIM), lambda b, n: (0, 0, n, 0, 0)),
                  pl.BlockSpec((2, 2, wide), lambda b, n: (0, 0, n)),
                  pl.BlockSpec((2, wide), lambda b, n: (0, n)),
                  s0_spec],
        out_specs=[pl.BlockSpec((seq_len, wide), lambda b, n: (b, n)),
                   pl.BlockSpec((None, 2, wide), lambda b, n: (b, 0, n))],
        out_shape=[jax.ShapeDtypeStruct((n_seq * seq_len, D_MODEL), BRANCH_DTYPE),
                   jax.ShapeDtypeStruct((n_seq, 2, D_MODEL), F32)],
        compiler_params=_cparams(("parallel", "parallel")),
        name="rglru",
    )(proj, proj, conv_w, conv_b, gate_w, gate_b, lam, s0)


def _dn_kernel(q_ref, k_ref, v_ref, z_ref, sm_ref, cwq_ref, cwk_ref, cwv_ref, par_ref, ng_ref, s0_ref,
               o_ref, so_ref, q_scr, k_scr, v_scr, c_scr, b_scr, of_scr, ob_scr, *, seq_len, hg):
    n = seq_len
    cc = DN_CHUNK
    sc = DN_SUPER
    n_super = n // sc
    head0 = pl.program_id(1) * hg
    roww = lax.broadcasted_iota(jnp.int32, (n, hg * HEAD_DIM), 0)
    row = lax.broadcasted_iota(jnp.int32, (n, HEAD_DIM), 0)
    lane = lax.broadcasted_iota(jnp.int32, (n, HEAD_DIM), 1)
    pos = row & (cc - 1)

    xq = _silu(_dw_conv(q_ref[...], cwq_ref[...], roww))
    xk = _silu(_dw_conv(k_ref[...], cwk_ref[...], roww))
    v_scr[...] = _silu(_dw_conv(v_ref[...], cwv_ref[...], roww))
    small = sm_ref[...]
    par = par_ref[...]
    g_all = -jnp.exp(par[0:1, :]) * _softplus(small + par[1:2, :])
    beta_all = jax.nn.sigmoid(small)
    cum_all = [g_all, g_all]
    step = 1
    while step < cc:
        cum_all[0] = cum_all[0] + jnp.where(pos >= step, pltpu.roll(cum_all[0], step, 0), 0.0)
        cum_all[1] = cum_all[1] + jnp.where(pos < cc - step, pltpu.roll(cum_all[1], n - step, 0), 0.0)
        step *= 2
    for hh in range(hg):
        cs = slice(hh * HEAD_DIM, (hh + 1) * HEAD_DIM)
        xqh, xkh = xq[:, cs], xk[:, cs]
        q_scr[:, cs] = xqh * lax.rsqrt(jnp.sum(xqh * xqh, axis=-1, keepdims=True) + EPS) * (HEAD_DIM ** -0.5)
        k_scr[:, cs] = xkh * lax.rsqrt(jnp.sum(xkh * xkh, axis=-1, keepdims=True) + EPS)
        head = head0 + hh
        for d in range(2):
            a_lane = lane == 2 * N_HEADS * d + head
            b_lane = lane == 2 * N_HEADS * d + N_HEADS + head
            cum = jnp.sum(jnp.where(a_lane, cum_all[d], 0.0), axis=-1, keepdims=True)
            beta = jnp.sum(jnp.where(b_lane, beta_all, 0.0), axis=-1, keepdims=True)
            c_scr[d, hh] = jnp.broadcast_to(cum, (n, HEAD_DIM))
            b_scr[d, hh] = jnp.broadcast_to(beta, (n, HEAD_DIM))

    ri = lax.broadcasted_iota(jnp.int32, (sc, sc), 0)
    cj = lax.broadcasted_iota(jnp.int32, (sc, sc), 1)
    sh = cc.bit_length() - 1
    same = (ri >> sh) == (cj >> sh)
    incl = (same & (ri >= cj), same & (ri <= cj))
    strict = (same & (ri > cj), same & (ri < cj))
    eye = jnp.where(ri == cj, 1.0, 0.0)
    level = [(ri >> 3) == (cj >> 3)]
    for b in range(4, sh + 1):
        level.append(((ri >> b) == (cj >> b)) & ((ri >> (b - 1)) != (cj >> (b - 1))))

    n_ch = sc // cc

    def super_chunks(chains):
        idx = range(len(chains))
        dd = [c[0] for c in chains]
        sl = [pl.ds(c[2], sc) for c in chains]
        cs = [slice(c[1] * HEAD_DIM, (c[1] + 1) * HEAD_DIM) for c in chains]
        q = [q_scr[sl[c], cs[c]] for c in idx]
        k = [k_scr[sl[c], cs[c]] for c in idx]
        v = [v_scr[sl[c], cs[c]] for c in idx]
        cum = [c_scr[dd[c], chains[c][1], sl[c], :] for c in idx]
        beta = [b_scr[dd[c], chains[c][1], sl[c], :] for c in idx]
        kk = [_mm_nt(k[c], k[c]) for c in idx]
        qk = [_mm_nt(q[c], k[c]) for c in idx]
        decay, x, attn = [], [], []
        for c in idx:
            cb = jnp.concatenate([cum[c], cum[c]], axis=1)
            diff = cb - cb.T
            dec = jnp.where(incl[dd[c]], jnp.exp(diff), 0.0)
            nb = jnp.concatenate([-beta[c], -beta[c]], axis=1)
            x.append(jnp.where(strict[dd[c]], kk[c] * nb * dec, 0.0))
            attn.append(qk[c] * dec)
        xp = [jnp.where(level[0], x[c], 0.0) for c in idx]
        p = [eye + xp[c] for c in idx]
        for _ in range(2):
            xp = [_mm(xp[c], xp[c]) for c in idx]
            pm = [_mm(p[c], xp[c]) for c in idx]
            p = [p[c] + pm[c] for c in idx]
        for lv in range(1, len(level)):
            t1 = [_mm(jnp.where(level[lv], x[c], 0.0), p[c]) for c in idx]
            t2 = [_mm(p[c], t1[c]) for c in idx]
            p = [p[c] + t2[c] for c in idx]
        rhs = [jnp.concatenate([v[c] * beta[c], k[c] * beta[c] * jnp.exp(cum[c])], axis=1) for c in idx]
        sol = [_mm(p[c], rhs[c]) for c in idx]
        qd = [q[c] * jnp.exp(cum[c]) for c in idx]
        s = [so_ref[dd[c], chains[c][1]] for c in idx]
        v_new = [[None] * n_ch for _ in idx]
        o_inter = [[None] * n_ch for _ in idx]
        for stp in range(n_ch):
            ch = [stp if dd[c] == 0 else n_ch - 1 - stp for c in idx]
            rs = [slice(ch[c] * cc, (ch[c] + 1) * cc) for c in idx]
            ws = [_mm(sol[c][rs[c], HEAD_DIM:], s[c]) for c in idx]
            for c in idx:
                o_inter[c][ch[c]] = _mm(qd[c][rs[c], :], s[c])
            kt = []
            for c in idx:
                cum_c = cum[c][rs[c], :]
                tot = cum_c[cc - 1:cc, :] if dd[c] == 0 else cum_c[0:1, :]
                v_new[c][ch[c]] = sol[c][rs[c], :HEAD_DIM] - ws[c]
                kt.append((k[c][rs[c], :] * jnp.exp(tot - cum_c), jnp.exp(tot)))
            upd = [_mm_tn(kt[c][0], v_new[c][ch[c]]) for c in idx]
            s = [s[c] * kt[c][1] + upd[c] for c in idx]
        oi = [_mm(attn[c], jnp.concatenate(v_new[c], axis=0)) for c in idx]
        for c in idx:
            o = jnp.concatenate(o_inter[c], axis=0) + oi[c]
            if dd[c] == 0:
                of_scr[sl[c], cs[c]] = o
            else:
                ob_scr[sl[c], cs[c]] = o
            so_ref[dd[c], chains[c][1]] = s[c]

    so_ref[...] = s0_ref[...]

    def body(i, carry):
        fwd = pl.multiple_of(i * sc, sc)
        bwd = pl.multiple_of((n_super - 1 - i) * sc, sc)
        super_chunks([(d, hh, fwd if d == 0 else bwd) for hh in range(hg) for d in range(2)])
        return carry

    lax.fori_loop(0, n_super, body, 0)
    tot = of_scr[...] + ob_scr[...]
    z = z_ref[...]
    for hh in range(hg):
        cs = slice(hh * HEAD_DIM, (hh + 1) * HEAD_DIM)
        o_ref[:, cs] = (_rms(tot[:, cs]) * ng_ref[hh] * _silu(z[:, cs])).astype(o_ref.dtype)


def _state_spec(s0, layer, hg):
    if layer is None:
        return pl.BlockSpec((None, 2, hg, HEAD_DIM, HEAD_DIM), lambda b, g: (b, 0, g, 0, 0))
    return pl.BlockSpec((None, None, 2, hg, HEAD_DIM, HEAD_DIM), lambda b, g: (b, layer, 0, g, 0, 0))


def _deltanet(proj, conv_w, par, norm_g, s0, layer, n_seq, seq_len, row_off, hg):
    wide = hg * HEAD_DIM

    def col(c0):
        return pl.BlockSpec((seq_len, wide), lambda b, g: (row_off + b, c0 // hg + g))

    def cw(c0):
        return pl.BlockSpec((4, wide), lambda b, g: (0, c0 // hg + g))

    vm = functools.partial(pltpu.VMEM, dtype=F32)
    return pl.pallas_call(
        functools.partial(_dn_kernel, seq_len=seq_len, hg=hg),
        grid=(n_seq, N_HEADS // hg),
        in_specs=[col(COL_DQ), col(COL_DK), col(COL_DV), col(COL_DZ),
                  pl.BlockSpec((seq_len, HEAD_DIM), lambda b, g: (row_off + b, COL_SMALL)),
                  cw(0), cw(N_HEADS), cw(2 * N_HEADS),
                  pl.BlockSpec((2, HEAD_DIM), lambda b, g: (0, 0)),
                  pl.BlockSpec((hg, 1, HEAD_DIM), lambda b, g: (g, 0, 0)),
                  _state_spec(s0, layer, hg)],
        out_specs=[pl.BlockSpec((seq_len, wide), lambda b, g: (b, g)),
                   pl.BlockSpec((None, 2, hg, HEAD_DIM, HEAD_DIM), lambda b, g: (b, 0, g, 0, 0))],
        out_shape=[jax.ShapeDtypeStruct((n_seq * seq_len, D_MODEL), BRANCH_DTYPE),
                   jax.ShapeDtypeStruct((n_seq, 2, N_HEADS, HEAD_DIM, HEAD_DIM), F32)],
        scratch_shapes=[vm((seq_len, wide)), vm((seq_len, wide)), vm((seq_len, wide)),
                        vm((2, hg, seq_len, HEAD_DIM)), vm((2, hg, seq_len, HEAD_DIM)),
                        vm((seq_len, wide)), vm((seq_len, wide))],
        compiler_params=_cparams(("parallel", "parallel"),
                                 VMEM_LIMIT_WIDE if seq_len * hg > 2048 else VMEM_LIMIT),
        name="deltanet",
    )(proj, proj, proj, proj, proj, conv_w, conv_w, conv_w, par, norm_g, s0)


def _merge_kernel(retc_ref, retl_ref, lruc_ref, lrul_ref, dnc_ref, dnl_ref, g0_ref, g1_ref, g2_ref, x_ref, mod_ref,
                  ng_ref, wbr_ref, wout_ref, xo_ref, h_ref, *, ctx_blocks):
    is_ctx = pl.program_id(0) < ctx_blocks
    ret = jnp.where(is_ctx, retc_ref[...], retl_ref[...])
    lru = jnp.where(is_ctx, lruc_ref[...], lrul_ref[...])
    dn = jnp.where(is_ctx, dnc_ref[...], dnl_ref[...])
    merged = jax.nn.sigmoid(g0_ref[...]) * _mm(ret, wbr_ref[0])
    merged = merged + jax.nn.sigmoid(g1_ref[...]) * _mm(lru, wbr_ref[1])
    merged = merged + jax.nn.sigmoid(g2_ref[...]) * _mm(dn, wbr_ref[2])
    x = x_ref[...] + mod_ref[2:3, :] * _mm(merged, wout_ref[...])
    xo_ref[...] = x
    h_ref[...] = (_rms(x) * ng_ref[...] * (1.0 + mod_ref[4:5, :]) + mod_ref[3:4, :]).T.astype(BF16)


def _merge(branches, proj, x, mod, norm_g, w_br, w_out, layer, mod_row, tm):
    t = x.shape[0]
    ctx_blocks = branches[0][0].shape[0] // tm
    row = pl.BlockSpec((tm, D_MODEL), lambda i: (i, 0))
    row_c = pl.BlockSpec((tm, D_MODEL), lambda i: (jnp.minimum(i, ctx_blocks - 1), 0))
    row_l = pl.BlockSpec((tm, D_MODEL), lambda i: (jnp.maximum(i - ctx_blocks, 0), 0))
    col0 = COL_BGATE * HEAD_DIM // D_MODEL

    def gate(k):
        return pl.BlockSpec((tm, D_MODEL), lambda i: (i, col0 + k))

    return pl.pallas_call(
        functools.partial(_merge_kernel, ctx_blocks=ctx_blocks),
        grid=(t // tm,),
        in_specs=[row_c, row_l, row_c, row_l, row_c, row_l, gate(0), gate(1), gate(2), row,
                  pl.BlockSpec((None, None, N_MOD, D_MODEL), lambda i: (layer, mod_row(i), 0, 0)),
                  pl.BlockSpec((1, D_MODEL), lambda i: (0, 0)),
                  pl.BlockSpec((None, 3, D_MODEL, D_MODEL), lambda i: (layer, 0, 0, 0)),
                  pl.BlockSpec((None, D_MODEL, D_MODEL), lambda i: (layer, 0, 0))],
        out_specs=[row, pl.BlockSpec((D_MODEL, tm), lambda i: (0, i))],
        out_shape=[jax.ShapeDtypeStruct((t, D_MODEL), F32), jax.ShapeDtypeStruct((D_MODEL, t), BF16)],
        compiler_params=_cparams(("parallel",)),
        name="merge",
    )(*branches[0], *branches[1], *branches[2], proj, proj, proj, x, mod, norm_g, w_br, w_out)


def _router_kernel(h_ref, wq_ref, keys_ref, thr_ref, e0_ref, e1_ref, q_scr, top_scr, cand_scr):
    tb = h_ref.shape[1]
    q_scr[...] = jnp.dot(wq_ref[...], h_ref[...], preferred_element_type=F32)

    top_scr[...] = jnp.full(top_scr.shape, NEG_BIG, F32)
    sub_row = lax.broadcasted_iota(jnp.int32, (SUBLANES, tb), 0)

    def top_values(x, dst):
        for kk in range(TOP_N):
            m = jnp.max(x, axis=0, keepdims=True)
            top_scr[dst, kk:kk + 1, :] = m
            x = jnp.where(x >= m, NEG_BIG, x)

    def body(hd, carry):
        base = pl.multiple_of(hd * 2 * N_KEYS, 2 * N_KEYS)
        s0 = _mm(keys_ref[0], q_scr[pl.ds(base, N_KEYS), :])
        s1 = _mm(keys_ref[1], q_scr[pl.ds(base + N_KEYS, N_KEYS), :])
        top_values(s0, 0)
        top_values(s1, 1)
        cand_scr[0:TOP_PAD, :] = top_scr[0, 0:1, :] + top_scr[1]
        a1 = top_scr[1, 0:SUBLANES, :]
        for p in range(1, N_MULTI):
            keep = sub_row < TOP_N // (p + 1)
            cand_scr[TOP_PAD + (p - 1) * SUBLANES:TOP_PAD + p * SUBLANES, :] = jnp.where(
                keep, top_scr[0, p:p + 1, :] + a1, NEG_BIG)
        cand_scr[CAND_ROWS - SUBLANES:CAND_ROWS, :] = top_scr[0, N_MULTI:TOP_N, :] + top_scr[1, 0:1, :]
        x = cand_scr[...]
        m0 = jnp.max(x, axis=0, keepdims=True)
        z = jnp.zeros_like(m0)
        m = m0
        for kk in range(PEER_TOPK):
            if kk > 0:
                x = jnp.where(x >= m, NEG_BIG, x)
                m = jnp.max(x, axis=0, keepdims=True)
            z = z + jnp.exp(m - m0)
        m_next = jnp.max(jnp.where(x >= m, NEG_BIG, x), axis=0, keepdims=True)
        tau = 0.5 * (m + m_next)
        max1 = top_scr[1, 0:1, :]
        thr = jnp.exp(tau - s0 - max1) / z
        e0 = 0.5 * jnp.exp(s0 - top_scr[0, 0:1, :])
        e1 = jnp.exp(s1 - max1) / z
        for c in range(tb // HEAD_DIM):
            ls = slice(c * HEAD_DIM, (c + 1) * HEAD_DIM)
            thr_ref[hd, c] = thr[:, ls]
            e0_ref[hd, c] = e0[:, ls]
            e1_ref[hd, c] = e1[:, ls]
        return carry

    lax.fori_loop(0, N_HEADS, body, 0)


def _router(h2, wq_t, keys, layer, tb):
    t = h2.shape[1]
    out = pl.BlockSpec((N_HEADS, tb // HEAD_DIM, N_KEYS, HEAD_DIM), lambda i: (0, i, 0, 0))
    shp = jax.ShapeDtypeStruct((N_HEADS, t // HEAD_DIM, N_KEYS, HEAD_DIM), F32)
    return pl.pallas_call(
        _router_kernel,
        grid=(t // tb,),
        in_specs=[pl.BlockSpec((D_MODEL, tb), lambda i: (0, i)),
                  pl.BlockSpec((None, 2 * N_KEYS * N_HEADS, D_MODEL), lambda i: (layer, 0, 0)),
                  pl.BlockSpec((None, 2, N_KEYS, N_KEYS), lambda i: (layer, 0, 0, 0))],
        out_specs=[out, out, out],
        out_shape=[shp, shp, shp],
        scratch_shapes=[pltpu.VMEM((2 * N_KEYS * N_HEADS, tb), F32),
                        pltpu.VMEM((2, TOP_PAD, tb), F32),
                        pltpu.VMEM((CAND_ROWS, tb), F32)],
        compiler_params=_cparams(("parallel",)),
        name="peer_router",
    )(h2, wq_t, keys)


def _expert_kernel(h_ref, u_ref, vt_ref, thr_ref, e0_ref, e1_ref, x_ref, mod_ref, o_ref,
                   acc_scr, g_scr, act_scr, *, tile_e):
    j = pl.program_id(1)
    tb = h_ref.shape[1]
    n_sub = tile_e // N_KEYS
    n_lane = tb // HEAD_DIM

    @pl.when(j == 0)
    def _():
        acc_scr[...] = jnp.zeros_like(acc_scr)

    act = jnp.dot(u_ref[...], h_ref[...], preferred_element_type=F32)
    for c in range(n_lane):
        act_scr[c] = act[:, c * HEAD_DIM:(c + 1) * HEAD_DIM]

    i0 = pl.multiple_of(j * n_sub, n_sub)
    for c in range(n_lane):
        thr = [thr_ref[hd, c, pl.ds(i0, n_sub), :] for hd in range(N_HEADS)]
        e0 = [e0_ref[hd, c, pl.ds(i0, n_sub), :] for hd in range(N_HEADS)]
        for ii in range(n_sub):
            rs = slice(ii * N_KEYS, (ii + 1) * N_KEYS)
            wd = None
            for hd in range(N_HEADS):
                e1 = e1_ref[hd, c]
                w = jnp.where(e1 >= thr[hd][ii:ii + 1, :], e1 * e0[hd][ii:ii + 1, :], 0.0)
                wd = w if wd is None else wd + w
            x = act_scr[c, rs, :]
            g_scr[c, rs, :] = ((x * wd) * (1.0 + lax.erf(x * math.sqrt(0.5)))).astype(BF16)
    g = jnp.concatenate([g_scr[c] for c in range(n_lane)], axis=1)
    acc_scr[...] += jnp.dot(vt_ref[...], g, preferred_element_type=F32)

    @pl.when(j == pl.num_programs(1) - 1)
    def _():
        o_ref[...] = x_ref[...] + mod_ref[5:6, :] * acc_scr[...].T


def _experts(h2t, u_tab, vt_tab, routing, x, mod, layer, mod_row, tb, tile_e):
    t = h2t.shape[1]
    n_exp = u_tab.shape[1]
    n_lane = tb // HEAD_DIM
    rt = pl.BlockSpec((N_HEADS, n_lane, N_KEYS, HEAD_DIM), lambda i, j: (0, i, 0, 0))
    n_tiles = n_exp // tile_e
    return pl.pallas_call(
        functools.partial(_expert_kernel, tile_e=tile_e),
        grid=(t // tb, n_tiles),
        in_specs=[pl.BlockSpec((D_MODEL, tb), lambda i, j: (0, i)),
                  pl.BlockSpec((None, tile_e, D_MODEL), lambda i, j: (layer, j, 0)),
                  pl.BlockSpec((None, None, D_MODEL, tile_e), lambda i, j: (layer, j, 0, 0)),
                  rt, rt, rt,
                  pl.BlockSpec((tb, D_MODEL), lambda i, j: (i, 0)),
                  pl.BlockSpec((None, None, N_MOD, D_MODEL), lambda i, j: (layer, mod_row(i), 0, 0))],
        out_specs=pl.BlockSpec((tb, D_MODEL), lambda i, j: (i, 0)),
        out_shape=jax.ShapeDtypeStruct((t, D_MODEL), F32),
        scratch_shapes=[pltpu.VMEM((D_MODEL, tb), F32), pltpu.VMEM((n_lane, tile_e, HEAD_DIM), BF16),
                        pltpu.VMEM((n_lane, tile_e, HEAD_DIM), F32)],
        compiler_params=_cparams(("parallel", "arbitrary")),
        name="peer_experts",
    )(h2t, u_tab, vt_tab, *routing, x, mod)


def _final_norm_kernel(x_ref, g_ref, o_ref):
    o_ref[...] = _rms(x_ref[...]) * g_ref[...]


def _final_norm(x, g, tm):
    t = x.shape[0]
    row = pl.BlockSpec((tm, D_MODEL), lambda i: (i, 0))
    return pl.pallas_call(
        _final_norm_kernel, grid=(t // tm,),
        in_specs=[row, pl.BlockSpec((1, D_MODEL), lambda i: (0, 0))],
        out_specs=row, out_shape=jax.ShapeDtypeStruct((t, D_MODEL), F32),
        compiler_params=_cparams(("parallel",)), name="final_norm",
    )(x, g)


def _rope_tables(seq_len):
    rows = seq_len // GRID_W
    r = jnp.repeat(jnp.arange(rows, dtype=F32), GRID_W)
    col = jnp.tile(jnp.arange(GRID_W, dtype=F32), rows)
    nf = HEAD_DIM // 4
    inv = ROPE_BASE ** (-jnp.arange(nf, dtype=F32) / nf)
    ang = jnp.concatenate([r[:, None] * inv, col[:, None] * inv], axis=-1)
    cos, sin = jnp.cos(ang), jnp.sin(ang)
    return jnp.concatenate([cos, cos], axis=-1), jnp.concatenate([-sin, sin], axis=-1)


def _lanes(a):
    return jnp.broadcast_to(jnp.moveaxis(a, -1, 0)[..., None], (a.shape[-1],) + a.shape[:-1] + (HEAD_DIM,))


def kernel(x_prompt, x_sample, c, state_ret, state_lru, state_dn, c_ctx, w_mod, b_mod, norm1_g, norm2_g, w_in, ret_gamma_logit, ret_norm_g, lru_conv_w, lru_conv_b, lru_gate_w, lru_gate_b, lru_lambda, dn_conv_w, dn_a_log, dn_dt_bias, dn_norm_g, w_br, w_out, peer_w_q, peer_sub_keys, peer_u, peer_v, final_norm_g):
    n_ctx, l_ctx, _ = x_prompt.shape
    n_lat, l_lat, _ = x_sample.shape
    t_ctx, t_lat = n_ctx * l_ctx, n_lat * l_lat
    assert t_ctx % l_lat == 0
    tb = TOKEN_BLOCK
    assert l_lat % tb == 0 and t_ctx % tb == 0
    ctx_blocks, per_seq = t_ctx // tb, l_lat // tb

    def mod_row(i):
        return jnp.where(i < ctx_blocks, 0, 1 + (i - ctx_blocks) // per_seq)

    tm = MERGE_ROWS
    ctx_blocks_m, per_seq_m = t_ctx // tm, l_lat // tm

    def mod_row_m(i):
        return jnp.where(i < ctx_blocks_m, 0, 1 + (i - ctx_blocks_m) // per_seq_m)

    x = jnp.concatenate([x_prompt.reshape(t_ctx, D_MODEL), x_sample.reshape(t_lat, D_MODEL)], axis=0)
    n_cond = 16
    cond = jnp.zeros((n_cond, D_MODEL), F32).at[0].set(c_ctx).at[1:1 + n_lat].set(c)
    mods = _modulation(cond, w_mod, b_mod).reshape(DEPTH, n_cond, N_MOD, D_MODEL)

    w_in_b = w_in.astype(BF16)
    w_in_r = jnp.concatenate(
        [w_in_b[:, :, :N_MAIN], w_in_b[:, :, N_MAIN + N_SMALL:], w_in_b[:, :, N_MAIN:N_MAIN + N_SMALL],
         jnp.zeros((DEPTH, D_MODEL, HEAD_DIM - N_SMALL), BF16)], axis=-1)
    w_br_b, w_out_b = w_br.astype(BF16), w_out.astype(BF16)
    wq_t = jnp.swapaxes(peer_w_q, 1, 2).astype(BF16)
    keys_b = peer_sub_keys.astype(BF16)
    u_b = peer_u.astype(BF16)
    vt_b = jnp.swapaxes(peer_v.reshape(DEPTH, N_EXPERTS // EXPERT_TILE, EXPERT_TILE, D_MODEL), 2, 3).astype(BF16)
    gam = _lanes(ret_gamma_logit)
    lane_pad = jnp.zeros((DEPTH, 2, N_HEADS), F32)
    dn_ab = jnp.stack([dn_a_log, dn_dt_bias], axis=1)
    dn_par = jnp.concatenate([dn_ab[:, :, 0], lane_pad, dn_ab[:, :, 1], lane_pad,
                              jnp.zeros((DEPTH, 2, HEAD_DIM - N_SMALL), F32)], axis=-1)
    rope_tabs = _rope_tables(l_lat)
    zero_ret = jnp.zeros((n_ctx, 2, N_HEADS, HEAD_DIM, HEAD_DIM), F32)
    zero_lru = jnp.zeros((n_ctx, 2, D_MODEL), F32)

    row_off_lat = t_ctx // l_lat

    ret_states, lru_states, dn_states = [], [], []
    for l in range(DEPTH):
        proj = _in_proj(x, mods, norm1_g[l][None], w_in_r, l,
                        lambda i: jnp.where(i < row_off_lat, 0, 1 + i - row_off_lat), l_lat)

        ng_ret = ret_norm_g[l][:, None, :]
        o_ret_c, s_ret = _retention(proj, gam[:, l], ng_ret, zero_ret, None, None, n_ctx, l_ctx, 0, MIXER_HEADS)
        o_ret_l, _ = _retention(proj, gam[:, l], ng_ret, state_ret, l, rope_tabs, n_lat, l_lat, row_off_lat, MIXER_HEADS)

        lru_args = (lru_conv_w[l], lru_conv_b[l][None], lru_gate_w[l], lru_gate_b[l], lru_lambda[l])
        o_lru_c, s_lru = _rglru(proj, *lru_args, zero_lru, None, n_ctx, l_ctx, 0, _lru_blocks(l_ctx))
        o_lru_l, _ = _rglru(proj, *lru_args, state_lru, l, n_lat, l_lat, row_off_lat, _lru_blocks(l_lat))

        ng_dn = dn_norm_g[l][:, None, :]
        o_dn_c, s_dn = _deltanet(proj, dn_conv_w[l], dn_par[l], ng_dn, zero_ret, None, n_ctx, l_ctx, 0, MIXER_HEADS)
        o_dn_l, _ = _deltanet(proj, dn_conv_w[l], dn_par[l], ng_dn, state_dn, l, n_lat, l_lat, row_off_lat, MIXER_HEADS)

        branches = ((o_ret_c, o_ret_l), (o_lru_c, o_lru_l), (o_dn_c, o_dn_l))
        x, h2 = _merge(branches, proj, x, mods, norm2_g[l][None], w_br_b, w_out_b, l, mod_row_m, tm)
        routing = _router(h2, wq_t, keys_b, l, tb)
        x = _experts(h2, u_b, vt_b, routing, x, mods, l, mod_row, tb, EXPERT_TILE)

        ret_states.append(s_ret)
        lru_states.append(s_lru)
        dn_states.append(s_dn)

    y = _final_norm(x, final_norm_g[None], tb)
    y_prompt = y[:t_ctx].reshape(n_ctx, l_ctx, D_MODEL)
    y_sample = y[t_ctx:].reshape(n_lat, l_lat, D_MODEL)
    return (y_prompt, y_sample, jnp.stack(ret_states, axis=1), jnp.stack(lru_states, axis=1),
            jnp.stack(dn_states, axis=1))
```

```python
import functools
import math

import jax
import jax.numpy as jnp
from jax import lax
from jax.experimental import pallas as pl
from jax.experimental.pallas import tpu as pltpu

F32 = jnp.float32
BF16 = jnp.bfloat16
BRANCH_DTYPE = BF16

D_MODEL = 1024
DEPTH = 4
N_MOD = 6
EPS = 1e-6
GRID_W = 64
ROPE_BASE = 10000.0
N_HEADS = 8
HEAD_DIM = 128
RET_CHUNK = 128
DN_CHUNK = 64
DN_SUPER = 256
LRU_C = 8.0
N_KEYS = 128
PEER_TOPK = 16
N_EXPERTS = N_KEYS * N_KEYS
SUBLANES = 8
TOP_N = PEER_TOPK + 1
TOP_PAD = -(-TOP_N // SUBLANES) * SUBLANES
N_MULTI = TOP_N - SUBLANES
CAND_ROWS = TOP_PAD + N_MULTI * SUBLANES
assert TOP_N // 2 <= SUBLANES and TOP_N // (N_MULTI + 1) == 1
EXPERT_TILE = 2048
NEG_BIG = -3.0e38

COL_RQ, COL_RK, COL_RV, COL_RG = 0, 8, 16, 24
COL_LX, COL_LG = 32, 40
COL_DQ, COL_DK, COL_DV, COL_DZ = 48, 56, 64, 72
N_MAIN = 80 * 128
N_SMALL = 4 * N_HEADS
N_TAIL = 3 * D_MODEL + HEAD_DIM
COL_SMALL = 3 * D_MODEL // HEAD_DIM

TOKEN_BLOCK = 512
MERGE_ROWS = 256
IN_PROJ_COLS = 2048
MOD_COLS = 1536
MIXER_HEADS = 4
LRU_STEP_ROWS = 2048
VMEM_LIMIT = 48 * 1024 * 1024
VMEM_LIMIT_WIDE = 58 * 1024 * 1024


def _cparams(sem, vmem_limit=VMEM_LIMIT):
    return pltpu.CompilerParams(dimension_semantics=sem, vmem_limit_bytes=vmem_limit)


def _mm(a, b):
    return jnp.dot(a.astype(BF16), b.astype(BF16), preferred_element_type=F32)


def _mm_nt(a, b):
    return lax.dot_general(a.astype(BF16), b.astype(BF16), (((1,), (1,)), ((), ())),
                           preferred_element_type=F32)


def _mm_tn(a, b):
    return lax.dot_general(a.astype(BF16), b.astype(BF16), (((0,), (0,)), ((), ())),
                           preferred_element_type=F32)


def _softplus(x):
    return jnp.maximum(x, 0.0) + jnp.log1p(jnp.exp(-jnp.abs(x)))


def _silu(x):
    return x * jax.nn.sigmoid(x)


def _gelu(x):
    return 0.5 * x * (1.0 + lax.erf(x * math.sqrt(0.5)))


def _rms(x):
    return x * lax.rsqrt(jnp.mean(x * x, axis=-1, keepdims=True) + EPS)


def _shift_rows(x, s, row):
    n = x.shape[0]
    if s == 0:
        return x
    y = pltpu.roll(x, (-s) % n, 0)
    ok = (row + s >= 0) & (row + s < n)
    return jnp.where(ok, y, 0.0)


def _dw_conv(x, w, row):
    y = _shift_rows(x, -2, row) * w[0:1, :]
    y = y + _shift_rows(x, -1, row) * w[1:2, :]
    y = y + x * w[2:3, :]
    y = y + _shift_rows(x, 1, row) * w[3:4, :]
    return y


def _mod_kernel(c_ref, w_ref, b_ref, o_ref):
    c = c_ref[...]
    o_ref[...] = jnp.dot(_silu(c), w_ref[...], precision=lax.Precision.HIGHEST,
                         preferred_element_type=F32) + b_ref[...]


def _modulation(cond, w_mod, b_mod):
    n_rows = cond.shape[0]
    tn = MOD_COLS
    n_out = N_MOD * D_MODEL
    return pl.pallas_call(
        _mod_kernel,
        grid=(DEPTH, n_out // tn),
        in_specs=[pl.BlockSpec((n_rows, D_MODEL), lambda l, j: (0, 0)),
                  pl.BlockSpec((None, D_MODEL, tn), lambda l, j: (l, 0, j)),
                  pl.BlockSpec((None, 1, tn), lambda l, j: (l, 0, j))],
        out_specs=pl.BlockSpec((None, n_rows, tn), lambda l, j: (l, 0, j)),
        out_shape=jax.ShapeDtypeStruct((DEPTH, n_rows, n_out), F32),
        compiler_params=_cparams(("parallel", "parallel")),
        name="modulation",
    )(cond, w_mod, b_mod.reshape(DEPTH, 1, n_out))


def _in_proj_kernel(x_ref, mod_ref, g_ref, w_ref, o_ref, h_scr):
    @pl.when(pl.program_id(1) == 0)
    def _():
        y = _rms(x_ref[...]) * g_ref[...]
        h_scr[...] = (y * (1.0 + mod_ref[1:2, :]) + mod_ref[0:1, :]).astype(BF16)

    o_ref[...] = jnp.dot(h_scr[...], w_ref[...], preferred_element_type=F32)


def _in_proj(x, mod, norm_g, w, layer, mod_row, tm, n_out, tn):
    t = x.shape[0]
    assert n_out % tn == 0 and n_out <= w.shape[-1]
    return pl.pallas_call(
        _in_proj_kernel,
        grid=(t // tm, n_out // tn),
        in_specs=[pl.BlockSpec((tm, D_MODEL), lambda i, j: (i, 0)),
                  pl.BlockSpec((None, None, N_MOD, D_MODEL), lambda i, j: (layer, mod_row(i), 0, 0)),
                  pl.BlockSpec((1, D_MODEL), lambda i, j: (0, 0)),
                  pl.BlockSpec((None, D_MODEL, tn), lambda i, j: (layer, 0, j))],
        out_specs=pl.BlockSpec((tm, tn), lambda i, j: (i, j)),
        out_shape=jax.ShapeDtypeStruct((t, n_out), F32),
        scratch_shapes=[pltpu.VMEM((tm, D_MODEL), BF16)],
        compiler_params=_cparams(("parallel", "arbitrary")),
        name="in_proj",
    )(x, mod, norm_g, w)


def _ret_kernel(*refs, seq_len, rope, hg):
    if rope:
        (q_ref, k_ref, v_ref, g_ref, gam_ref, ng_ref, s0_ref, cs_ref, sn_ref,
         o_ref, so_ref, of_scr, ob_scr) = refs
    else:
        (q_ref, k_ref, v_ref, g_ref, gam_ref, ng_ref, s0_ref,
         o_ref, so_ref, of_scr, ob_scr) = refs
    c = RET_CHUNK
    n_chunks = seq_len // c
    r = lax.broadcasted_iota(jnp.int32, (c, HEAD_DIM), 0).astype(F32)
    ci = lax.broadcasted_iota(jnp.int32, (c, c), 0)
    si = lax.broadcasted_iota(jnp.int32, (c, c), 1)
    dmat = (ci - si).astype(F32)
    scale = HEAD_DIM ** -0.5

    chains = [(hh, d) for hh in range(hg) for d in range(2)]
    idx = range(len(chains))
    dec, qsc, ksc, gch = [], [], [], []
    for hh, d in chains:
        lg = -_softplus(-gam_ref[hh, d:d + 1, :])
        if d == 0:
            dec.append(jnp.where(dmat >= 0, jnp.exp(lg * jnp.maximum(dmat, 0.0)), 0.0))
            qsc.append(jnp.exp(lg * (r + 1.0)))
            ksc.append(jnp.exp(lg * (c - 1.0 - r)))
        else:
            dec.append(jnp.where(dmat <= 0, jnp.exp(lg * jnp.maximum(-dmat, 0.0)), 0.0))
            qsc.append(jnp.exp(lg * (c - r)))
            ksc.append(jnp.exp(lg * r))
        gch.append(jnp.exp(lg * c))

    s = [s0_ref[d, hh] for hh, d in chains]
    for stp in range(n_chunks):
        sl = [pl.ds((stp if d == 0 else n_chunks - 1 - stp) * c, c) for _, d in chains]
        cs = [slice(hh * HEAD_DIM, (hh + 1) * HEAD_DIM) for hh, _ in chains]
        q = [q_ref[sl[i], cs[i]] for i in idx]
        k = [k_ref[sl[i], cs[i]] * scale for i in idx]
        v = [v_ref[sl[i], cs[i]] for i in idx]
        if rope:
            cos = [cs_ref[sl[i], :] for i in idx]
            sin = [sn_ref[sl[i], :] for i in idx]
            q = [q[i] * cos[i] + pltpu.roll(q[i], HEAD_DIM // 2, 1) * sin[i] for i in idx]
            k = [k[i] * cos[i] + pltpu.roll(k[i], HEAD_DIM // 2, 1) * sin[i] for i in idx]
        sc = [_mm_nt(q[i], k[i]) for i in idx]
        qs = [_mm(q[i] * qsc[i], s[i]) for i in idx]
        kv = [_mm_tn(k[i] * ksc[i], v[i]) for i in idx]
        oi = [_mm(sc[i] * dec[i], v[i]) for i in idx]
        s = [s[i] * gch[i] + kv[i] for i in idx]
        for i in idx:
            if chains[i][1] == 0:
                of_scr[sl[i], cs[i]] = oi[i] + qs[i]
            else:
                ob_scr[sl[i], cs[i]] = oi[i] + qs[i]
    for i in idx:
        so_ref[chains[i][1], chains[i][0]] = s[i]
    tot = of_scr[...] + ob_scr[...]
    g = g_ref[...]
    for hh in range(hg):
        cs1 = slice(hh * HEAD_DIM, (hh + 1) * HEAD_DIM)
        o_ref[:, cs1] = (_rms(tot[:, cs1]) * ng_ref[hh] * _silu(g[:, cs1])).astype(o_ref.dtype)


def _retention(proj, gam, norm_g, s0, layer, rope_tabs, n_seq, seq_len, row_off, hg):
    rope = rope_tabs is not None
    wide = hg * HEAD_DIM

    def col(c0):
        return pl.BlockSpec((seq_len, wide), lambda b, g: (row_off + b, c0 // hg + g))

    in_specs = [col(COL_RQ), col(COL_RK), col(COL_RV), col(COL_RG),
                pl.BlockSpec((hg, 2, HEAD_DIM), lambda b, g: (g, 0, 0)),
                pl.BlockSpec((hg, 1, HEAD_DIM), lambda b, g: (g, 0, 0)),
                _state_spec(s0, layer, hg)]
    args = [proj, proj, proj, proj, gam, norm_g, s0]
    if rope:
        tab = pl.BlockSpec((seq_len, HEAD_DIM), lambda b, g: (0, 0))
        in_specs += [tab, tab]
        args += list(rope_tabs)
    return pl.pallas_call(
        functools.partial(_ret_kernel, seq_len=seq_len, rope=rope, hg=hg),
        grid=(n_seq, N_HEADS // hg),
        in_specs=in_specs,
        out_specs=[pl.BlockSpec((seq_len, wide), lambda b, g: (b, g)),
                   pl.BlockSpec((None, 2, hg, HEAD_DIM, HEAD_DIM), lambda b, g: (b, 0, g, 0, 0))],
        out_shape=[jax.ShapeDtypeStruct((n_seq * seq_len, D_MODEL), BRANCH_DTYPE),
                   jax.ShapeDtypeStruct((n_seq, 2, N_HEADS, HEAD_DIM, HEAD_DIM), F32)],
        scratch_shapes=[pltpu.VMEM((seq_len, wide), F32), pltpu.VMEM((seq_len, wide), F32)],
        compiler_params=_cparams(("parallel", "parallel")),
        name="retention_rope" if rope else "retention",
    )(*args)


def _lru_kernel(x_ref, gate_ref, cw_ref, cb_ref, gw_ref, gb_ref, lam_ref, s0_ref, o_ref, so_ref, *, seq_len, nb):
    n = seq_len
    row = lax.broadcasted_iota(jnp.int32, (n, nb * HEAD_DIM), 0)
    xc = _dw_conv(x_ref[...], cw_ref[...], row) + cb_ref[...]
    lam = lam_ref[...]

    def gate(d, which):
        cols = [_mm(xc[:, s * HEAD_DIM:(s + 1) * HEAD_DIM], gw_ref[d, which, s]) for s in range(nb)]
        return jax.nn.sigmoid(jnp.concatenate(cols, axis=1) + gb_ref[d, which:which + 1, :])

    hs = []
    for d in range(2):
        r_gate = gate(d, 0)
        i_gate = gate(d, 1)
        log_a = -LRU_C * r_gate * _softplus(-lam[d:d + 1, :])
        a = jnp.exp(log_a)
        u = jnp.sqrt(-jnp.tanh(log_a) * (1.0 + a * a)) * i_gate * xc
        step = 1
        while step < n:
            if d == 0:
                ok = row >= step
                sh = step
            else:
                ok = row < n - step
                sh = n - step
            a_sh = jnp.where(ok, pltpu.roll(a, sh, 0), 1.0)
            u_sh = jnp.where(ok, pltpu.roll(u, sh, 0), 0.0)
            u = a * u_sh + u
            a = a * a_sh
            step *= 2
        h = u + a * s0_ref[d:d + 1, :]
        hs.append(h)
        so_ref[d:d + 1, :] = h[n - 1:n, :] if d == 0 else h[0:1, :]
    o_ref[...] = ((hs[0] + hs[1]) * _gelu(gate_ref[...])).astype(o_ref.dtype)


def _lru_blocks(seq_len):
    return max(1, min(N_HEADS, LRU_STEP_ROWS // seq_len))


def _rglru(proj, conv_w, conv_b, gate_w, gate_b, lam, s0, layer, n_seq, seq_len, row_off, nb):
    wide = nb * HEAD_DIM
    if layer is None:
        s0_spec = pl.BlockSpec((None, 2, wide), lambda b, n: (b, 0, n))
    else:
        s0_spec = pl.BlockSpec((None, None, 2, wide), lambda b, n: (b, layer, 0, n))
    return pl.pallas_call(
        functools.partial(_lru_kernel, seq_len=seq_len, nb=nb),
        grid=(n_seq, N_HEADS // nb),
        in_specs=[pl.BlockSpec((seq_len, wide), lambda b, n: (row_off + b, COL_LX // nb + n)),
                  pl.BlockSpec((seq_len, wide), lambda b, n: (row_off + b, COL_LG // nb + n)),
                  pl.BlockSpec((4, wide), lambda b, n: (0, n)),
                  pl.BlockSpec((1, wide), lambda b, n: (0, n)),
                  pl.BlockSpec((2, 2, nb, HEAD_DIM, HEAD_DIM), lambda b, n: (0, 0, n, 0, 0)),
                  pl.BlockSpec((2, 2, wide), lambda b, n: (0, 0, n)),
                  pl.BlockSpec((2, wide), lambda b, n: (0, n)),
                  s0_spec],
        out_specs=[pl.BlockSpec((seq_len, wide), lambda b, n: (b, n)),
                   pl.BlockSpec((None, 2, wide), lambda b, n: (b, 0, n))],
        out_shape=[jax.ShapeDtypeStruct((n_seq * seq_len, D_MODEL), BRANCH_DTYPE),
                   jax.ShapeDtypeStruct((n_seq, 2, D_MODEL), F32)],
        compiler_params=_cparams(("parallel", "parallel")),
        name="rglru",
    )(proj, proj, conv_w, conv_b, gate_w, gate_b, lam, s0)


def _dn_kernel(q_ref, k_ref, v_ref, z_ref, sm_ref, cwq_ref, cwk_ref, cwv_ref, par_ref, ng_ref, s0_ref,
               o_ref, so_ref, q_scr, k_scr, v_scr, c_scr, b_scr, of_scr, ob_scr, *, seq_len, hg):
    n = seq_len
    cc = DN_CHUNK
    sc = DN_SUPER
    n_super = n // sc
    head0 = pl.program_id(1) * hg
    roww = lax.broadcasted_iota(jnp.int32, (n, hg * HEAD_DIM), 0)
    row = lax.broadcasted_iota(jnp.int32, (n, HEAD_DIM), 0)
    lane = lax.broadcasted_iota(jnp.int32, (n, HEAD_DIM), 1)
    pos = row & (cc - 1)

    xq = _silu(_dw_conv(q_ref[...], cwq_ref[...], roww))
    xk = _silu(_dw_conv(k_ref[...], cwk_ref[...], roww))
    v_scr[...] = _silu(_dw_conv(v_ref[...], cwv_ref[...], roww))
    small = sm_ref[...]
    par = par_ref[...]
    g_all = -jnp.exp(par[0:1, :]) * _softplus(small + par[1:2, :])
    beta_all = jax.nn.sigmoid(small)
    cum_all = [g_all, g_all]
    step = 1
    while step < cc:
        cum_all[0] = cum_all[0] + jnp.where(pos >= step, pltpu.roll(cum_all[0], step, 0), 0.0)
        cum_all[1] = cum_all[1] + jnp.where(pos < cc - step, pltpu.roll(cum_all[1], n - step, 0), 0.0)
        step *= 2
    for hh in range(hg):
        cs = slice(hh * HEAD_DIM, (hh + 1) * HEAD_DIM)
        xqh, xkh = xq[:, cs], xk[:, cs]
        q_scr[:, cs] = xqh * lax.rsqrt(jnp.sum(xqh * xqh, axis=-1, keepdims=True) + EPS) * (HEAD_DIM ** -0.5)
        k_scr[:, cs] = xkh * lax.rsqrt(jnp.sum(xkh * xkh, axis=-1, keepdims=True) + EPS)
        head = head0 + hh
        for d in range(2):
            a_lane = lane == 2 * N_HEADS * d + head
            b_lane = lane == 2 * N_HEADS * d + N_HEADS + head
            cum = jnp.sum(jnp.where(a_lane, cum_all[d], 0.0), axis=-1, keepdims=True)
            beta = jnp.sum(jnp.where(b_lane, beta_all, 0.0), axis=-1, keepdims=True)
            c_scr[d, hh] = jnp.broadcast_to(cum, (n, HEAD_DIM))
            b_scr[d, hh] = jnp.broadcast_to(beta, (n, HEAD_DIM))

    ri = lax.broadcasted_iota(jnp.int32, (sc, sc), 0)
    cj = lax.broadcasted_iota(jnp.int32, (sc, sc), 1)
    sh = cc.bit_length() - 1
    same = (ri >> sh) == (cj >> sh)
    incl = (same & (ri >= cj), same & (ri <= cj))
    strict = (same & (ri > cj), same & (ri < cj))
    eye = jnp.where(ri == cj, 1.0, 0.0)
    base = SUBLANES.bit_length() - 1
    level = [(ri >> base) == (cj >> base)]
    for b in range(base + 1, sh + 1):
        level.append(((ri >> b) == (cj >> b)) & ((ri >> (b - 1)) != (cj >> (b - 1))))

    n_ch = sc // cc

    def super_chunks(chains):
        idx = range(len(chains))
        dd = [c[0] for c in chains]
        sl = [pl.ds(c[2], sc) for c in chains]
        cs = [slice(c[1] * HEAD_DIM, (c[1] + 1) * HEAD_DIM) for c in chains]
        q = [q_scr[sl[c], cs[c]] for c in idx]
        k = [k_scr[sl[c], cs[c]] for c in idx]
        v = [v_scr[sl[c], cs[c]] for c in idx]
        cum = [c_scr[dd[c], chains[c][1], sl[c], :] for c in idx]
        beta = [b_scr[dd[c], chains[c][1], sl[c], :] for c in idx]
        kk = [_mm_nt(k[c], k[c]) for c in idx]
        qk = [_mm_nt(q[c], k[c]) for c in idx]
        decay, x, attn = [], [], []
        for c in idx:
            cb = jnp.concatenate([cum[c], cum[c]], axis=1)
            diff = cb - cb.T
            dec = jnp.where(incl[dd[c]], jnp.exp(diff), 0.0)
            nb = jnp.concatenate([-beta[c], -beta[c]], axis=1)
            x.append(jnp.where(strict[dd[c]], kk[c] * nb * dec, 0.0))
            attn.append(qk[c] * dec)
        xp = [jnp.where(level[0], x[c], 0.0) for c in idx]
        p = [eye + xp[c] for c in idx]
        for _ in range(base - 1):
            xp = [_mm(xp[c], xp[c]) for c in idx]
            pm = [_mm(p[c], xp[c]) for c in idx]
            p = [p[c] + pm[c] for c in idx]
        for lv in range(1, len(level)):
            t1 = [_mm(jnp.where(level[lv], x[c], 0.0), p[c]) for c in idx]
            t2 = [_mm(p[c], t1[c]) for c in idx]
            p = [p[c] + t2[c] for c in idx]
        rhs = [jnp.concatenate([v[c] * beta[c], k[c] * beta[c] * jnp.exp(cum[c])], axis=1) for c in idx]
        sol = [_mm(p[c], rhs[c]) for c in idx]
        qd = [q[c] * jnp.exp(cum[c]) for c in idx]
        s = [so_ref[dd[c], chains[c][1]] for c in idx]
        v_new = [[None] * n_ch for _ in idx]
        o_inter = [[None] * n_ch for _ in idx]
        for stp in range(n_ch):
            ch = [stp if dd[c] == 0 else n_ch - 1 - stp for c in idx]
            rs = [slice(ch[c] * cc, (ch[c] + 1) * cc) for c in idx]
            ws = [_mm(sol[c][rs[c], HEAD_DIM:], s[c]) for c in idx]
            for c in idx:
                o_inter[c][ch[c]] = _mm(qd[c][rs[c], :], s[c])
            kt = []
            for c in idx:
                cum_c = cum[c][rs[c], :]
                tot = cum_c[cc - 1:cc, :] if dd[c] == 0 else cum_c[0:1, :]
                v_new[c][ch[c]] = sol[c][rs[c], :HEAD_DIM] - ws[c]
                kt.append((k[c][rs[c], :] * jnp.exp(tot - cum_c), jnp.exp(tot)))
            upd = [_mm_tn(kt[c][0], v_new[c][ch[c]]) for c in idx]
            s = [s[c] * kt[c][1] + upd[c] for c in idx]
        oi = [_mm(attn[c], jnp.concatenate(v_new[c], axis=0)) for c in idx]
        for c in idx:
            o = jnp.concatenate(o_inter[c], axis=0) + oi[c]
            if dd[c] == 0:
                of_scr[sl[c], cs[c]] = o
            else:
                ob_scr[sl[c], cs[c]] = o
            so_ref[dd[c], chains[c][1]] = s[c]

    so_ref[...] = s0_ref[...]

    def body(i, carry):
        fwd = pl.multiple_of(i * sc, sc)
        bwd = pl.multiple_of((n_super - 1 - i) * sc, sc)
        super_chunks([(d, hh, fwd if d == 0 else bwd) for hh in range(hg) for d in range(2)])
        return carry

    lax.fori_loop(0, n_super, body, 0)
    tot = of_scr[...] + ob_scr[...]
    z = z_ref[...]
    for hh in range(hg):
        cs = slice(hh * HEAD_DIM, (hh + 1) * HEAD_DIM)
        o_ref[:, cs] = (_rms(tot[:, cs]) * ng_ref[hh] * _silu(z[:, cs])).astype(o_ref.dtype)


def _state_spec(s0, layer, hg):
    if layer is None:
        return pl.BlockSpec((None, 2, hg, HEAD_DIM, HEAD_DIM), lambda b, g: (b, 0, g, 0, 0))
    return pl.BlockSpec((None, None, 2, hg, HEAD_DIM, HEAD_DIM), lambda b, g: (b, layer, 0, g, 0, 0))


def _deltanet(proj, tail, conv_w, par, norm_g, s0, layer, n_seq, seq_len, row_off, hg):
    wide = hg * HEAD_DIM

    def col(c0):
        return pl.BlockSpec((seq_len, wide), lambda b, g: (row_off + b, c0 // hg + g))

    def cw(c0):
        return pl.BlockSpec((4, wide), lambda b, g: (0, c0 // hg + g))

    vm = functools.partial(pltpu.VMEM, dtype=F32)
    return pl.pallas_call(
        functools.partial(_dn_kernel, seq_len=seq_len, hg=hg),
        grid=(n_seq, N_HEADS // hg),
        in_specs=[col(COL_DQ), col(COL_DK), col(COL_DV), col(COL_DZ),
                  pl.BlockSpec((seq_len, HEAD_DIM), lambda b, g: (row_off + b, COL_SMALL)),
                  cw(0), cw(N_HEADS), cw(2 * N_HEADS),
                  pl.BlockSpec((2, HEAD_DIM), lambda b, g: (0, 0)),
                  pl.BlockSpec((hg, 1, HEAD_DIM), lambda b, g: (g, 0, 0)),
                  _state_spec(s0, layer, hg)],
        out_specs=[pl.BlockSpec((seq_len, wide), lambda b, g: (b, g)),
                   pl.BlockSpec((None, 2, hg, HEAD_DIM, HEAD_DIM), lambda b, g: (b, 0, g, 0, 0))],
        out_shape=[jax.ShapeDtypeStruct((n_seq * seq_len, D_MODEL), BRANCH_DTYPE),
                   jax.ShapeDtypeStruct((n_seq, 2, N_HEADS, HEAD_DIM, HEAD_DIM), F32)],
        scratch_shapes=[vm((seq_len, wide)), vm((seq_len, wide)), vm((seq_len, wide)),
                        vm((2, hg, seq_len, HEAD_DIM)), vm((2, hg, seq_len, HEAD_DIM)),
                        vm((seq_len, wide)), vm((seq_len, wide))],
        compiler_params=_cparams(("parallel", "parallel"),
                                 VMEM_LIMIT_WIDE if seq_len * hg > 2048 else VMEM_LIMIT),
        name="deltanet",
    )(proj, proj, proj, proj, tail, conv_w, conv_w, conv_w, par, norm_g, s0)


def _merge_kernel(retc_ref, retl_ref, lruc_ref, lrul_ref, dnc_ref, dnl_ref, g0_ref, g1_ref, g2_ref, x_ref, mod_ref,
                  ng_ref, wbr_ref, wout_ref, xo_ref, h_ref, *, ctx_blocks):
    is_ctx = pl.program_id(0) < ctx_blocks
    ret = jnp.where(is_ctx, retc_ref[...], retl_ref[...])
    lru = jnp.where(is_ctx, lruc_ref[...], lrul_ref[...])
    dn = jnp.where(is_ctx, dnc_ref[...], dnl_ref[...])
    merged = jax.nn.sigmoid(g0_ref[...]) * _mm(ret, wbr_ref[0])
    merged = merged + jax.nn.sigmoid(g1_ref[...]) * _mm(lru, wbr_ref[1])
    merged = merged + jax.nn.sigmoid(g2_ref[...]) * _mm(dn, wbr_ref[2])
    x = x_ref[...] + mod_ref[2:3, :] * _mm(merged, wout_ref[...])
    xo_ref[...] = x
    h_ref[...] = (_rms(x) * ng_ref[...] * (1.0 + mod_ref[4:5, :]) + mod_ref[3:4, :]).T.astype(BF16)


def _merge(branches, tail, x, mod, norm_g, w_br, w_out, layer, mod_row, tm):
    t = x.shape[0]
    ctx_blocks = branches[0][0].shape[0] // tm
    row = pl.BlockSpec((tm, D_MODEL), lambda i: (i, 0))
    row_c = pl.BlockSpec((tm, D_MODEL), lambda i: (jnp.minimum(i, ctx_blocks - 1), 0))
    row_l = pl.BlockSpec((tm, D_MODEL), lambda i: (jnp.maximum(i - ctx_blocks, 0), 0))

    def gate(k):
        return pl.BlockSpec((tm, D_MODEL), lambda i: (i, k))

    return pl.pallas_call(
        functools.partial(_merge_kernel, ctx_blocks=ctx_blocks),
        grid=(t // tm,),
        in_specs=[row_c, row_l, row_c, row_l, row_c, row_l, gate(0), gate(1), gate(2), row,
                  pl.BlockSpec((None, None, N_MOD, D_MODEL), lambda i: (layer, mod_row(i), 0, 0)),
                  pl.BlockSpec((1, D_MODEL), lambda i: (0, 0)),
                  pl.BlockSpec((None, 3, D_MODEL, D_MODEL), lambda i: (layer, 0, 0, 0)),
                  pl.BlockSpec((None, D_MODEL, D_MODEL), lambda i: (layer, 0, 0))],
        out_specs=[row, pl.BlockSpec((D_MODEL, tm), lambda i: (0, i))],
        out_shape=[jax.ShapeDtypeStruct((t, D_MODEL), F32), jax.ShapeDtypeStruct((D_MODEL, t), BF16)],
        compiler_params=_cparams(("parallel",)),
        name="merge",
    )(*branches[0], *branches[1], *branches[2], tail, tail, tail, x, mod, norm_g, w_br, w_out)


def _router_kernel(h_ref, wq_ref, keys_ref, thr_ref, e0_ref, e1_ref, q_scr, top_scr, cand_scr):
    tb = h_ref.shape[1]
    q_scr[...] = jnp.dot(wq_ref[...], h_ref[...], preferred_element_type=F32)

    top_scr[...] = jnp.full(top_scr.shape, NEG_BIG, F32)
    sub_row = lax.broadcasted_iota(jnp.int32, (SUBLANES, tb), 0)

    def top_values(x, dst):
        for kk in range(TOP_N):
            m = jnp.max(x, axis=0, keepdims=True)
            top_scr[dst, kk:kk + 1, :] = m
            x = jnp.where(x >= m, NEG_BIG, x)

    def body(hd, carry):
        base = pl.multiple_of(hd * 2 * N_KEYS, 2 * N_KEYS)
        s0 = _mm(keys_ref[0], q_scr[pl.ds(base, N_KEYS), :])
        s1 = _mm(keys_ref[1], q_scr[pl.ds(base + N_KEYS, N_KEYS), :])
        top_values(s0, 0)
        top_values(s1, 1)
        cand_scr[0:TOP_PAD, :] = top_scr[0, 0:1, :] + top_scr[1]
        a1 = top_scr[1, 0:SUBLANES, :]
        for p in range(1, N_MULTI):
            keep = sub_row < TOP_N // (p + 1)
            cand_scr[TOP_PAD + (p - 1) * SUBLANES:TOP_PAD + p * SUBLANES, :] = jnp.where(
                keep, top_scr[0, p:p + 1, :] + a1, NEG_BIG)
        cand_scr[CAND_ROWS - SUBLANES:CAND_ROWS, :] = top_scr[0, N_MULTI:TOP_N, :] + top_scr[1, 0:1, :]
        x = cand_scr[...]
        m0 = jnp.max(x, axis=0, keepdims=True)
        z = jnp.zeros_like(m0)
        m = m0
        for kk in range(PEER_TOPK):
            if kk > 0:
                x = jnp.where(x >= m, NEG_BIG, x)
                m = jnp.max(x, axis=0, keepdims=True)
            z = z + jnp.exp(m - m0)
        m_next = jnp.max(jnp.where(x >= m, NEG_BIG, x), axis=0, keepdims=True)
        tau = 0.5 * (m + m_next)
        max1 = top_scr[1, 0:1, :]
        thr = jnp.exp(tau - s0 - max1) / z
        e0 = 0.5 * jnp.exp(s0 - top_scr[0, 0:1, :])
        e1 = jnp.exp(s1 - max1) / z
        for c in range(tb // HEAD_DIM):
            ls = slice(c * HEAD_DIM, (c + 1) * HEAD_DIM)
            thr_ref[hd, c] = thr[:, ls]
            e0_ref[hd, c] = e0[:, ls]
            e1_ref[hd, c] = e1[:, ls]
        return carry

    lax.fori_loop(0, N_HEADS, body, 0)


def _router(h2, wq_t, keys, layer, tb):
    t = h2.shape[1]
    out = pl.BlockSpec((N_HEADS, tb // HEAD_DIM, N_KEYS, HEAD_DIM), lambda i: (0, i, 0, 0))
    shp = jax.ShapeDtypeStruct((N_HEADS, t // HEAD_DIM, N_KEYS, HEAD_DIM), F32)
    return pl.pallas_call(
        _router_kernel,
        grid=(t // tb,),
        in_specs=[pl.BlockSpec((D_MODEL, tb), lambda i: (0, i)),
                  pl.BlockSpec((None, 2 * N_KEYS * N_HEADS, D_MODEL), lambda i: (layer, 0, 0)),
                  pl.BlockSpec((None, 2, N_KEYS, N_KEYS), lambda i: (layer, 0, 0, 0))],
        out_specs=[out, out, out],
        out_shape=[shp, shp, shp],
        scratch_shapes=[pltpu.VMEM((2 * N_KEYS * N_HEADS, tb), F32),
                        pltpu.VMEM((2, TOP_PAD, tb), F32),
                        pltpu.VMEM((CAND_ROWS, tb), F32)],
        compiler_params=_cparams(("parallel",)),
        name="peer_router",
    )(h2, wq_t, keys)


def _expert_kernel(h_ref, u_ref, vt_ref, thr_ref, e0_ref, e1_ref, x_ref, mod_ref, o_ref,
                   acc_scr, g_scr, act_scr, *, tile_e):
    j = pl.program_id(1)
    tb = h_ref.shape[1]
    n_sub = tile_e // N_KEYS
    n_lane = tb // HEAD_DIM

    @pl.when(j == 0)
    def _():
        acc_scr[...] = jnp.zeros_like(acc_scr)

    act = jnp.dot(u_ref[...], h_ref[...], preferred_element_type=F32)
    for c in range(n_lane):
        act_scr[c] = act[:, c * HEAD_DIM:(c + 1) * HEAD_DIM]

    i0 = pl.multiple_of(j * n_sub, n_sub)
    for c in range(n_lane):
        thr = [thr_ref[hd, c, pl.ds(i0, n_sub), :] for hd in range(N_HEADS)]
        e0 = [e0_ref[hd, c, pl.ds(i0, n_sub), :] for hd in range(N_HEADS)]
        for ii in range(n_sub):
            rs = slice(ii * N_KEYS, (ii + 1) * N_KEYS)
            wd = None
            for hd in range(N_HEADS):
                e1 = e1_ref[hd, c]
                w = jnp.where(e1 >= thr[hd][ii:ii + 1, :], e1 * e0[hd][ii:ii + 1, :], 0.0)
                wd = w if wd is None else wd + w
            x = act_scr[c, rs, :]
            g_scr[c, rs, :] = ((x * wd) * (1.0 + lax.erf(x * math.sqrt(0.5)))).astype(BF16)
    g = jnp.concatenate([g_scr[c] for c in range(n_lane)], axis=1)
    acc_scr[...] += jnp.dot(vt_ref[...], g, preferred_element_type=F32)

    @pl.when(j == pl.num_programs(1) - 1)
    def _():
        o_ref[...] = x_ref[...] + mod_ref[5:6, :] * acc_scr[...].T


def _experts(h2t, u_tab, vt_tab, routing, x, mod, layer, mod_row, tb, tile_e):
    t = h2t.shape[1]
    n_exp = u_tab.shape[1]
    n_lane = tb // HEAD_DIM
    rt = pl.BlockSpec((N_HEADS, n_lane, N_KEYS, HEAD_DIM), lambda i, j: (0, i, 0, 0))
    n_tiles = n_exp // tile_e
    return pl.pallas_call(
        functools.partial(_expert_kernel, tile_e=tile_e),
        grid=(t // tb, n_tiles),
        in_specs=[pl.BlockSpec((D_MODEL, tb), lambda i, j: (0, i)),
                  pl.BlockSpec((None, tile_e, D_MODEL), lambda i, j: (layer, j, 0)),
                  pl.BlockSpec((None, None, D_MODEL, tile_e), lambda i, j: (layer, j, 0, 0)),
                  rt, rt, rt,
                  pl.BlockSpec((tb, D_MODEL), lambda i, j: (i, 0)),
                  pl.BlockSpec((None, None, N_MOD, D_MODEL), lambda i, j: (layer, mod_row(i), 0, 0))],
        out_specs=pl.BlockSpec((tb, D_MODEL), lambda i, j: (i, 0)),
        out_shape=jax.ShapeDtypeStruct((t, D_MODEL), F32),
        scratch_shapes=[pltpu.VMEM((D_MODEL, tb), F32), pltpu.VMEM((n_lane, tile_e, HEAD_DIM), BF16),
                        pltpu.VMEM((n_lane, tile_e, HEAD_DIM), F32)],
        compiler_params=_cparams(("parallel", "arbitrary")),
        name="peer_experts",
    )(h2t, u_tab, vt_tab, *routing, x, mod)


def _final_norm_kernel(x_ref, g_ref, o_ref):
    o_ref[...] = _rms(x_ref[...]) * g_ref[...]


def _final_norm(x, g, tm):
    t = x.shape[0]
    row = pl.BlockSpec((tm, D_MODEL), lambda i: (i, 0))
    return pl.pallas_call(
        _final_norm_kernel, grid=(t // tm,),
        in_specs=[row, pl.BlockSpec((1, D_MODEL), lambda i: (0, 0))],
        out_specs=row, out_shape=jax.ShapeDtypeStruct((t, D_MODEL), F32),
        compiler_params=_cparams(("parallel",)), name="final_norm",
    )(x, g)


def _rope_tables(seq_len):
    rows = seq_len // GRID_W
    r = jnp.repeat(jnp.arange(rows, dtype=F32), GRID_W)
    col = jnp.tile(jnp.arange(GRID_W, dtype=F32), rows)
    nf = HEAD_DIM // 4
    inv = ROPE_BASE ** (-jnp.arange(nf, dtype=F32) / nf)
    ang = jnp.concatenate([r[:, None] * inv, col[:, None] * inv], axis=-1)
    cos, sin = jnp.cos(ang), jnp.sin(ang)
    return jnp.concatenate([cos, cos], axis=-1), jnp.concatenate([-sin, sin], axis=-1)


def _lanes(a):
    return jnp.broadcast_to(jnp.moveaxis(a, -1, 0)[..., None], (a.shape[-1],) + a.shape[:-1] + (HEAD_DIM,))


def kernel(x_prompt, x_sample, c, state_ret, state_lru, state_dn, c_ctx, w_mod, b_mod, norm1_g, norm2_g, w_in, ret_gamma_logit, ret_norm_g, lru_conv_w, lru_conv_b, lru_gate_w, lru_gate_b, lru_lambda, dn_conv_w, dn_a_log, dn_dt_bias, dn_norm_g, w_br, w_out, peer_w_q, peer_sub_keys, peer_u, peer_v, final_norm_g):
    n_ctx, l_ctx, _ = x_prompt.shape
    n_lat, l_lat, _ = x_sample.shape
    t_ctx, t_lat = n_ctx * l_ctx, n_lat * l_lat
    assert t_ctx % l_lat == 0
    tb = TOKEN_BLOCK
    assert l_lat % tb == 0 and t_ctx % tb == 0
    ctx_blocks, per_seq = t_ctx // tb, l_lat // tb

    def mod_row(i):
        return jnp.where(i < ctx_blocks, 0, 1 + (i - ctx_blocks) // per_seq)

    tm = MERGE_ROWS
    ctx_blocks_m, per_seq_m = t_ctx // tm, l_lat // tm

    def mod_row_m(i):
        return jnp.where(i < ctx_blocks_m, 0, 1 + (i - ctx_blocks_m) // per_seq_m)

    x = jnp.concatenate([x_prompt.reshape(t_ctx, D_MODEL), x_sample.reshape(t_lat, D_MODEL)], axis=0)
    n_cond = 16
    cond = jnp.zeros((n_cond, D_MODEL), F32).at[0].set(c_ctx).at[1:1 + n_lat].set(c)
    mods = _modulation(cond, w_mod, b_mod).reshape(DEPTH, n_cond, N_MOD, D_MODEL)

    w_in_b = w_in.astype(BF16)
    w_tail = jnp.concatenate(
        [w_in_b[:, :, N_MAIN + N_SMALL:], w_in_b[:, :, N_MAIN:N_MAIN + N_SMALL],
         jnp.zeros((DEPTH, D_MODEL, HEAD_DIM - N_SMALL), BF16)], axis=-1)
    w_br_b, w_out_b = w_br.astype(BF16), w_out.astype(BF16)
    wq_t = jnp.swapaxes(peer_w_q, 1, 2).astype(BF16)
    keys_b = peer_sub_keys.astype(BF16)
    u_b = peer_u.astype(BF16)
    vt_b = jnp.swapaxes(peer_v.reshape(DEPTH, N_EXPERTS // EXPERT_TILE, EXPERT_TILE, D_MODEL), 2, 3).astype(BF16)
    gam = _lanes(ret_gamma_logit)
    lane_pad = jnp.zeros((DEPTH, 2, N_HEADS), F32)
    dn_ab = jnp.stack([dn_a_log, dn_dt_bias], axis=1)
    dn_par = jnp.concatenate([dn_ab[:, :, 0], lane_pad, dn_ab[:, :, 1], lane_pad,
                              jnp.zeros((DEPTH, 2, HEAD_DIM - N_SMALL), F32)], axis=-1)
    rope_tabs = _rope_tables(l_lat)
    zero_ret = jnp.zeros((n_ctx, 2, N_HEADS, HEAD_DIM, HEAD_DIM), F32)
    zero_lru = jnp.zeros((n_ctx, 2, D_MODEL), F32)

    row_off_lat = t_ctx // l_lat

    ret_states, lru_states, dn_states = [], [], []
    for l in range(DEPTH):
        proj = _in_proj(x, mods, norm1_g[l][None], w_in_b, l,
                        lambda i: jnp.where(i < row_off_lat, 0, 1 + i - row_off_lat), l_lat, N_MAIN, IN_PROJ_COLS)
        tail = _in_proj(x, mods, norm1_g[l][None], w_tail, l, mod_row, tb, N_TAIL, N_TAIL)

        ng_ret = ret_norm_g[l][:, None, :]
        o_ret_c, s_ret = _retention(proj, gam[:, l], ng_ret, zero_ret, None, None, n_ctx, l_ctx, 0, MIXER_HEADS)
        o_ret_l, _ = _retention(proj, gam[:, l], ng_ret, state_ret, l, rope_tabs, n_lat, l_lat, row_off_lat, MIXER_HEADS)

        lru_args = (lru_conv_w[l], lru_conv_b[l][None], lru_gate_w[l], lru_gate_b[l], lru_lambda[l])
        o_lru_c, s_lru = _rglru(proj, *lru_args, zero_lru, None, n_ctx, l_ctx, 0, _lru_blocks(l_ctx))
        o_lru_l, _ = _rglru(proj, *lru_args, state_lru, l, n_lat, l_lat, row_off_lat, _lru_blocks(l_lat))

        ng_dn = dn_norm_g[l][:, None, :]
        dn_args = (proj, tail, dn_conv_w[l], dn_par[l], ng_dn)
        o_dn_c, s_dn = _deltanet(*dn_args, zero_ret, None, n_ctx, l_ctx, 0, MIXER_HEADS)
        o_dn_l, _ = _deltanet(*dn_args, state_dn, l, n_lat, l_lat, row_off_lat, MIXER_HEADS)

        branches = ((o_ret_c, o_ret_l), (o_lru_c, o_lru_l), (o_dn_c, o_dn_l))
        x, h2 = _merge(branches, tail, x, mods, norm2_g[l][None], w_br_b, w_out_b, l, mod_row_m, tm)
        routing = _router(h2, wq_t, keys_b, l, tb)
        x = _experts(h2, u_b, vt_b, routing, x, mods, l, mod_row, tb, EXPERT_TILE)

        ret_states.append(s_ret)
        lru_states.append(s_lru)
        dn_states.append(s_dn)

    y = _final_norm(x, final_norm_g[None], tb)
    y_prompt = y[:t_ctx].reshape(n_ctx, l_ctx, D_MODEL)
    y_sample = y[t_ctx:].reshape(n_lat, l_lat, D_MODEL)
    return (y_prompt, y_sample, jnp.stack(ret_states, axis=1), jnp.stack(lru_states, axis=1),
            jnp.stack(dn_states, axis=1))
```

```python
import functools
import math

import jax
import jax.numpy as jnp
from jax import lax
from jax.experimental import pallas as pl
from jax.experimental.pallas import tpu as pltpu

F32 = jnp.float32
BF16 = jnp.bfloat16
BRANCH_DTYPE = BF16

D_MODEL = 1024
DEPTH = 4
N_MOD = 6
EPS = 1e-6
GRID_W = 64
ROPE_BASE = 10000.0
N_HEADS = 8
HEAD_DIM = 128
RET_CHUNK = 128
DN_CHUNK = 64
DN_SUPER = 256
LRU_C = 8.0
N_KEYS = 128
PEER_TOPK = 16
N_EXPERTS = N_KEYS * N_KEYS
SUBLANES = 8
TOP_N = PEER_TOPK + 1
TOP_PAD = -(-TOP_N // SUBLANES) * SUBLANES
N_MULTI = TOP_N - SUBLANES
CAND_ROWS = TOP_PAD + N_MULTI * SUBLANES
assert TOP_N // 2 <= SUBLANES and TOP_N // (N_MULTI + 1) == 1
EXPERT_TILE = 2048
NEG_BIG = -3.0e38

COL_RQ, COL_RK, COL_RV, COL_RG = 0, 8, 16, 24
COL_LX, COL_LG = 32, 40
COL_DQ, COL_DK, COL_DV, COL_DZ = 48, 56, 64, 72
N_MAIN = 80 * 128
N_SMALL = 4 * N_HEADS
N_TAIL = 3 * D_MODEL + HEAD_DIM
COL_SMALL = 3 * D_MODEL // HEAD_DIM

TOKEN_BLOCK = 512
MERGE_ROWS = 256
IN_PROJ_COLS = 2048
MOD_COLS = 1536
MIXER_HEADS = 4
LRU_STEP_ROWS = 2048
VMEM_LIMIT = 48 * 1024 * 1024
VMEM_LIMIT_WIDE = 58 * 1024 * 1024


def _cparams(sem, vmem_limit=VMEM_LIMIT):
    return pltpu.CompilerParams(dimension_semantics=sem, vmem_limit_bytes=vmem_limit)


def _mm(a, b):
    return jnp.dot(a.astype(BF16), b.astype(BF16), preferred_element_type=F32)


def _mm_nt(a, b):
    return lax.dot_general(a.astype(BF16), b.astype(BF16), (((1,), (1,)), ((), ())),
                           preferred_element_type=F32)


def _mm_tn(a, b):
    return lax.dot_general(a.astype(BF16), b.astype(BF16), (((0,), (0,)), ((), ())),
                           preferred_element_type=F32)


def _softplus(x):
    return jnp.maximum(x, 0.0) + jnp.log1p(jnp.exp(-jnp.abs(x)))


def _silu(x):
    return x * jax.nn.sigmoid(x)


def _gelu(x):
    return 0.5 * x * (1.0 + lax.erf(x * math.sqrt(0.5)))


def _rms(x):
    return x * lax.rsqrt(jnp.mean(x * x, axis=-1, keepdims=True) + EPS)


def _shift_rows(x, s, row):
    n = x.shape[0]
    if s == 0:
        return x
    y = pltpu.roll(x, (-s) % n, 0)
    ok = (row + s >= 0) & (row + s < n)
    return jnp.where(ok, y, 0.0)


def _dw_conv(x, w, row):
    y = _shift_rows(x, -2, row) * w[0:1, :]
    y = y + _shift_rows(x, -1, row) * w[1:2, :]
    y = y + x * w[2:3, :]
    y = y + _shift_rows(x, 1, row) * w[3:4, :]
    return y


def _mod_kernel(c_ref, w_ref, b_ref, o_ref):
    c = c_ref[...]
    o_ref[...] = jnp.dot(_silu(c), w_ref[...], precision=lax.Precision.HIGHEST,
                         preferred_element_type=F32) + b_ref[...]


def _modulation(cond, w_mod, b_mod):
    n_rows = cond.shape[0]
    tn = MOD_COLS
    n_out = N_MOD * D_MODEL
    return pl.pallas_call(
        _mod_kernel,
        grid=(DEPTH, n_out // tn),
        in_specs=[pl.BlockSpec((n_rows, D_MODEL), lambda l, j: (0, 0)),
                  pl.BlockSpec((None, D_MODEL, tn), lambda l, j: (l, 0, j)),
                  pl.BlockSpec((None, 1, tn), lambda l, j: (l, 0, j))],
        out_specs=pl.BlockSpec((None, n_rows, tn), lambda l, j: (l, 0, j)),
        out_shape=jax.ShapeDtypeStruct((DEPTH, n_rows, n_out), F32),
        compiler_params=_cparams(("parallel", "parallel")),
        name="modulation",
    )(cond, w_mod, b_mod.reshape(DEPTH, 1, n_out))


def _in_proj_kernel(x_ref, mod_ref, g_ref, w_ref, o_ref, h_scr):
    @pl.when(pl.program_id(1) == 0)
    def _():
        y = _rms(x_ref[...]) * g_ref[...]
        h_scr[...] = (y * (1.0 + mod_ref[1:2, :]) + mod_ref[0:1, :]).astype(BF16)

    o_ref[...] = jnp.dot(h_scr[...], w_ref[...], preferred_element_type=F32)


def _in_proj(x, mod, norm_g, w, layer, mod_row, tm, n_out, tn):
    t = x.shape[0]
    assert n_out % tn == 0 and n_out <= w.shape[-1]
    return pl.pallas_call(
        _in_proj_kernel,
        grid=(t // tm, n_out // tn),
        in_specs=[pl.BlockSpec((tm, D_MODEL), lambda i, j: (i, 0)),
                  pl.BlockSpec((None, None, N_MOD, D_MODEL), lambda i, j: (layer, mod_row(i), 0, 0)),
                  pl.BlockSpec((1, D_MODEL), lambda i, j: (0, 0)),
                  pl.BlockSpec((None, D_MODEL, tn), lambda i, j: (layer, 0, j))],
        out_specs=pl.BlockSpec((tm, tn), lambda i, j: (i, j)),
        out_shape=jax.ShapeDtypeStruct((t, n_out), F32),
        scratch_shapes=[pltpu.VMEM((tm, D_MODEL), BF16)],
        compiler_params=_cparams(("parallel", "arbitrary")),
        name="in_proj",
    )(x, mod, norm_g, w)


def _ret_kernel(*refs, seq_len, rope, hg):
    if rope:
        (q_ref, k_ref, v_ref, g_ref, gam_ref, ng_ref, s0_ref, cs_ref, sn_ref,
         o_ref, so_ref, of_scr, ob_scr) = refs
    else:
        (q_ref, k_ref, v_ref, g_ref, gam_ref, ng_ref, s0_ref,
         o_ref, so_ref, of_scr, ob_scr) = refs
    c = RET_CHUNK
    n_chunks = seq_len // c
    r = lax.broadcasted_iota(jnp.int32, (c, HEAD_DIM), 0).astype(F32)
    ci = lax.broadcasted_iota(jnp.int32, (c, c), 0)
    si = lax.broadcasted_iota(jnp.int32, (c, c), 1)
    dmat = (ci - si).astype(F32)
    scale = HEAD_DIM ** -0.5

    chains = [(hh, d) for hh in range(hg) for d in range(2)]
    idx = range(len(chains))
    dec, qsc, ksc, gch = [], [], [], []
    for hh, d in chains:
        lg = -_softplus(-gam_ref[hh, d:d + 1, :])
        if d == 0:
            dec.append(jnp.where(dmat >= 0, jnp.exp(lg * jnp.maximum(dmat, 0.0)), 0.0))
            qsc.append(jnp.exp(lg * (r + 1.0)))
            ksc.append(jnp.exp(lg * (c - 1.0 - r)))
        else:
            dec.append(jnp.where(dmat <= 0, jnp.exp(lg * jnp.maximum(-dmat, 0.0)), 0.0))
            qsc.append(jnp.exp(lg * (c - r)))
            ksc.append(jnp.exp(lg * r))
        gch.append(jnp.exp(lg * c))

    s = [s0_ref[d, hh] for hh, d in chains]
    for stp in range(n_chunks):
        sl = [pl.ds((stp if d == 0 else n_chunks - 1 - stp) * c, c) for _, d in chains]
        cs = [slice(hh * HEAD_DIM, (hh + 1) * HEAD_DIM) for hh, _ in chains]
        q = [q_ref[sl[i], cs[i]] for i in idx]
        k = [k_ref[sl[i], cs[i]] * scale for i in idx]
        v = [v_ref[sl[i], cs[i]] for i in idx]
        if rope:
            cos = [cs_ref[sl[i], :] for i in idx]
            sin = [sn_ref[sl[i], :] for i in idx]
            q = [q[i] * cos[i] + pltpu.roll(q[i], HEAD_DIM // 2, 1) * sin[i] for i in idx]
            k = [k[i] * cos[i] + pltpu.roll(k[i], HEAD_DIM // 2, 1) * sin[i] for i in idx]
        sc = [_mm_nt(q[i], k[i]) for i in idx]
        qs = [_mm(q[i] * qsc[i], s[i]) for i in idx]
        kv = [_mm_tn(k[i] * ksc[i], v[i]) for i in idx]
        oi = [_mm(sc[i] * dec[i], v[i]) for i in idx]
        s = [s[i] * gch[i] + kv[i] for i in idx]
        for i in idx:
            if chains[i][1] == 0:
                of_scr[sl[i], cs[i]] = oi[i] + qs[i]
            else:
                ob_scr[sl[i], cs[i]] = oi[i] + qs[i]
    for i in idx:
        so_ref[chains[i][1], chains[i][0]] = s[i]
    tot = of_scr[...] + ob_scr[...]
    g = g_ref[...]
    for hh in range(hg):
        cs1 = slice(hh * HEAD_DIM, (hh + 1) * HEAD_DIM)
        o_ref[:, cs1] = (_rms(tot[:, cs1]) * ng_ref[hh] * _silu(g[:, cs1])).astype(o_ref.dtype)


def _retention(proj, gam, norm_g, s0, layer, rope_tabs, n_seq, seq_len, row_off, hg):
    rope = rope_tabs is not None
    wide = hg * HEAD_DIM

    def col(c0):
        return pl.BlockSpec((seq_len, wide), lambda b, g: (row_off + b, c0 // hg + g))

    in_specs = [col(COL_RQ), col(COL_RK), col(COL_RV), col(COL_RG),
                pl.BlockSpec((hg, 2, HEAD_DIM), lambda b, g: (g, 0, 0)),
                pl.BlockSpec((hg, 1, HEAD_DIM), lambda b, g: (g, 0, 0)),
                _state_spec(s0, layer, hg)]
    args = [proj, proj, proj, proj, gam, norm_g, s0]
    if rope:
        tab = pl.BlockSpec((seq_len, HEAD_DIM), lambda b, g: (0, 0))
        in_specs += [tab, tab]
        args += list(rope_tabs)
    return pl.pallas_call(
        functools.partial(_ret_kernel, seq_len=seq_len, rope=rope, hg=hg),
        grid=(n_seq, N_HEADS // hg),
        in_specs=in_specs,
        out_specs=[pl.BlockSpec((seq_len, wide), lambda b, g: (b, g)),
                   pl.BlockSpec((None, 2, hg, HEAD_DIM, HEAD_DIM), lambda b, g: (b, 0, g, 0, 0))],
        out_shape=[jax.ShapeDtypeStruct((n_seq * seq_len, D_MODEL), BRANCH_DTYPE),
                   jax.ShapeDtypeStruct((n_seq, 2, N_HEADS, HEAD_DIM, HEAD_DIM), F32)],
        scratch_shapes=[pltpu.VMEM((seq_len, wide), F32), pltpu.VMEM((seq_len, wide), F32)],
        compiler_params=_cparams(("parallel", "parallel")),
        name="retention_rope" if rope else "retention",
    )(*args)


def _lru_kernel(x_ref, gate_ref, cw_ref, cb_ref, gw_ref, gb_ref, lam_ref, s0_ref, o_ref, so_ref, *, seq_len, nb):
    n = seq_len
    row = lax.broadcasted_iota(jnp.int32, (n, nb * HEAD_DIM), 0)
    xc = _dw_conv(x_ref[...], cw_ref[...], row) + cb_ref[...]
    lam = lam_ref[...]

    def gate(d, which):
        cols = [_mm(xc[:, s * HEAD_DIM:(s + 1) * HEAD_DIM], gw_ref[d, which, s]) for s in range(nb)]
        return jax.nn.sigmoid(jnp.concatenate(cols, axis=1) + gb_ref[d, which:which + 1, :])

    hs = []
    for d in range(2):
        r_gate = gate(d, 0)
        i_gate = gate(d, 1)
        log_a = -LRU_C * r_gate * _softplus(-lam[d:d + 1, :])
        a = jnp.exp(log_a)
        u = jnp.sqrt(-jnp.tanh(log_a) * (1.0 + a * a)) * i_gate * xc
        step = 1
        while step < n:
            if d == 0:
                ok = row >= step
                sh = step
            else:
                ok = row < n - step
                sh = n - step
            a_sh = jnp.where(ok, pltpu.roll(a, sh, 0), 1.0)
            u_sh = jnp.where(ok, pltpu.roll(u, sh, 0), 0.0)
            u = a * u_sh + u
            a = a * a_sh
            step *= 2
        h = u + a * s0_ref[d:d + 1, :]
        hs.append(h)
        so_ref[d:d + 1, :] = h[n - 1:n, :] if d == 0 else h[0:1, :]
    o_ref[...] = ((hs[0] + hs[1]) * _gelu(gate_ref[...])).astype(o_ref.dtype)


def _lru_blocks(seq_len):
    return max(1, min(N_HEADS, LRU_STEP_ROWS // seq_len))


def _rglru(proj, conv_w, conv_b, gate_w, gate_b, lam, s0, layer, n_seq, seq_len, row_off, nb):
    wide = nb * HEAD_DIM
    if layer is None:
        s0_spec = pl.BlockSpec((None, 2, wide), lambda b, n: (b, 0, n))
    else:
        s0_spec = pl.BlockSpec((None, None, 2, wide), lambda b, n: (b, layer, 0, n))
    return pl.pallas_call(
        functools.partial(_lru_kernel, seq_len=seq_len, nb=nb),
        grid=(n_seq, N_HEADS // nb),
        in_specs=[pl.BlockSpec((seq_len, wide), lambda b, n: (row_off + b, COL_LX // nb + n)),
                  pl.BlockSpec((seq_len, wide), lambda b, n: (row_off + b, COL_LG // nb + n)),
                  pl.BlockSpec((4, wide), lambda b, n: (0, n)),
                  pl.BlockSpec((1, wide), lambda b, n: (0, n)),
                  pl.BlockSpec((2, 2, nb, HEAD_DIM, HEAD_DIM), lambda b, n: (0, 0, n, 0, 0)),
                  pl.BlockSpec((2, 2, wide), lambda b, n: (0, 0, n)),
                  pl.BlockSpec((2, wide), lambda b, n: (0, n)),
                  s0_spec],
        out_specs=[pl.BlockSpec((seq_len, wide), lambda b, n: (b, n)),
                   pl.BlockSpec((None, 2, wide), lambda b, n: (b, 0, n))],
        out_shape=[jax.ShapeDtypeStruct((n_seq * seq_len, D_MODEL), BRANCH_DTYPE),
                   jax.ShapeDtypeStruct((n_seq, 2, D_MODEL), F32)],
        compiler_params=_cparams(("parallel", "parallel")),
        name="rglru",
    )(proj, proj, conv_w, conv_b, gate_w, gate_b, lam, s0)


def _dn_kernel(q_ref, k_ref, v_ref, z_ref, sm_ref, cwq_ref, cwk_ref, cwv_ref, par_ref, ng_ref, s0_ref,
               o_ref, so_ref, q_scr, k_scr, v_scr, c_scr, b_scr, of_scr, ob_scr, *, seq_len, hg):
    n = seq_len
    cc = DN_CHUNK
    sc = DN_SUPER
    n_super = n // sc
    head0 = pl.program_id(1) * hg
    roww = lax.broadcasted_iota(jnp.int32, (n, hg * HEAD_DIM), 0)
    row = lax.broadcasted_iota(jnp.int32, (n, HEAD_DIM), 0)
    lane = lax.broadcasted_iota(jnp.int32, (n, HEAD_DIM), 1)
    pos = row & (cc - 1)

    xq = _silu(_dw_conv(q_ref[...], cwq_ref[...], roww))
    xk = _silu(_dw_conv(k_ref[...], cwk_ref[...], roww))
    v_scr[...] = _silu(_dw_conv(v_ref[...], cwv_ref[...], roww))
    small = sm_ref[...]
    par = par_ref[...]
    g_all = -jnp.exp(par[0:1, :]) * _softplus(small + par[1:2, :])
    beta_all = jax.nn.sigmoid(small)
    cum_all = [g_all, g_all]
    step = 1
    while step < cc:
        cum_all[0] = cum_all[0] + jnp.where(pos >= step, pltpu.roll(cum_all[0], step, 0), 0.0)
        cum_all[1] = cum_all[1] + jnp.where(pos < cc - step, pltpu.roll(cum_all[1], n - step, 0), 0.0)
        step *= 2
    for hh in range(hg):
        cs = slice(hh * HEAD_DIM, (hh + 1) * HEAD_DIM)
        xqh, xkh = xq[:, cs], xk[:, cs]
        q_scr[:, cs] = xqh * lax.rsqrt(jnp.sum(xqh * xqh, axis=-1, keepdims=True) + EPS) * (HEAD_DIM ** -0.5)
        k_scr[:, cs] = xkh * lax.rsqrt(jnp.sum(xkh * xkh, axis=-1, keepdims=True) + EPS)
        head = head0 + hh
        for d in range(2):
            a_lane = lane == 2 * N_HEADS * d + head
            b_lane = lane == 2 * N_HEADS * d + N_HEADS + head
            cum = jnp.sum(jnp.where(a_lane, cum_all[d], 0.0), axis=-1, keepdims=True)
            beta = jnp.sum(jnp.where(b_lane, beta_all, 0.0), axis=-1, keepdims=True)
            c_scr[d, hh] = jnp.broadcast_to(cum, (n, HEAD_DIM))
            b_scr[d, hh] = jnp.broadcast_to(beta, (n, HEAD_DIM))

    ri = lax.broadcasted_iota(jnp.int32, (sc, sc), 0)
    cj = lax.broadcasted_iota(jnp.int32, (sc, sc), 1)
    sh = cc.bit_length() - 1
    same = (ri >> sh) == (cj >> sh)
    incl = (same & (ri >= cj), same & (ri <= cj))
    strict = (same & (ri > cj), same & (ri < cj))
    eye = jnp.where(ri == cj, 1.0, 0.0)
    base = SUBLANES.bit_length() - 1
    level = [(ri >> base) == (cj >> base)]
    for b in range(base + 1, sh + 1):
        level.append(((ri >> b) == (cj >> b)) & ((ri >> (b - 1)) != (cj >> (b - 1))))

    n_ch = sc // cc

    def super_chunks(chains):
        idx = range(len(chains))
        dd = [c[0] for c in chains]
        sl = [pl.ds(c[2], sc) for c in chains]
        cs = [slice(c[1] * HEAD_DIM, (c[1] + 1) * HEAD_DIM) for c in chains]
        q = [q_scr[sl[c], cs[c]] for c in idx]
        k = [k_scr[sl[c], cs[c]] for c in idx]
        v = [v_scr[sl[c], cs[c]] for c in idx]
        cum = [c_scr[dd[c], chains[c][1], sl[c], :] for c in idx]
        beta = [b_scr[dd[c], chains[c][1], sl[c], :] for c in idx]
        kk = [_mm_nt(k[c], k[c]) for c in idx]
        qk = [_mm_nt(q[c], k[c]) for c in idx]
        decay, x, attn = [], [], []
        for c in idx:
            cb = jnp.concatenate([cum[c], cum[c]], axis=1)
            diff = cb - cb.T
            dec = jnp.where(incl[dd[c]], jnp.exp(diff), 0.0)
            nb = jnp.concatenate([-beta[c], -beta[c]], axis=1)
            x.append(jnp.where(strict[dd[c]], kk[c] * nb * dec, 0.0))
            attn.append(qk[c] * dec)
        xp = [jnp.where(level[0], x[c], 0.0) for c in idx]
        p = [eye + xp[c] for c in idx]
        for _ in range(base - 1):
            xp = [_mm(xp[c], xp[c]) for c in idx]
            pm = [_mm(p[c], xp[c]) for c in idx]
            p = [p[c] + pm[c] for c in idx]
        for lv in range(1, len(level)):
            t1 = [_mm(jnp.where(level[lv], x[c], 0.0), p[c]) for c in idx]
            t2 = [_mm(p[c], t1[c]) for c in idx]
            p = [p[c] + t2[c] for c in idx]
        rhs = [jnp.concatenate([v[c] * beta[c], k[c] * beta[c] * jnp.exp(cum[c])], axis=1) for c in idx]
        sol = [_mm(p[c], rhs[c]) for c in idx]
        qd = [q[c] * jnp.exp(cum[c]) for c in idx]
        s = [so_ref[dd[c], chains[c][1]] for c in idx]
        v_new = [[None] * n_ch for _ in idx]
        o_inter = [[None] * n_ch for _ in idx]
        for stp in range(n_ch):
            ch = [stp if dd[c] == 0 else n_ch - 1 - stp for c in idx]
            rs = [slice(ch[c] * cc, (ch[c] + 1) * cc) for c in idx]
            ws = [_mm(sol[c][rs[c], HEAD_DIM:], s[c]) for c in idx]
            for c in idx:
                o_inter[c][ch[c]] = _mm(qd[c][rs[c], :], s[c])
            kt = []
            for c in idx:
                cum_c = cum[c][rs[c], :]
                tot = cum_c[cc - 1:cc, :] if dd[c] == 0 else cum_c[0:1, :]
                v_new[c][ch[c]] = sol[c][rs[c], :HEAD_DIM] - ws[c]
                kt.append((k[c][rs[c], :] * jnp.exp(tot - cum_c), jnp.exp(tot)))
            upd = [_mm_tn(kt[c][0], v_new[c][ch[c]]) for c in idx]
            s = [s[c] * kt[c][1] + upd[c] for c in idx]
        oi = [_mm(attn[c], jnp.concatenate(v_new[c], axis=0)) for c in idx]
        for c in idx:
            o = jnp.concatenate(o_inter[c], axis=0) + oi[c]
            if dd[c] == 0:
                of_scr[sl[c], cs[c]] = o
            else:
                ob_scr[sl[c], cs[c]] = o
            so_ref[dd[c], chains[c][1]] = s[c]

    so_ref[...] = s0_ref[...]

    def body(i, carry):
        fwd = pl.multiple_of(i * sc, sc)
        bwd = pl.multiple_of((n_super - 1 - i) * sc, sc)
        super_chunks([(d, hh, fwd if d == 0 else bwd) for hh in range(hg) for d in range(2)])
        return carry

    lax.fori_loop(0, n_super, body, 0)
    tot = of_scr[...] + ob_scr[...]
    z = z_ref[...]
    for hh in range(hg):
        cs = slice(hh * HEAD_DIM, (hh + 1) * HEAD_DIM)
        o_ref[:, cs] = (_rms(tot[:, cs]) * ng_ref[hh] * _silu(z[:, cs])).astype(o_ref.dtype)


def _state_spec(s0, layer, hg):
    if layer is None:
        return pl.BlockSpec((None, 2, hg, HEAD_DIM, HEAD_DIM), lambda b, g: (b, 0, g, 0, 0))
    return pl.BlockSpec((None, None, 2, hg, HEAD_DIM, HEAD_DIM), lambda b, g: (b, layer, 0, g, 0, 0))


def _deltanet(proj, tail, conv_w, par, norm_g, s0, layer, n_seq, seq_len, row_off, hg):
    wide = hg * HEAD_DIM

    def col(c0):
        return pl.BlockSpec((seq_len, wide), lambda b, g: (row_off + b, c0 // hg + g))

    def cw(c0):
        return pl.BlockSpec((4, wide), lambda b, g: (0, c0 // hg + g))

    vm = functools.partial(pltpu.VMEM, dtype=F32)
    return pl.pallas_call(
        functools.partial(_dn_kernel, seq_len=seq_len, hg=hg),
        grid=(n_seq, N_HEADS // hg),
        in_specs=[col(COL_DQ), col(COL_DK), col(COL_DV), col(COL_DZ),
                  pl.BlockSpec((seq_len, HEAD_DIM), lambda b, g: (row_off + b, COL_SMALL)),
                  cw(0), cw(N_HEADS), cw(2 * N_HEADS),
                  pl.BlockSpec((2, HEAD_DIM), lambda b, g: (0, 0)),
                  pl.BlockSpec((hg, 1, HEAD_DIM), lambda b, g: (g, 0, 0)),
                  _state_spec(s0, layer, hg)],
        out_specs=[pl.BlockSpec((seq_len, wide), lambda b, g: (b, g)),
                   pl.BlockSpec((None, 2, hg, HEAD_DIM, HEAD_DIM), lambda b, g: (b, 0, g, 0, 0))],
        out_shape=[jax.ShapeDtypeStruct((n_seq * seq_len, D_MODEL), BRANCH_DTYPE),
                   jax.ShapeDtypeStruct((n_seq, 2, N_HEADS, HEAD_DIM, HEAD_DIM), F32)],
        scratch_shapes=[vm((seq_len, wide)), vm((seq_len, wide)), vm((seq_len, wide)),
                        vm((2, hg, seq_len, HEAD_DIM)), vm((2, hg, seq_len, HEAD_DIM)),
                        vm((seq_len, wide)), vm((seq_len, wide))],
        compiler_params=_cparams(("parallel", "parallel"),
                                 VMEM_LIMIT_WIDE if seq_len * hg > 2048 else VMEM_LIMIT),
        name="deltanet",
    )(proj, proj, proj, proj, tail, conv_w, conv_w, conv_w, par, norm_g, s0)


def _merge_kernel(retc_ref, retl_ref, lruc_ref, lrul_ref, dnc_ref, dnl_ref, g0_ref, g1_ref, g2_ref, x_ref, mod_ref,
                  ng_ref, wbr_ref, wout_ref, xo_ref, h_ref, *, ctx_blocks):
    is_ctx = pl.program_id(0) < ctx_blocks
    ret = jnp.where(is_ctx, retc_ref[...], retl_ref[...])
    lru = jnp.where(is_ctx, lruc_ref[...], lrul_ref[...])
    dn = jnp.where(is_ctx, dnc_ref[...], dnl_ref[...])
    merged = jax.nn.sigmoid(g0_ref[...]) * _mm(ret, wbr_ref[0])
    merged = merged + jax.nn.sigmoid(g1_ref[...]) * _mm(lru, wbr_ref[1])
    merged = merged + jax.nn.sigmoid(g2_ref[...]) * _mm(dn, wbr_ref[2])
    x = x_ref[...] + mod_ref[2:3, :] * _mm(merged, wout_ref[...])
    xo_ref[...] = x
    h_ref[...] = (_rms(x) * ng_ref[...] * (1.0 + mod_ref[4:5, :]) + mod_ref[3:4, :]).T.astype(BF16)


def _merge(branches, tail, x, mod, norm_g, w_br, w_out, layer, mod_row, tm):
    t = x.shape[0]
    ctx_blocks = branches[0][0].shape[0] // tm
    row = pl.BlockSpec((tm, D_MODEL), lambda i: (i, 0))
    row_c = pl.BlockSpec((tm, D_MODEL), lambda i: (jnp.minimum(i, ctx_blocks - 1), 0))
    row_l = pl.BlockSpec((tm, D_MODEL), lambda i: (jnp.maximum(i - ctx_blocks, 0), 0))

    def gate(k):
        return pl.BlockSpec((tm, D_MODEL), lambda i: (i, k))

    return pl.pallas_call(
        functools.partial(_merge_kernel, ctx_blocks=ctx_blocks),
        grid=(t // tm,),
        in_specs=[row_c, row_l, row_c, row_l, row_c, row_l, gate(0), gate(1), gate(2), row,
                  pl.BlockSpec((None, None, N_MOD, D_MODEL), lambda i: (layer, mod_row(i), 0, 0)),
                  pl.BlockSpec((1, D_MODEL), lambda i: (0, 0)),
                  pl.BlockSpec((None, 3, D_MODEL, D_MODEL), lambda i: (layer, 0, 0, 0)),
                  pl.BlockSpec((None, D_MODEL, D_MODEL), lambda i: (layer, 0, 0))],
        out_specs=[row, pl.BlockSpec((D_MODEL, tm), lambda i: (0, i))],
        out_shape=[jax.ShapeDtypeStruct((t, D_MODEL), F32), jax.ShapeDtypeStruct((D_MODEL, t), BF16)],
        compiler_params=_cparams(("parallel",)),
        name="merge",
    )(*branches[0], *branches[1], *branches[2], tail, tail, tail, x, mod, norm_g, w_br, w_out)


def _router_kernel(h_ref, wq_ref, keys_ref, thr_ref, e0_ref, e1_ref, q_scr, top_scr, cand_scr):
    tb = h_ref.shape[1]
    q_scr[...] = jnp.dot(wq_ref[...], h_ref[...], preferred_element_type=F32)

    top_scr[...] = jnp.full(top_scr.shape, NEG_BIG, F32)
    sub_row = lax.broadcasted_iota(jnp.int32, (SUBLANES, tb), 0)

    def top_values(x, dst):
        for kk in range(TOP_N):
            m = jnp.max(x, axis=0, keepdims=True)
            top_scr[dst, kk:kk + 1, :] = m
            x = jnp.where(x >= m, NEG_BIG, x)

    def body(hd, carry):
        base = pl.multiple_of(hd * 2 * N_KEYS, 2 * N_KEYS)
        s0 = _mm(keys_ref[0], q_scr[pl.ds(base, N_KEYS), :])
        s1 = _mm(keys_ref[1], q_scr[pl.ds(base + N_KEYS, N_KEYS), :])
        top_values(s0, 0)
        top_values(s1, 1)
        cand_scr[0:TOP_PAD, :] = top_scr[0, 0:1, :] + top_scr[1]
        a1 = top_scr[1, 0:SUBLANES, :]
        for p in range(1, N_MULTI):
            keep = sub_row < TOP_N // (p + 1)
            cand_scr[TOP_PAD + (p - 1) * SUBLANES:TOP_PAD + p * SUBLANES, :] = jnp.where(
                keep, top_scr[0, p:p + 1, :] + a1, NEG_BIG)
        cand_scr[CAND_ROWS - SUBLANES:CAND_ROWS, :] = top_scr[0, N_MULTI:TOP_N, :] + top_scr[1, 0:1, :]
        x = cand_scr[...]
        m0 = jnp.max(x, axis=0, keepdims=True)
        z = jnp.zeros_like(m0)
        m = m0
        for kk in range(PEER_TOPK):
            if kk > 0:
                x = jnp.where(x >= m, NEG_BIG, x)
                m = jnp.max(x, axis=0, keepdims=True)
            z = z + jnp.exp(m - m0)
        m_next = jnp.max(jnp.where(x >= m, NEG_BIG, x), axis=0, keepdims=True)
        tau = 0.5 * (m + m_next)
        max1 = top_scr[1, 0:1, :]
        thr = jnp.exp(tau - s0 - max1) / z
        e0 = 0.5 * jnp.exp(s0 - top_scr[0, 0:1, :])
        e1 = jnp.exp(s1 - max1) / z
        for c in range(tb // HEAD_DIM):
            ls = slice(c * HEAD_DIM, (c + 1) * HEAD_DIM)
            thr_ref[hd, c] = thr[:, ls]
            e0_ref[hd, c] = e0[:, ls]
            e1_ref[hd, c] = e1[:, ls]
        return carry

    lax.fori_loop(0, N_HEADS, body, 0)


def _router(h2, wq_t, keys, layer, tb):
    t = h2.shape[1]
    out = pl.BlockSpec((N_HEADS, tb // HEAD_DIM, N_KEYS, HEAD_DIM), lambda i: (0, i, 0, 0))
    shp = jax.ShapeDtypeStruct((N_HEADS, t // HEAD_DIM, N_KEYS, HEAD_DIM), F32)
    return pl.pallas_call(
        _router_kernel,
        grid=(t // tb,),
        in_specs=[pl.BlockSpec((D_MODEL, tb), lambda i: (0, i)),
                  pl.BlockSpec((None, 2 * N_KEYS * N_HEADS, D_MODEL), lambda i: (layer, 0, 0)),
                  pl.BlockSpec((None, 2, N_KEYS, N_KEYS), lambda i: (layer, 0, 0, 0))],
        out_specs=[out, out, out],
        out_shape=[shp, shp, shp],
        scratch_shapes=[pltpu.VMEM((2 * N_KEYS * N_HEADS, tb), F32),
                        pltpu.VMEM((2, TOP_PAD, tb), F32),
                        pltpu.VMEM((CAND_ROWS, tb), F32)],
        compiler_params=_cparams(("parallel",)),
        name="peer_router",
    )(h2, wq_t, keys)


def _expert_kernel(h_ref, u_ref, vt_ref, thr_ref, e0_ref, e1_ref, x_ref, mod_ref, o_ref,
                   acc_scr, g_scr, act_scr, *, tile_e):
    j = pl.program_id(1)
    tb = h_ref.shape[1]
    n_sub = tile_e // N_KEYS
    n_lane = tb // HEAD_DIM

    @pl.when(j == 0)
    def _():
        acc_scr[...] = jnp.zeros_like(acc_scr)

    act = jnp.dot(u_ref[...], h_ref[...], preferred_element_type=F32)
    for c in range(n_lane):
        act_scr[c] = act[:, c * HEAD_DIM:(c + 1) * HEAD_DIM]

    i0 = pl.multiple_of(j * n_sub, n_sub)
    for c in range(n_lane):
        thr = [thr_ref[hd, c, pl.ds(i0, n_sub), :] for hd in range(N_HEADS)]
        e0 = [e0_ref[hd, c, pl.ds(i0, n_sub), :] for hd in range(N_HEADS)]
        for ii in range(n_sub):
            rs = slice(ii * N_KEYS, (ii + 1) * N_KEYS)
            wd = None
            for hd in range(N_HEADS):
                e1 = e1_ref[hd, c]
                w = jnp.where(e1 >= thr[hd][ii:ii + 1, :], e1 * e0[hd][ii:ii + 1, :], 0.0)
                wd = w if wd is None else wd + w
            x = act_scr[c, rs, :]
            g_scr[c, rs, :] = ((x * wd) * (1.0 + lax.erf(x * math.sqrt(0.5)))).astype(BF16)
    g = jnp.concatenate([g_scr[c] for c in range(n_lane)], axis=1)
    acc_scr[...] += jnp.dot(vt_ref[...], g, preferred_element_type=F32)

    @pl.when(j == pl.num_programs(1) - 1)
    def _():
        o_ref[...] = x_ref[...] + mod_ref[5:6, :] * acc_scr[...].T


def _experts(h2t, u_tab, vt_tab, routing, x, mod, layer, mod_row, tb, tile_e):
    t = h2t.shape[1]
    n_exp = u_tab.shape[1]
    n_lane = tb // HEAD_DIM
    rt = pl.BlockSpec((N_HEADS, n_lane, N_KEYS, HEAD_DIM), lambda i, j: (0, i, 0, 0))
    n_tiles = n_exp // tile_e
    return pl.pallas_call(
        functools.partial(_expert_kernel, tile_e=tile_e),
        grid=(t // tb, n_tiles),
        in_specs=[pl.BlockSpec((D_MODEL, tb), lambda i, j: (0, i)),
                  pl.BlockSpec((None, tile_e, D_MODEL), lambda i, j: (layer, j, 0)),
                  pl.BlockSpec((None, None, D_MODEL, tile_e), lambda i, j: (layer, j, 0, 0)),
                  rt, rt, rt,
                  pl.BlockSpec((tb, D_MODEL), lambda i, j: (i, 0)),
                  pl.BlockSpec((None, None, N_MOD, D_MODEL), lambda i, j: (layer, mod_row(i), 0, 0))],
        out_specs=pl.BlockSpec((tb, D_MODEL), lambda i, j: (i, 0)),
        out_shape=jax.ShapeDtypeStruct((t, D_MODEL), F32),
        scratch_shapes=[pltpu.VMEM((D_MODEL, tb), F32), pltpu.VMEM((n_lane, tile_e, HEAD_DIM), BF16),
                        pltpu.VMEM((n_lane, tile_e, HEAD_DIM), F32)],
        compiler_params=_cparams(("parallel", "arbitrary")),
        name="peer_experts",
    )(h2t, u_tab, vt_tab, *routing, x, mod)


def _final_norm_kernel(x_ref, g_ref, o_ref):
    o_ref[...] = _rms(x_ref[...]) * g_ref[...]


def _final_norm(x, g, tm):
    t = x.shape[0]
    row = pl.BlockSpec((tm, D_MODEL), lambda i: (i, 0))
    return pl.pallas_call(
        _final_norm_kernel, grid=(t // tm,),
        in_specs=[row, pl.BlockSpec((1, D_MODEL), lambda i: (0, 0))],
        out_specs=row, out_shape=jax.ShapeDtypeStruct((t, D_MODEL), F32),
        compiler_params=_cparams(("parallel",)), name="final_norm",
    )(x, g)


def _rope_tables(seq_len):
    rows = seq_len // GRID_W
    r = jnp.repeat(jnp.arange(rows, dtype=F32), GRID_W)
    col = jnp.tile(jnp.arange(GRID_W, dtype=F32), rows)
    nf = HEAD_DIM // 4
    inv = ROPE_BASE ** (-jnp.arange(nf, dtype=F32) / nf)
    ang = jnp.concatenate([r[:, None] * inv, col[:, None] * inv], axis=-1)
    cos, sin = jnp.cos(ang), jnp.sin(ang)
    return jnp.concatenate([cos, cos], axis=-1), jnp.concatenate([-sin, sin], axis=-1)


def _lanes(a):
    return jnp.broadcast_to(jnp.moveaxis(a, -1, 0)[..., None], (a.shape[-1],) + a.shape[:-1] + (HEAD_DIM,))


def kernel(x_prompt, x_sample, c, state_ret, state_lru, state_dn, c_ctx, w_mod, b_mod, norm1_g, norm2_g, w_in, ret_gamma_logit, ret_norm_g, lru_conv_w, lru_conv_b, lru_gate_w, lru_gate_b, lru_lambda, dn_conv_w, dn_a_log, dn_dt_bias, dn_norm_g, w_br, w_out, peer_w_q, peer_sub_keys, peer_u, peer_v, final_norm_g):
    n_ctx, l_ctx, _ = x_prompt.shape
    n_lat, l_lat, _ = x_sample.shape
    t_ctx, t_lat = n_ctx * l_ctx, n_lat * l_lat
    assert t_ctx % l_lat == 0
    tb = TOKEN_BLOCK
    assert l_lat % tb == 0 and t_ctx % tb == 0
    ctx_blocks, per_seq = t_ctx // tb, l_lat // tb

    def mod_row(i):
        return jnp.where(i < ctx_blocks, 0, 1 + (i - ctx_blocks) // per_seq)

    tm = MERGE_ROWS
    ctx_blocks_m, per_seq_m = t_ctx // tm, l_lat // tm

    def mod_row_m(i):
        return jnp.where(i < ctx_blocks_m, 0, 1 + (i - ctx_blocks_m) // per_seq_m)

    x = jnp.concatenate([x_prompt.reshape(t_ctx, D_MODEL), x_sample.reshape(t_lat, D_MODEL)], axis=0)
    n_cond = 16
    cond = jnp.zeros((n_cond, D_MODEL), F32).at[0].set(c_ctx).at[1:1 + n_lat].set(c)
    mods = _modulation(cond, w_mod, b_mod).reshape(DEPTH, n_cond, N_MOD, D_MODEL)

    w_main = w_in[:, :, :N_MAIN].astype(BF16)
    w_tail = jnp.concatenate(
        [w_in[:, :, N_MAIN + N_SMALL:], w_in[:, :, N_MAIN:N_MAIN + N_SMALL],
         jnp.zeros((DEPTH, D_MODEL, HEAD_DIM - N_SMALL), F32)], axis=-1).astype(BF16)
    w_br_b, w_out_b = w_br.astype(BF16), w_out.astype(BF16)
    wq_t = jnp.swapaxes(peer_w_q, 1, 2).astype(BF16)
    keys_b = peer_sub_keys.astype(BF16)
    u_b = peer_u.astype(BF16)
    vt_b = jnp.swapaxes(peer_v.reshape(DEPTH, N_EXPERTS // EXPERT_TILE, EXPERT_TILE, D_MODEL), 2, 3).astype(BF16)
    gam = _lanes(ret_gamma_logit)
    lane_pad = jnp.zeros((DEPTH, 2, N_HEADS), F32)
    dn_ab = jnp.stack([dn_a_log, dn_dt_bias], axis=1)
    dn_par = jnp.concatenate([dn_ab[:, :, 0], lane_pad, dn_ab[:, :, 1], lane_pad,
                              jnp.zeros((DEPTH, 2, HEAD_DIM - N_SMALL), F32)], axis=-1)
    rope_tabs = _rope_tables(l_lat)
    zero_ret = jnp.zeros((n_ctx, 2, N_HEADS, HEAD_DIM, HEAD_DIM), F32)
    zero_lru = jnp.zeros((n_ctx, 2, D_MODEL), F32)

    row_off_lat = t_ctx // l_lat

    ret_states, lru_states, dn_states = [], [], []
    for l in range(DEPTH):
        proj = _in_proj(x, mods, norm1_g[l][None], w_main, l,
                        lambda i: jnp.where(i < row_off_lat, 0, 1 + i - row_off_lat), l_lat, N_MAIN, IN_PROJ_COLS)
        tail = _in_proj(x, mods, norm1_g[l][None], w_tail, l, mod_row, tb, N_TAIL, N_TAIL)

        ng_ret = ret_norm_g[l][:, None, :]
        o_ret_c, s_ret = _retention(proj, gam[:, l], ng_ret, zero_ret, None, None, n_ctx, l_ctx, 0, MIXER_HEADS)
        o_ret_l, _ = _retention(proj, gam[:, l], ng_ret, state_ret, l, rope_tabs, n_lat, l_lat, row_off_lat, MIXER_HEADS)

        lru_args = (lru_conv_w[l], lru_conv_b[l][None], lru_gate_w[l], lru_gate_b[l], lru_lambda[l])
        o_lru_c, s_lru = _rglru(proj, *lru_args, zero_lru, None, n_ctx, l_ctx, 0, _lru_blocks(l_ctx))
        o_lru_l, _ = _rglru(proj, *lru_args, state_lru, l, n_lat, l_lat, row_off_lat, _lru_blocks(l_lat))

        ng_dn = dn_norm_g[l][:, None, :]
        dn_args = (proj, tail, dn_conv_w[l], dn_par[l], ng_dn)
        o_dn_c, s_dn = _deltanet(*dn_args, zero_ret, None, n_ctx, l_ctx, 0, MIXER_HEADS)
        o_dn_l, _ = _deltanet(*dn_args, state_dn, l, n_lat, l_lat, row_off_lat, MIXER_HEADS)

        branches = ((o_ret_c, o_ret_l), (o_lru_c, o_lru_l), (o_dn_c, o_dn_l))
        x, h2 = _merge(branches, tail, x, mods, norm2_g[l][None], w_br_b, w_out_b, l, mod_row_m, tm)
        routing = _router(h2, wq_t, keys_b, l, tb)
        x = _experts(h2, u_b, vt_b, routing, x, mods, l, mod_row, tb, EXPERT_TILE)

        ret_states.append(s_ret)
        lru_states.append(s_lru)
        dn_states.append(s_dn)

    y = _final_norm(x, final_norm_g[None], tb)
    y_prompt = y[:t_ctx].reshape(n_ctx, l_ctx, D_MODEL)
    y_sample = y[t_ctx:].reshape(n_lat, l_lat, D_MODEL)
    return (y_prompt, y_sample, jnp.stack(ret_states, axis=1), jnp.stack(lru_states, axis=1),
            jnp.stack(dn_states, axis=1))
```
